```python
import math
import jax
import jax.numpy as jnp
from jax import lax
import numpy as np

D_MODEL = 2048
BATCH = 4
SEQ = 2048
DEPTH = 4
DEC_BATCH = 8
DEC_SEQ = 8
PAST_LEN = 16384
PAGE_SIZE = 128

HEAD_DIM = 128
NSA_HEADS = 8
NSA_KV_HEADS = 2
CMP_BLOCK = 32
CMP_STRIDE = 16
SEL_BLOCK = 64
N_SEL = 16
WINDOW = 512
BAND_BLOCK = 128
SEL_Q_CHUNK = 64
RET_HEADS = 4
HGRN_HEADS = 4
REC_CHUNK = 64
D_FF = 5632
PLE_DIM = 256
NSA_WIDTH = NSA_HEADS * HEAD_DIM
RET_WIDTH = RET_HEADS * HEAD_DIM
HGRN_WIDTH = HGRN_HEADS * HEAD_DIM
MIX_WIDTH = NSA_WIDTH + RET_WIDTH + HGRN_WIDTH
IN_SIZES = (NSA_WIDTH, 6 * NSA_KV_HEADS * HEAD_DIM, 3 * NSA_HEADS, 4 * RET_WIDTH, 4 * HGRN_WIDTH, 3 * D_MODEL)
IN_WIDTH = sum(IN_SIZES)
NEG_INF = -1e30
FORCE_SCORE = 1e6
EPS = 1e-6

kernel_name = 'nsa_retention_hgrn2_hybrid_step'


def rms_norm(x, gain):
    xf = x.astype(jnp.float32)
    y = xf * lax.rsqrt(jnp.mean(xf * xf, axis=-1, keepdims=True) + EPS)
    return (y * gain.astype(jnp.float32)).astype(x.dtype)


def swiglu(x, w_gu, w_down):
    g, v = jnp.split(x @ w_gu, 2, axis=-1)
    return (jax.nn.silu(g) * v) @ w_down


def split_columns(z):
    out, start = [], 0
    for s in IN_SIZES:
        out.append(z[..., start:start + s])
        start += s
    return out


def alibi_slopes():
    h = jnp.arange(1, NSA_HEADS + 1, dtype=jnp.float32)
    return jnp.exp2(-8.0 * h / NSA_HEADS).reshape(NSA_KV_HEADS, NSA_HEADS // NSA_KV_HEADS)


def gqa_attend(q, k, v, allowed, dist, slopes):
    n, tq, h, hd = q.shape
    g = k.shape[2]
    qg = q.reshape(n, tq, g, h // g, hd)
    s = jnp.einsum('nqgrd,nkgd->ngrqk', qg, k, preferred_element_type=jnp.float32) * (hd ** -0.5)
    s = s - slopes[None, :, :, None, None] * dist[:, None, None].astype(jnp.float32)
    s = jnp.where(allowed[:, None, None], s, NEG_INF)
    p = jax.nn.softmax(s, axis=-1)
    p = jnp.where(jnp.any(allowed, axis=-1)[:, None, None, :, None], p, 0.0)
    o = jnp.einsum('ngrqk,nkgd->nqgrd', p.astype(v.dtype), v)
    return o.reshape(n, tq, h, hd), p


def compress(x, pos_w, w_phi):
    n, L, g, hd = x.shape
    n16 = L // CMP_STRIDE
    parts = CMP_BLOCK // CMP_STRIDE
    n_cmp = n16 - parts + 1
    xs = x[:, :n16 * CMP_STRIDE].reshape(n, n16, CMP_STRIDE, g, hd)
    agg = 0.0
    for m in range(parts):
        agg = agg + jnp.einsum('bnjgd,jd->bngd', xs[:, m:m + n_cmp], pos_w[m * CMP_STRIDE:(m + 1) * CMP_STRIDE])
    return jnp.einsum('bngd,de->bnge', agg, w_phi)


def compressed_branch(q, k_raw, v_raw, q_pos, cmp_pos, cmp_w, gain_kc, slopes):
    kc = rms_norm(compress(k_raw, cmp_pos[0], cmp_w[0]), gain_kc)
    vc = compress(v_raw, cmp_pos[1], cmp_w[1])
    n_cmp = kc.shape[1]
    end = jnp.arange(n_cmp, dtype=jnp.int32) * CMP_STRIDE + CMP_BLOCK - 1
    dist = q_pos[:, None] - end[None, :]
    allowed = dist >= 0
    o, p = gqa_attend(q, kc, vc, allowed[None], dist[None], slopes)
    return o, p.sum(axis=2)


def select_blocks(imp, q_pos, L):
    n_cmp = imp.shape[-1]
    n_slc = -(-L // SEL_BLOCK)
    c_start = jnp.arange(n_cmp) * CMP_STRIDE
    s_start = jnp.arange(n_slc) * SEL_BLOCK
    overlap = ((c_start[:, None] < s_start[None, :] + SEL_BLOCK)
               & (c_start[:, None] + CMP_BLOCK > s_start[None, :])).astype(jnp.float32)
    score = jnp.einsum('ngti,ij->ngtj', imp, overlap)
    cur = q_pos // SEL_BLOCK
    j = jnp.arange(n_slc)[None, :]
    forced = (j == 0) | (j == cur[:, None]) | (j == cur[:, None] - 1)
    reach = s_start[None, :] <= q_pos[:, None]
    score = jnp.where(forced, FORCE_SCORE, score)
    score = jnp.where(reach, score, -1.0)
    top, idx = lax.top_k(score, min(N_SEL, n_slc))
    return idx, top >= 0.0


def select_attend(q, kb, vb, idx, valid, q_pos, slopes):
    n, tc, h, hd = q.shape
    g = kb.shape[1]
    ni = jnp.arange(n)[:, None, None, None]
    gi = jnp.arange(g)[None, :, None, None]
    kg = kb[ni, gi, idx]
    vg = vb[ni, gi, idx]
    qg = q.reshape(n, tc, g, h // g, hd)
    s = jnp.einsum('ntgrd,ngtksd->ngrtks', qg, kg, preferred_element_type=jnp.float32) * (hd ** -0.5)
    k_pos = idx[..., None] * SEL_BLOCK + jnp.arange(SEL_BLOCK)
    dist = q_pos[None, None, :, None, None] - k_pos
    allowed = valid[..., None] & (dist >= 0)
    s = s - slopes[None, :, :, None, None, None] * dist[:, :, None].astype(jnp.float32)
    s = jnp.where(allowed[:, :, None], s, NEG_INF)
    k_sel = idx.shape[-1]
    p = jax.nn.softmax(s.reshape(n, g, h // g, tc, k_sel * SEL_BLOCK), axis=-1).reshape(s.shape)
    o = jnp.einsum('ngrtks,ngtksd->ntgrd', p.astype(vg.dtype), vg)
    return o.reshape(n, tc, h, hd)


def selection_branch(q, k, v, idx, valid, q_pos, slopes):
    n, L, g, hd = k.shape
    t, h = q.shape[1], q.shape[2]
    n_slc = -(-L // SEL_BLOCK)
    pad = n_slc * SEL_BLOCK - L
    kb = jnp.pad(k, ((0, 0), (0, pad), (0, 0), (0, 0))).reshape(n, n_slc, SEL_BLOCK, g, hd).transpose(0, 3, 1, 2, 4)
    vb = jnp.pad(v, ((0, 0), (0, pad), (0, 0), (0, 0))).reshape(n, n_slc, SEL_BLOCK, g, hd).transpose(0, 3, 1, 2, 4)
    qc = math.gcd(t, SEL_Q_CHUNK)
    nc = t // qc
    if nc == 1:
        return select_attend(q, kb, vb, idx, valid, q_pos, slopes)
    k_sel = idx.shape[-1]
    qs = q.reshape(n, nc, qc, h, hd).swapaxes(0, 1)
    idxs = idx.reshape(n, g, nc, qc, k_sel).transpose(2, 0, 1, 3, 4)
    vals = valid.reshape(n, g, nc, qc, k_sel).transpose(2, 0, 1, 3, 4)
    poss = q_pos.reshape(nc, qc)
    out = lax.map(lambda a: select_attend(a[0], kb, vb, a[1], a[2], a[3], slopes), (qs, idxs, vals, poss))
    return out.swapaxes(0, 1).reshape(n, t, h, hd)


def window_banded(q, k, v, slopes):
    n, t, h, hd = q.shape
    g = k.shape[2]
    nb = t // BAND_BLOCK
    span = WINDOW + BAND_BLOCK
    kp = jnp.pad(k, ((0, 0), (WINDOW, 0), (0, 0), (0, 0)))
    vp = jnp.pad(v, ((0, 0), (WINDOW, 0), (0, 0), (0, 0)))
    gidx = jnp.arange(nb)[:, None] * BAND_BLOCK + jnp.arange(span)[None, :]
    kband = kp[:, gidx].reshape(n * nb, span, g, hd)
    vband = vp[:, gidx].reshape(n * nb, span, g, hd)
    q_pos = jnp.arange(nb)[:, None] * BAND_BLOCK + jnp.arange(BAND_BLOCK)[None, :]
    k_pos = gidx - WINDOW
    dist = q_pos[:, :, None] - k_pos[:, None, :]
    allowed = (k_pos[:, None, :] >= 0) & (dist >= 0) & (dist <= WINDOW)
    allowed = jnp.broadcast_to(allowed[None], (n,) + allowed.shape).reshape(n * nb, BAND_BLOCK, span)
    dist = jnp.broadcast_to(dist[None], (n,) + dist.shape).reshape(n * nb, BAND_BLOCK, span)
    o, _ = gqa_attend(q.reshape(n * nb, BAND_BLOCK, h, hd), kband, vband, allowed, dist, slopes)
    return o.reshape(n, t, h, hd)


def to_chunks(x, c):
    n, t, h, d = x.shape
    return x.astype(jnp.float32).reshape(n, t // c, c, h, d).transpose(1, 0, 3, 2, 4)


def from_chunks(o):
    nc, n, h, c, d = o.shape
    return o.transpose(1, 0, 3, 2, 4).reshape(n, nc * c, h, d)


def retention(q, k, v, s0):
    n, t, h, dk = q.shape
    c = math.gcd(t, REC_CHUNK)
    lg = jnp.log1p(-jnp.exp2(-5.0 - jnp.arange(h, dtype=jnp.float32)))
    i = jnp.arange(c, dtype=jnp.float32)
    diff = i[:, None] - i[None, :]
    dmat = jnp.where(diff >= 0, jnp.exp(lg[:, None, None] * jnp.maximum(diff, 0.0)), 0.0)
    q_dec = jnp.exp(lg[:, None] * (i + 1.0))[..., None]
    k_dec = jnp.exp(lg[:, None] * (c - 1.0 - i))[..., None]
    c_dec = jnp.exp(lg * c)[:, None, None]

    def step(s, inp):
        qc, kc, vc = inp
        a = jnp.einsum('nhtd,nhsd->nhts', qc, kc) * dmat
        o = jnp.einsum('nhts,nhse->nhte', a, vc) + jnp.einsum('nhtd,nhde->nhte', qc * q_dec, s)
        s = s * c_dec + jnp.einsum('nhsd,nhse->nhde', kc * k_dec, vc)
        return s, o

    s, o = lax.scan(step, s0.astype(jnp.float32), (to_chunks(q, c), to_chunks(k, c), to_chunks(v, c)))
    return from_chunks(o), s


def hgrn2(q, logf, k, v, s0):
    n, t, h, dk = q.shape
    c = math.gcd(t, REC_CHUNK)
    tri = jnp.tril(jnp.ones((c, c), dtype=bool))[None, None, :, :, None]

    def step(s, inp):
        qc, lfc, kc, vc = inp
        b = jnp.cumsum(lfc, axis=2)
        rel = jnp.where(tri, b[:, :, :, None, :] - b[:, :, None, :, :], NEG_INF)
        a = jnp.einsum('nhtk,nhsk,nhtsk->nhts', qc, kc, jnp.exp(rel))
        o = jnp.einsum('nhts,nhsv->nhtv', a, vc) + jnp.einsum('nhtk,nhkv->nhtv', qc * jnp.exp(b), s)
        bl = b[:, :, -1:, :]
        s = jnp.exp(bl[:, :, 0, :])[..., None] * s + jnp.einsum('nhsk,nhsv->nhkv', kc * jnp.exp(bl - b), vc)
        return s, o

    s, o = lax.scan(step, s0.astype(jnp.float32),
                    (to_chunks(q, c), to_chunks(logf, c), to_chunks(k, c), to_chunks(v, c)))
    return from_chunks(o), s


def token_mixing(u, w_in, qk_gain, cmp_pos, cmp_w, out_gain, lb, w_branch, w_o, q_pos, past):
    n, t, _ = u.shape
    zq, zkv, zg, zr, zh, zm = split_columns(u @ w_in)
    slopes = alibi_slopes()
    q = rms_norm(zq.reshape(n, t, NSA_HEADS, HEAD_DIM), qk_gain[0])
    kv = zkv.reshape(n, t, 6, NSA_KV_HEADS, HEAD_DIM)
    k_slc = rms_norm(kv[:, :, 2], qk_gain[1])
    k_win = rms_norm(kv[:, :, 4], qk_gain[2])
    kv_rows = jnp.stack([kv[:, :, 0], kv[:, :, 1], k_slc, kv[:, :, 3]], axis=2)
    win_rows = jnp.stack([k_win, kv[:, :, 5]], axis=2)
    if past is None:
        full = kv_rows
        s_ret0 = jnp.zeros((n, RET_HEADS, HEAD_DIM, HEAD_DIM), jnp.float32)
        s_hgrn0 = jnp.zeros((n, HGRN_HEADS, HEAD_DIM, HEAD_DIM), jnp.float32)
    else:
        kv_past, win_buf, s_ret0, s_hgrn0, past_len = past
        full = jnp.concatenate([kv_past.astype(kv_rows.dtype), kv_rows], axis=1)
    o_cmp, imp = compressed_branch(q, full[:, :, 0], full[:, :, 1], q_pos, cmp_pos, cmp_w, qk_gain[3], slopes)
    idx, valid = select_blocks(imp, q_pos, full.shape[1])
    o_slc = selection_branch(q, full[:, :, 2], full[:, :, 3], idx, valid, q_pos, slopes)
    if past is None:
        o_win = window_banded(q, k_win, kv[:, :, 5], slopes)
        win_state = win_rows[:, t - min(WINDOW, t):]
    else:
        win_all = jnp.concatenate([win_buf.astype(win_rows.dtype), win_rows], axis=1)
        wb = win_buf.shape[1]
        k_pos = past_len - wb + jnp.arange(wb + t, dtype=jnp.int32)
        dist = q_pos[:, None] - k_pos[None, :]
        allowed = (dist >= 0) & (dist <= WINDOW)
        o_win, _ = gqa_attend(q, win_all[:, :, 0], win_all[:, :, 1], allowed[None], dist[None], slopes)
        win_state = win_all[:, wb + t - min(WINDOW, wb + t):]
    gate = jax.nn.sigmoid(zg.astype(jnp.float32)).reshape(n, t, NSA_HEADS, 3, 1)
    o_nsa = (gate[:, :, :, 0] * o_cmp + gate[:, :, :, 1] * o_slc + gate[:, :, :, 2] * o_win).astype(u.dtype)
    zr = zr.reshape(n, t, 4, RET_HEADS, HEAD_DIM)
    o_r, s_ret = retention(zr[:, :, 0], zr[:, :, 1] * (HEAD_DIM ** -0.5), zr[:, :, 2], s_ret0)
    o_r = rms_norm(o_r, out_gain[0].reshape(RET_HEADS, HEAD_DIM)) * jax.nn.silu(zr[:, :, 3].astype(jnp.float32))
    zh = zh.reshape(n, t, 4, HGRN_HEADS, HEAD_DIM)
    lbh = lb.reshape(HGRN_HEADS, HEAD_DIM)
    f = lbh + (1.0 - lbh) * jax.nn.sigmoid(zh[:, :, 1].astype(jnp.float32))
    o_h, s_hgrn = hgrn2(zh[:, :, 0], jnp.log(f), 1.0 - f, zh[:, :, 2], s_hgrn0)
    o_h = rms_norm(o_h, out_gain[1].reshape(HGRN_HEADS, HEAD_DIM)) * jax.nn.silu(zh[:, :, 3].astype(jnp.float32))
    mg = jax.nn.sigmoid(zm.reshape(n, t, 3, D_MODEL))
    ya = o_nsa.reshape(n, t, NSA_WIDTH) @ w_branch[:NSA_WIDTH]
    yb = o_r.reshape(n, t, RET_WIDTH).astype(u.dtype) @ w_branch[NSA_WIDTH:NSA_WIDTH + RET_WIDTH]
    yc = o_h.reshape(n, t, HGRN_WIDTH).astype(u.dtype) @ w_branch[NSA_WIDTH + RET_WIDTH:]
    mixed = (mg[:, :, 0] * ya + mg[:, :, 1] * yb + mg[:, :, 2] * yc).astype(u.dtype)
    return mixed @ w_o, kv_rows, win_state, s_ret, s_hgrn


def run_layer(x, p_l, li, norms, w_in, qk_gain, cmp_pos, cmp_w, out_gain, lb_all, w_branch, w_o,
              w_gu, w_down, w_ple_proj, w_ple_gate, q_pos, past):
    h = x + 0.5 * swiglu(rms_norm(x, norms[li, 0]), w_gu[li, 0], w_down[li, 0])
    mix, kv_rows, win_state, s_ret, s_hgrn = token_mixing(
        rms_norm(h, norms[li, 1]), w_in[li], qk_gain[li], cmp_pos[li], cmp_w[li], out_gain[li],
        lb_all[li], w_branch[li], w_o[li], q_pos, past)
    h = h + mix
    h = h + 0.5 * swiglu(rms_norm(h, norms[li, 2]), w_gu[li, 1], w_down[li, 1])
    gate = jax.nn.sigmoid(rms_norm(h, norms[li, 3]) @ w_ple_gate[li])
    h = h + gate * (p_l @ w_ple_proj[li])
    return h, kv_rows, win_state, s_ret, s_hgrn


def setup_inputs(seed: int = 0) -> dict:
    key = jax.random.key(seed)
    ks = jax.random.split(key, 24)
    f32 = jnp.float32
    n_pages = PAST_LEN // PAGE_SIZE
    n_pool = (DEC_BATCH * n_pages * 5) // 4
    w_buf = min(WINDOW, PAST_LEN)

    def nrm(k, shape, scale):
        return jax.random.normal(k, shape, f32) * scale

    perm = jax.random.permutation(ks[6], n_pool)
    return {
        'x_prompt': nrm(ks[0], (BATCH, SEQ, D_MODEL), 1.0),
        'x_sample': nrm(ks[1], (DEC_BATCH, DEC_SEQ, D_MODEL), 1.0),
        'cache_kv': nrm(ks[2], (DEPTH, n_pool, PAGE_SIZE, 4, NSA_KV_HEADS, HEAD_DIM), 1.0),
        'cache_win': nrm(ks[3], (DEPTH, DEC_BATCH, w_buf, 2, NSA_KV_HEADS, HEAD_DIM), 1.0),
        'state_ret': nrm(ks[4], (DEPTH, DEC_BATCH, RET_HEADS, HEAD_DIM, HEAD_DIM), 1.0),
        'state_hgrn': nrm(ks[5], (DEPTH, DEC_BATCH, HGRN_HEADS, HEAD_DIM, HEAD_DIM), 0.5),
        'page_table': perm[:DEC_BATCH * n_pages].reshape(DEC_BATCH, n_pages).astype(jnp.int32),
        'p_prompt': nrm(ks[7], (DEPTH, BATCH, SEQ, PLE_DIM), 1.0),
        'p_sample': nrm(ks[8], (DEPTH, DEC_BATCH, DEC_SEQ, PLE_DIM), 1.0),
        'norms': 1.0 + nrm(ks[9], (DEPTH, 4, D_MODEL), 0.02),
        'w_in': nrm(ks[10], (DEPTH, D_MODEL, IN_WIDTH), D_MODEL ** -0.5),
        'qk_gain': 1.0 + nrm(ks[11], (DEPTH, 4, HEAD_DIM), 0.02),
        'cmp_pos': nrm(ks[12], (DEPTH, 2, CMP_BLOCK, HEAD_DIM), CMP_BLOCK ** -0.5),
        'cmp_w': nrm(ks[13], (DEPTH, 2, HEAD_DIM, HEAD_DIM), HEAD_DIM ** -0.5),
        'out_gain': 1.0 + nrm(ks[14], (DEPTH, 2, RET_WIDTH), 0.02),
        'hgrn_lb': nrm(ks[15], (DEPTH, HGRN_WIDTH), 1.0),
        'w_branch': nrm(ks[16], (DEPTH, MIX_WIDTH, D_MODEL), NSA_WIDTH ** -0.5),
        'w_o': nrm(ks[17], (DEPTH, D_MODEL, D_MODEL), D_MODEL ** -0.5),
        'w_gu': nrm(ks[18], (DEPTH, 2, D_MODEL, 2 * D_FF), D_MODEL ** -0.5),
        'w_down': nrm(ks[19], (DEPTH, 2, D_FF, D_MODEL), D_FF ** -0.5),
        'w_ple_proj': nrm(ks[20], (DEPTH, PLE_DIM, D_MODEL), PLE_DIM ** -0.5),
        'w_ple_gate': nrm(ks[21], (DEPTH, D_MODEL, D_MODEL), D_MODEL ** -0.5),
    }


def reference(x_prompt, x_sample, cache_kv, cache_win, state_ret, state_hgrn, page_table, p_prompt, p_sample,
              norms, w_in, qk_gain, cmp_pos, cmp_w, out_gain, hgrn_lb, w_branch, w_o, w_gu, w_down,
              w_ple_proj, w_ple_gate):
    n_dec, n_pages = page_table.shape
    page = cache_kv.shape[2]
    past_len = n_pages * page
    pos_p = jnp.arange(x_prompt.shape[1], dtype=jnp.int32)
    pos_s = past_len + jnp.arange(x_sample.shape[1], dtype=jnp.int32)
    lb_soft = jax.nn.softmax(hgrn_lb.astype(jnp.float32), axis=0)
    lb_all = jnp.cumsum(lb_soft, axis=0) - lb_soft[0:1]
    hp, hs = x_prompt, x_sample
    kv_p, kv_s, win_p, win_s, ret_p, ret_s, hg_p, hg_s = [], [], [], [], [], [], [], []
    for li in range(DEPTH):
        hp, a, b, c, d = run_layer(hp, p_prompt[li], li, norms, w_in, qk_gain, cmp_pos, cmp_w, out_gain, lb_all,
                                   w_branch, w_o, w_gu, w_down, w_ple_proj, w_ple_gate, pos_p, None)
        kv_p.append(a); win_p.append(b); ret_p.append(c); hg_p.append(d)
        kv_past = cache_kv[li][page_table].reshape(n_dec, past_len, 4, NSA_KV_HEADS, HEAD_DIM)
        past = (kv_past, cache_win[li], state_ret[li], state_hgrn[li], past_len)
        hs, a, b, c, d = run_layer(hs, p_sample[li], li, norms, w_in, qk_gain, cmp_pos, cmp_w, out_gain, lb_all,
                                   w_branch, w_o, w_gu, w_down, w_ple_proj, w_ple_gate, pos_s, past)
        kv_s.append(a); win_s.append(b); ret_s.append(c); hg_s.append(d)
    return (hp, hs, jnp.stack(kv_p), jnp.stack(kv_s), jnp.stack(win_p), jnp.stack(win_s),
            jnp.stack(ret_p), jnp.stack(ret_s), jnp.stack(hg_p), jnp.stack(hg_s))
```

```python
import functools
import math

import jax
import jax.numpy as jnp
import numpy as np
from jax import lax
from jax.experimental import pallas as pl
from jax.experimental.pallas import tpu as pltpu

F32 = jnp.float32
BF16 = jnp.bfloat16

D_MODEL = 2048
HEAD_DIM = 128
NSA_HEADS = 8
NSA_KV_HEADS = 2
HEADS_PER_GROUP = NSA_HEADS // NSA_KV_HEADS
CMP_BLOCK = 32
CMP_STRIDE = 16
SEL_BLOCK = 64
N_SEL = 16
WINDOW = 512
RET_HEADS = 4
HGRN_HEADS = 4
D_FF = 5632
PLE_DIM = 256
NSA_WIDTH = NSA_HEADS * HEAD_DIM
RET_WIDTH = RET_HEADS * HEAD_DIM
HGRN_WIDTH = HGRN_HEADS * HEAD_DIM
KV_WIDTH = 6 * NSA_KV_HEADS * HEAD_DIM
GATE_WIDTH = 3 * NSA_HEADS
NEG_INF = -1e30
FORCE_SCORE = 1e6
EPS = 1e-6
ATT_SCALE = HEAD_DIM ** -0.5

LANES = 128
PLAIN_RET_BLK = 0
PLAIN_HGRN_BLK = (4 * RET_WIDTH) // LANES
PLAIN_MERGE_OFF = 4 * RET_WIDTH + 4 * HGRN_WIDTH
PLAIN_GATE_BLK = (PLAIN_MERGE_OFF + 3 * D_MODEL) // LANES
PLAIN_WIDTH = PLAIN_MERGE_OFF + 3 * D_MODEL + LANES

VMEM_LIMIT = 56 * 1024 * 1024


def _cp(sem, vmem=VMEM_LIMIT):
    return pltpu.CompilerParams(dimension_semantics=sem, vmem_limit_bytes=vmem)


def _rms(x):
    return x * lax.rsqrt(jnp.mean(x * x, axis=-1, keepdims=True) + EPS)


def _dot(a, b):
    return jnp.dot(a.astype(BF16), b.astype(BF16), preferred_element_type=F32)


def _dot_nt(a, b):
    return lax.dot_general(a.astype(BF16), b.astype(BF16), (((1,), (1,)), ((), ())),
                           preferred_element_type=F32)


def _dot_tn(a, b):
    rows = a.shape[0]
    if rows % LANES:
        pad = LANES - rows % LANES
        a = jnp.concatenate([a, jnp.zeros((pad, a.shape[1]), a.dtype)], axis=0)
        b = jnp.concatenate([b, jnp.zeros((pad, b.shape[1]), b.dtype)], axis=0)
    return _dot(a.T, b)


def _split_bf16(x):
    hi = x.astype(BF16)
    lo = (x - hi.astype(F32)).astype(BF16)
    return hi, lo


def _sigmoid(x):
    return 1.0 / (1.0 + jnp.exp(-x))


def _ffn_body(x_ref, g1_ref, wg_ref, wv_ref, wd_ref, g2_ref, h_ref, u_ref, xn_sc, acc_sc, *, nf):
    j = pl.program_id(1)

    @pl.when(j == 0)
    def _():
        xn_sc[...] = (_rms(x_ref[...]) * g1_ref[...]).astype(BF16)
        acc_sc[...] = jnp.zeros_like(acc_sc)

    xn = xn_sc[...]
    g = jnp.dot(xn, wg_ref[...], preferred_element_type=F32)
    v = jnp.dot(xn, wv_ref[...], preferred_element_type=F32)
    a = (g * _sigmoid(g) * v).astype(BF16)
    acc_sc[...] += jnp.dot(a, wd_ref[...], preferred_element_type=F32)

    @pl.when(j == nf - 1)
    def _():
        h = x_ref[...] + 0.5 * acc_sc[...]
        h_ref[...] = h
        u_ref[...] = (_rms(h) * g2_ref[...]).astype(BF16)


def _ffn(x, g1, w_gu, w_down, g2, *, tm, tf):
    m, d = x.shape
    f = w_down.shape[0]
    nf = f // tf
    return pl.pallas_call(
        functools.partial(_ffn_body, nf=nf),
        grid=(m // tm, nf),
        in_specs=[
            pl.BlockSpec((tm, d), lambda i, j: (i, 0)),
            pl.BlockSpec((1, d), lambda i, j: (0, 0)),
            pl.BlockSpec((d, tf), lambda i, j: (0, j)),
            pl.BlockSpec((d, tf), lambda i, j: (0, j + nf)),
            pl.BlockSpec((tf, d), lambda i, j: (j, 0)),
            pl.BlockSpec((1, d), lambda i, j: (0, 0)),
        ],
        out_specs=[
            pl.BlockSpec((tm, d), lambda i, j: (i, 0)),
            pl.BlockSpec((tm, d), lambda i, j: (i, 0)),
        ],
        out_shape=[jax.ShapeDtypeStruct((m, d), F32), jax.ShapeDtypeStruct((m, d), BF16)],
        scratch_shapes=[pltpu.VMEM((tm, d), BF16), pltpu.VMEM((tm, d), F32)],
        compiler_params=_cp(("parallel", "arbitrary")),
        name="ffn",
    )(x, g1, w_gu, w_gu, w_down, g2)


def _proj_plain_body(u_ref, w_ref, o_ref):
    o_ref[...] = jnp.dot(u_ref[...], w_ref[...], preferred_element_type=F32).astype(o_ref.dtype)


def _proj_norm_body(u_ref, w_ref, gain_ref, flag_ref, o_ref, *, tn):
    z = jnp.dot(u_ref[...], w_ref[...], preferred_element_type=F32)
    for c in range(tn // LANES):
        sl = slice(c * LANES, (c + 1) * LANES)
        zc = z[:, sl]
        normed = _rms(zc) * gain_ref[:, sl]
        o_ref[:, sl] = jnp.where(flag_ref[:, sl] > 0.5, normed, zc).astype(o_ref.dtype)


def _proj(u, w, *, tm, tn, out_dtype, gain=None, flag=None, name):
    m, k = u.shape
    n = w.shape[1]
    in_specs = [pl.BlockSpec((tm, k), lambda i, j: (i, 0)), pl.BlockSpec((k, tn), lambda i, j: (0, j))]
    args = [u, w]
    if gain is None:
        body = _proj_plain_body
    else:
        body = functools.partial(_proj_norm_body, tn=tn)
        in_specs += [pl.BlockSpec((1, tn), lambda i, j: (0, j)), pl.BlockSpec((1, tn), lambda i, j: (0, j))]
        args += [gain, flag]
    return pl.pallas_call(
        body,
        grid=(m // tm, n // tn),
        in_specs=in_specs,
        out_specs=pl.BlockSpec((tm, tn), lambda i, j: (i, j)),
        out_shape=jax.ShapeDtypeStruct((m, n), out_dtype),
        compiler_params=_cp(("parallel", "arbitrary")),
        name=name,
    )(*args)


def _merge_body(oa_ref, ob_ref, oc_ref, m0_ref, m1_ref, m2_ref, h_ref, wb_ref, wo_ref, out_ref):
    ya = jnp.dot(oa_ref[...], wb_ref[0:NSA_WIDTH, :], preferred_element_type=F32)
    yb = jnp.dot(ob_ref[...], wb_ref[NSA_WIDTH:NSA_WIDTH + RET_WIDTH, :], preferred_element_type=F32)
    yc = jnp.dot(oc_ref[...], wb_ref[NSA_WIDTH + RET_WIDTH:, :], preferred_element_type=F32)
    mixed = _sigmoid(m0_ref[...]) * ya + _sigmoid(m1_ref[...]) * yb + _sigmoid(m2_ref[...]) * yc
    out_ref[...] = h_ref[...] + jnp.dot(mixed.astype(BF16), wo_ref[...], preferred_element_type=F32)


def _merge(o_nsa, o_ret, o_hgrn, zplain, h, w_branch, w_o, *, tm):
    m, d = h.shape
    mb = PLAIN_MERGE_OFF // d
    row = lambda i: (i, 0)
    const = lambda i: (0, 0)
    return pl.pallas_call(
        _merge_body,
        grid=(m // tm,),
        in_specs=[
            pl.BlockSpec((tm, NSA_WIDTH), row),
            pl.BlockSpec((tm, RET_WIDTH), row),
            pl.BlockSpec((tm, HGRN_WIDTH), row),
            pl.BlockSpec((tm, d), lambda i: (i, mb)),
            pl.BlockSpec((tm, d), lambda i: (i, mb + 1)),
            pl.BlockSpec((tm, d), lambda i: (i, mb + 2)),
            pl.BlockSpec((tm, d), row),
            pl.BlockSpec(w_branch.shape, const),
            pl.BlockSpec(w_o.shape, const),
        ],
        out_specs=pl.BlockSpec((tm, d), row),
        out_shape=jax.ShapeDtypeStruct((m, d), F32),
        compiler_params=_cp(("parallel",)),
        name="merge",
    )(o_nsa, o_ret, o_hgrn, zplain, zplain, zplain, h, w_branch, w_o)


def _ple_body(h_ref, u_ref, p_ref, wg_ref, wp_ref, out_ref):
    gate = _sigmoid(jnp.dot(u_ref[...], wg_ref[...], preferred_element_type=F32))
    proj = jnp.dot(p_ref[...].astype(BF16), wp_ref[...], preferred_element_type=F32)
    out_ref[...] = h_ref[...] + gate * proj


def _ple(h, u, p, w_gate, w_proj, *, tm):
    m, d = h.shape
    row = lambda i: (i, 0)
    const = lambda i: (0, 0)
    return pl.pallas_call(
        _ple_body,
        grid=(m // tm,),
        in_specs=[
            pl.BlockSpec((tm, d), row),
            pl.BlockSpec((tm, d), row),
            pl.BlockSpec((tm, p.shape[1]), row),
            pl.BlockSpec(w_gate.shape, const),
            pl.BlockSpec(w_proj.shape, const),
        ],
        out_specs=pl.BlockSpec((tm, d), row),
        out_shape=jax.ShapeDtypeStruct((m, d), F32),
        compiler_params=_cp(("parallel",)),
        name="ple",
    )(h, u, p, w_gate, w_proj)


def _retention_tables(c):
    lg = np.log1p(-np.exp2(-5.0 - np.arange(RET_HEADS, dtype=np.float64)))
    i = np.arange(c, dtype=np.float64)
    diff = i[:, None] - i[None, :]
    dmat = np.where(diff >= 0, np.exp(lg[:, None, None] * np.maximum(diff, 0.0)), 0.0)
    q_dec = np.exp(lg[:, None] * (i + 1.0))[..., None] * np.ones((1, 1, HEAD_DIM))
    k_dec = np.exp(lg[:, None] * (c - 1.0 - i))[..., None] * np.ones((1, 1, HEAD_DIM))
    c_dec = np.exp(lg * c)[:, None, None] * np.ones((1, 8, HEAD_DIM))
    return tuple(jnp.asarray(a, F32) for a in (dmat, q_dec, k_dec, c_dec))


def _retention_body(*refs, has_state, nch):
    if has_state:
        q_ref, k_ref, v_ref, g_ref, dm_ref, qd_ref, kd_ref, cd_ref, gain_ref, s0_ref, o_ref, s_ref, st_sc = refs
    else:
        q_ref, k_ref, v_ref, g_ref, dm_ref, qd_ref, kd_ref, cd_ref, gain_ref, o_ref, s_ref, st_sc = refs
    c = pl.program_id(2)

    @pl.when(c == 0)
    def _():
        st_sc[...] = s0_ref[...] if has_state else jnp.zeros_like(st_sc)

    q = q_ref[...]
    k = k_ref[...] * ATT_SCALE
    v = v_ref[...]
    s = st_sc[...]
    a = _dot_nt(q, k) * dm_ref[...]
    o = _dot(a, v) + _dot(q * qd_ref[...], s)
    st_sc[...] = s * cd_ref[0:1, :] + _dot_tn(k * kd_ref[...], v)
    g = g_ref[...]
    o_ref[...] = (_rms(o) * gain_ref[...] * (g * _sigmoid(g))).astype(o_ref.dtype)

    @pl.when(c == nch - 1)
    def _():
        s_ref[...] = st_sc[...]


def _retention(zplain, gain, s0, li, *, n, t, c):
    nch = t // c
    dmat, q_dec, k_dec, c_dec = _retention_tables(c)
    hb = RET_HEADS
    row = lambda w: (lambda i, h, j: (i * nch + j, PLAIN_RET_BLK + w * hb + h))
    tab = lambda i, h, j: (h, 0, 0)
    in_specs = [pl.BlockSpec((c, LANES), row(w)) for w in range(4)]
    in_specs += [
        pl.BlockSpec((None, c, c), tab),
        pl.BlockSpec((None, c, LANES), tab),
        pl.BlockSpec((None, c, LANES), tab),
        pl.BlockSpec((None, 8, LANES), tab),
        pl.BlockSpec((1, LANES), lambda i, h, j: (0, h)),
    ]
    args = [zplain] * 4 + [dmat, q_dec, k_dec, c_dec, gain]
    if s0 is not None:
        in_specs.append(pl.BlockSpec((None, None, None, LANES, LANES), lambda i, h, j: (li, i, h, 0, 0)))
        args.append(s0)
    return pl.pallas_call(
        functools.partial(_retention_body, has_state=s0 is not None, nch=nch),
        grid=(n, hb, nch),
        in_specs=in_specs,
        out_specs=[
            pl.BlockSpec((c, LANES), lambda i, h, j: (i * nch + j, h)),
            pl.BlockSpec((None, None, LANES, LANES), lambda i, h, j: (i, h, 0, 0)),
        ],
        out_shape=[
            jax.ShapeDtypeStruct((n * t, RET_WIDTH), BF16),
            jax.ShapeDtypeStruct((n, hb, LANES, LANES), F32),
        ],
        scratch_shapes=[pltpu.VMEM((LANES, LANES), F32)],
        compiler_params=_cp(("parallel", "parallel", "arbitrary")),
        name="retention",
    )(*args)


HGRN_SUB = 16


def _hgrn_body(*refs, has_state, nch, c):
    if has_state:
        q_ref, f_ref, v_ref, g_ref, lb_ref, gain_ref, s0_ref, o_ref, s_ref, st_sc, b_sc, k_sc, v_sc, o_sc = refs
    else:
        q_ref, f_ref, v_ref, g_ref, lb_ref, gain_ref, o_ref, s_ref, st_sc, b_sc, k_sc, v_sc, o_sc = refs
    ci = pl.program_id(2)

    @pl.when(ci == 0)
    def _():
        st_sc[...] = s0_ref[...].T if has_state else jnp.zeros_like(st_sc)

    lb = lb_ref[...]
    f = lb + (1.0 - lb) * _sigmoid(f_ref[...])
    logf = jnp.log(f)
    ri = lax.broadcasted_iota(jnp.int32, (c, c), 0)
    si = lax.broadcasted_iota(jnp.int32, (c, c), 1)
    tri = jnp.where(ri >= si, 1.0, 0.0).astype(BF16)
    hi = logf.astype(BF16)
    r1 = logf - hi.astype(F32)
    mid = r1.astype(BF16)
    lo = (r1 - mid.astype(F32)).astype(BF16)
    b = (jnp.dot(tri, hi, preferred_element_type=F32) + jnp.dot(tri, mid, preferred_element_type=F32)
         + jnp.dot(tri, lo, preferred_element_type=F32))
    q = q_ref[...]
    kk = 1.0 - f
    v = v_ref[...]
    b_sc[...] = b
    k_sc[...] = kk
    v_sc[...] = v
    st = st_sc[...]
    o_sc[...] = _dot_nt(q * jnp.exp(b), st)

    sub = min(HGRN_SUB, c)
    row_id = lax.broadcasted_iota(jnp.int32, (sub, LANES), 0)
    for blk in range(c // sub):
        r0 = blk * sub
        b_i = b[r0:r0 + sub]
        q_i = q[r0:r0 + sub]
        acc = jnp.zeros((sub, LANES), F32)
        if blk > 0:
            ref = b_sc[r0 - 1:r0, :]
            qt = q_i * jnp.exp(b_i - ref)
            kt = kk[0:r0] * jnp.exp(ref - b[0:r0])
            acc = _dot(_dot_nt(qt, kt), v[0:r0])
        for s in range(sub):
            b_s = b_sc[r0 + s:r0 + s + 1, :]
            k_s = k_sc[r0 + s:r0 + s + 1, :]
            v_s = v_sc[r0 + s:r0 + s + 1, :]
            w = q_i * jnp.exp(jnp.minimum(b_i - b_s, 0.0)) * k_s
            w = jnp.where(row_id >= s, w, 0.0)
            acc = acc + jnp.sum(w, axis=-1, keepdims=True) * v_s
        o_sc[r0:r0 + sub, :] += acc

    b_last = b_sc[c - 1:c, :]
    st_sc[...] = st * jnp.exp(b_last) + _dot_tn(v, kk * jnp.exp(b_last - b))
    g = g_ref[...]
    o_ref[...] = (_rms(o_sc[...]) * gain_ref[...] * (g * _sigmoid(g))).astype(o_ref.dtype)

    @pl.when(ci == nch - 1)
    def _():
        s_ref[...] = st_sc[...].T


def _hgrn(zplain, lb, gain, s0, li, *, n, t, c):
    nch = t // c
    hb = HGRN_HEADS
    row = lambda w: (lambda i, h, j: (i * nch + j, PLAIN_HGRN_BLK + w * hb + h))
    head = lambda i, h, j: (0, h)
    in_specs = [pl.BlockSpec((c, LANES), row(w)) for w in range(4)]
    in_specs += [pl.BlockSpec((1, LANES), head), pl.BlockSpec((1, LANES), head)]
    args = [zplain] * 4 + [lb, gain]
    if s0 is not None:
        in_specs.append(pl.BlockSpec((None, None, None, LANES, LANES), lambda i, h, j: (li, i, h, 0, 0)))
        args.append(s0)
    return pl.pallas_call(
        functools.partial(_hgrn_body, has_state=s0 is not None, nch=nch, c=c),
        grid=(n, hb, nch),
        in_specs=in_specs,
        out_specs=[
            pl.BlockSpec((c, LANES), lambda i, h, j: (i * nch + j, h)),
            pl.BlockSpec((None, None, LANES, LANES), lambda i, h, j: (i, h, 0, 0)),
        ],
        out_shape=[
            jax.ShapeDtypeStruct((n * t, HGRN_WIDTH), BF16),
            jax.ShapeDtypeStruct((n, hb, LANES, LANES), F32),
        ],
        scratch_shapes=[pltpu.VMEM((LANES, LANES), F32)] + [pltpu.VMEM((c, LANES), F32)] * 4,
        compiler_params=_cp(("parallel", "parallel", "arbitrary")),
        name="hgrn",
    )(*args)


def _lb_body(x_ref, o_ref):
    x = x_ref[...]
    e = jnp.exp(x - jnp.max(x, axis=0, keepdims=True))
    sm = e / jnp.sum(e, axis=0, keepdims=True)
    acc = jnp.zeros_like(sm[0:1])
    o_ref[0:1, :] = acc
    for layer in range(1, x.shape[0]):
        acc = acc + sm[layer:layer + 1]
        o_ref[layer:layer + 1, :] = acc


def _hgrn_lower_bounds(hgrn_lb):
    return pl.pallas_call(
        _lb_body,
        out_shape=jax.ShapeDtypeStruct(hgrn_lb.shape, F32),
        name="hgrn_lb",
    )(hgrn_lb.astype(F32))


def _group_sums(x, p0, p1):
    rows, width = x.shape
    xg = x.reshape(rows // CMP_STRIDE, CMP_STRIDE, width)
    return jnp.sum(xg * p0[None], axis=1), jnp.sum(xg * p1[None], axis=1)


def _cmp_partial_body(x_ref, p0_ref, p1_ref, a0_ref, a1_ref):
    a0, a1 = _group_sums(x_ref[...], p0_ref[...], p1_ref[...])
    a0_ref[...] = a0
    a1_ref[...] = a1


def _cmp_partial(kvrows, p0, p1, *, rb):
    m = kvrows.shape[0]
    w = p0.shape[1]
    const = lambda i: (0, 0)
    out = jax.ShapeDtypeStruct((m // CMP_STRIDE, w), F32)
    return pl.pallas_call(
        _cmp_partial_body,
        grid=(m // rb,),
        in_specs=[pl.BlockSpec((rb, w), lambda i: (i, 0)), pl.BlockSpec(p0.shape, const), pl.BlockSpec(p1.shape, const)],
        out_specs=[pl.BlockSpec((rb // CMP_STRIDE, w), lambda i: (i, 0))] * 2,
        out_shape=[out, out],
        compiler_params=_cp(("parallel",)),
        name="cmp_partial",
    )(kvrows, p0, p1)


def _cmp_partial_paged_body(pt_ref, *refs, pages):
    x_refs = refs[:pages]
    p0_ref, p1_ref, a0_ref, a1_ref = refs[pages:]
    gp = x_refs[0].shape[0] // CMP_STRIDE
    for k in range(pages):
        a0, a1 = _group_sums(x_refs[k][...], p0_ref[...], p1_ref[...])
        a0_ref[k * gp:(k + 1) * gp, :] = a0
        a1_ref[k * gp:(k + 1) * gp, :] = a1


def _cmp_partial_paged(cache4, page_table, p0, p1, li, *, pages):
    n, n_pages = page_table.shape
    page = cache4.shape[2]
    w = p0.shape[1]
    gp = page // CMP_STRIDE
    const = lambda i, c, pt: (0, 0)
    page_spec = lambda k: pl.BlockSpec((None, None, page, w), lambda i, c, pt: (li, pt[i, c * pages + k], 0, 0))
    out = jax.ShapeDtypeStruct((n, n_pages * gp, w), F32)
    return pl.pallas_call(
        functools.partial(_cmp_partial_paged_body, pages=pages),
        grid_spec=pltpu.PrefetchScalarGridSpec(
            num_scalar_prefetch=1,
            grid=(n, n_pages // pages),
            in_specs=[page_spec(k) for k in range(pages)] + [pl.BlockSpec(p0.shape, const), pl.BlockSpec(p1.shape, const)],
            out_specs=[pl.BlockSpec((None, pages * gp, w), lambda i, c, pt: (i, c, 0))] * 2,
        ),
        out_shape=[out, out],
        compiler_params=_cp(("parallel", "arbitrary")),
        name="cmp_partial_paged",
    )(page_table, *([cache4] * pages), p0, p1)


def _cmp_final_body(a0_ref, a1_ref, wk_ref, wv_ref, gain_ref, kc_ref, vc_ref):
    ng = a0_ref.shape[0]
    agg = a0_ref[...] + pltpu.roll(a1_ref[...], ng - 1, 0)
    for g in range(NSA_KV_HEADS):
        sl = slice(g * HEAD_DIM, (g + 1) * HEAD_DIM)
        ak = agg[:, sl]
        av = agg[:, NSA_KV_HEADS * HEAD_DIM + g * HEAD_DIM:NSA_KV_HEADS * HEAD_DIM + (g + 1) * HEAD_DIM]
        kc_ref[:, sl] = (_rms(_dot(ak, wk_ref[...])) * gain_ref[...]).astype(kc_ref.dtype)
        vc_ref[:, sl] = _dot(av, wv_ref[...]).astype(vc_ref.dtype)


def _cmp_final(a0, a1, w_k, w_v, gain):
    n, ng, w = a0.shape
    const = lambda i: (0, 0)
    out = jax.ShapeDtypeStruct((n, ng, NSA_KV_HEADS * HEAD_DIM), BF16)
    return pl.pallas_call(
        _cmp_final_body,
        grid=(n,),
        in_specs=[
            pl.BlockSpec((None, ng, w), lambda i: (i, 0, 0)),
            pl.BlockSpec((None, ng, w), lambda i: (i, 0, 0)),
            pl.BlockSpec(w_k.shape, const),
            pl.BlockSpec(w_v.shape, const),
            pl.BlockSpec(gain.shape, const),
        ],
        out_specs=[pl.BlockSpec((None, ng, NSA_KV_HEADS * HEAD_DIM), lambda i: (i, 0, 0))] * 2,
        out_shape=[out, out],
        compiler_params=_cp(("parallel",)),
        name="cmp_final",
    )(a0, a1, w_k, w_v, gain)


def _overlap_matrix(n_cmp_pad, n_slc, width):
    ci = lax.broadcasted_iota(jnp.int32, (n_cmp_pad, width), 0)
    sj = lax.broadcasted_iota(jnp.int32, (n_cmp_pad, width), 1)
    c_start = ci * CMP_STRIDE
    s_start = sj * SEL_BLOCK
    hit = ((c_start < s_start + SEL_BLOCK) & (c_start + CMP_BLOCK > s_start)
           & (ci < n_cmp_pad - 1) & (sj < n_slc))
    return jnp.where(hit, 1.0, 0.0).astype(BF16)


def _block_scores(imp, qpos_i, n_cmp_pad, n_slc, width):
    hi, lo = _split_bf16(imp)
    ov = _overlap_matrix(n_cmp_pad, n_slc, width)
    score = jnp.dot(hi, ov, preferred_element_type=F32) + jnp.dot(lo, ov, preferred_element_type=F32)
    sj = lax.broadcasted_iota(jnp.int32, score.shape, 1)
    cur = qpos_i // SEL_BLOCK
    forced = (sj == 0) | (sj == cur) | (sj == cur - 1)
    score = jnp.where(forced, FORCE_SCORE, score)
    score = jnp.where(sj * SEL_BLOCK <= qpos_i, score, -1.0)
    return jnp.where(sj < n_slc, score, -2.0)


def _masked_softmax(s, allow):
    s = jnp.where(allow, s, NEG_INF)
    m = jnp.max(s, axis=-1, keepdims=True)
    e = jnp.where(allow, jnp.exp(s - m), 0.0)
    return e / jnp.maximum(jnp.sum(e, axis=-1, keepdims=True), 1e-30)


SEL_KEY_BLOCK = 256


def _nsa_prompt_body(q_ref, zg_ref, kc_ref, vc_ref, ks_ref, vs_ref, kw_ref, vw_ref, o_ref, *, t_len, tq):
    g = pl.program_id(1)
    t0 = pl.program_id(2) * tq
    n_cmp_pad = kc_ref.shape[0]
    n_slc = -(-t_len // SEL_BLOCK)
    hpg = HEADS_PER_GROUP
    qpos_i = t0 + lax.broadcasted_iota(jnp.int32, (tq, 1), 0)
    slopes = [jnp.where(g == 0, 2.0 ** -(r + 1), 2.0 ** -(r + 1 + hpg)) for r in range(hpg)]
    qs = [q_ref[:, r * HEAD_DIM:(r + 1) * HEAD_DIM] for r in range(hpg)]

    ci = lax.broadcasted_iota(jnp.int32, (tq, n_cmp_pad), 1)
    cdist_i = qpos_i - (ci * CMP_STRIDE + CMP_BLOCK - 1)
    callow = (cdist_i >= 0) & (ci < n_cmp_pad - 1)
    cdist = cdist_i.astype(F32)
    kc = kc_ref[...]
    vc = vc_ref[...]
    imp = jnp.zeros((tq, n_cmp_pad), F32)
    o_cmp = []
    for r in range(hpg):
        p = _masked_softmax(_dot_nt(qs[r], kc) * ATT_SCALE - slopes[r] * cdist, callow)
        o_cmp.append(_dot(p, vc))
        imp = imp + p

    score = _block_scores(imp, qpos_i, n_cmp_pad, n_slc, LANES)
    s_t = score.T[0:n_slc]
    jrow = lax.broadcasted_iota(jnp.int32, (n_slc, tq), 0)
    rank = jnp.zeros((n_slc, tq), jnp.int32)
    for jp in range(n_slc):
        row = s_t[jp:jp + 1, :]
        beats = (row > s_t) | ((row == s_t) & (jp < jrow))
        rank = rank + jnp.where(beats, 1, 0)
    sel_t = jnp.where((rank < N_SEL) & (s_t >= 0.0), 1.0, 0.0)
    sel_t = jnp.concatenate([sel_t, jnp.zeros((LANES - n_slc, tq), F32)], axis=0)
    sel = sel_t.T.astype(BF16)

    kb = SEL_KEY_BLOCK
    nkb = (t0 + tq + kb - 1) // kb

    def sel_step(i, carry):
        k0 = pl.multiple_of(i * kb, kb)
        kblk = ks_ref[pl.ds(k0, kb), :].astype(BF16)
        vblk = vs_ref[pl.ds(k0, kb), :].astype(BF16)
        kpos_i = k0 + lax.broadcasted_iota(jnp.int32, (1, kb), 1)
        ej = lax.broadcasted_iota(jnp.int32, (LANES, kb), 0)
        ec = lax.broadcasted_iota(jnp.int32, (LANES, kb), 1)
        expand = jnp.where(ej == (k0 + ec) // SEL_BLOCK, 1.0, 0.0).astype(BF16)
        allow = (jnp.dot(sel, expand, preferred_element_type=F32) > 0.5) & (kpos_i <= qpos_i)
        dist = (qpos_i - kpos_i).astype(F32)
        out = []
        for r in range(hpg):
            m_old, l_old, acc_old = carry[3 * r:3 * r + 3]
            s = jnp.where(allow, _dot_nt(qs[r], kblk) * ATT_SCALE - slopes[r] * dist, NEG_INF)
            m_new = jnp.maximum(m_old, jnp.max(s, axis=-1, keepdims=True))
            alpha = jnp.exp(m_old - m_new)
            e = jnp.where(allow, jnp.exp(s - m_new), 0.0)
            out += [m_new, alpha * l_old + jnp.sum(e, axis=-1, keepdims=True),
                    alpha * acc_old + _dot(e, vblk)]
        return tuple(out)

    init = (jnp.full((tq, 1), NEG_INF, F32), jnp.zeros((tq, 1), F32), jnp.zeros((tq, HEAD_DIM), F32)) * hpg
    fin = lax.fori_loop(0, nkb, sel_step, init)
    o_slc = [fin[3 * r + 2] / fin[3 * r + 1] for r in range(hpg)]

    wk = WINDOW + tq
    ws = pl.multiple_of(jnp.clip(t0 - WINDOW, 0, t_len - wk), LANES)
    kw = kw_ref[pl.ds(ws, wk), :].astype(BF16)
    vw = vw_ref[pl.ds(ws, wk), :].astype(BF16)
    wdist_i = qpos_i - (ws + lax.broadcasted_iota(jnp.int32, (1, wk), 1))
    wallow = (wdist_i >= 0) & (wdist_i <= WINDOW)
    wdist = wdist_i.astype(F32)
    gates = _sigmoid(zg_ref[...])
    for r in range(hpg):
        p = _masked_softmax(_dot_nt(qs[r], kw) * ATT_SCALE - slopes[r] * wdist, wallow)
        o_win = _dot(p, vw)
        gate = [jnp.where(g == 0, gates[:, 3 * r + b:3 * r + b + 1],
                          gates[:, 3 * (r + hpg) + b:3 * (r + hpg) + b + 1]) for b in range(3)]
        o = gate[0] * o_cmp[r] + gate[1] * o_slc[r] + gate[2] * o_win
        o_ref[:, r * HEAD_DIM:(r + 1) * HEAD_DIM] = o.astype(o_ref.dtype)


def _nsa_prompt(q, zplain, kc, vc, kvrows, winrows, *, n, t, tq):
    nt = t // tq
    gw = HEADS_PER_GROUP * HEAD_DIM
    n_cmp_pad = kc.shape[1]
    g_blocks = NSA_KV_HEADS
    qrow = lambda i, g, j: (i * nt + j, g)
    seq = lambda blk: (lambda i, g, j: (i, blk + g))
    return pl.pallas_call(
        functools.partial(_nsa_prompt_body, t_len=t, tq=tq),
        grid=(n, NSA_KV_HEADS, nt),
        in_specs=[
            pl.BlockSpec((tq, gw), qrow),
            pl.BlockSpec((tq, LANES), lambda i, g, j: (i * nt + j, PLAIN_GATE_BLK)),
            pl.BlockSpec((None, n_cmp_pad, HEAD_DIM), lambda i, g, j: (i, 0, g)),
            pl.BlockSpec((None, n_cmp_pad, HEAD_DIM), lambda i, g, j: (i, 0, g)),
            pl.BlockSpec((t, HEAD_DIM), seq(2 * g_blocks)),
            pl.BlockSpec((t, HEAD_DIM), seq(3 * g_blocks)),
            pl.BlockSpec((t, HEAD_DIM), seq(0)),
            pl.BlockSpec((t, HEAD_DIM), seq(g_blocks)),
        ],
        out_specs=pl.BlockSpec((tq, gw), qrow),
        out_shape=jax.ShapeDtypeStruct((n * t, NSA_WIDTH), BF16),
        compiler_params=_cp(("parallel", "parallel", "arbitrary")),
        name="nsa_prompt",
    )(q, zplain, kc, vc, kvrows, kvrows, winrows, winrows)


def _layer_weights(li, norms, w_in, qk_gain, cmp_pos, cmp_w, out_gain, lb_all, w_branch, w_o, w_gu, w_down,
                   w_ple_proj, w_ple_gate):
    wi = w_in[li]
    kv_off = NSA_WIDTH
    gate_off = kv_off + KV_WIDTH
    plain_off = gate_off + GATE_WIDTH
    slot = NSA_KV_HEADS * HEAD_DIM
    ones = jnp.ones((slot,), F32)
    zeros = jnp.zeros((slot,), F32)
    gain = lambda i: jnp.tile(qk_gain[li, i], NSA_KV_HEADS)
    w_gate = jnp.pad(wi[:, gate_off:plain_off], ((0, 0), (0, LANES - GATE_WIDTH)))
    row = lambda v: v.reshape(1, -1).astype(F32)
    cp = cmp_pos[li]
    half = lambda w, m: w[m * CMP_STRIDE:(m + 1) * CMP_STRIDE]
    pos = lambda m: jnp.concatenate([half(cp[0], m)] * NSA_KV_HEADS + [half(cp[1], m)] * NSA_KV_HEADS, axis=1)
    return dict(
        norm=[row(norms[li, i]) for i in range(4)],
        w_q=wi[:, :kv_off].astype(BF16),
        q_gain=row(jnp.tile(qk_gain[li, 0], NSA_HEADS)),
        q_flag=jnp.ones((1, NSA_WIDTH), F32),
        w_kv=wi[:, kv_off:kv_off + 4 * slot].astype(BF16),
        kv_gain=row(jnp.concatenate([ones, ones, gain(1), ones])),
        kv_flag=row(jnp.concatenate([zeros, zeros, ones, zeros])),
        w_win=wi[:, kv_off + 4 * slot:gate_off].astype(BF16),
        win_gain=row(jnp.concatenate([gain(2), ones])),
        win_flag=row(jnp.concatenate([ones, zeros])),
        w_plain=jnp.concatenate([wi[:, plain_off:], w_gate], axis=1).astype(BF16),
        cmp_p0=pos(0).astype(F32),
        cmp_p1=pos(1).astype(F32),
        w_phi_k=cmp_w[li, 0].astype(BF16),
        w_phi_v=cmp_w[li, 1].astype(BF16),
        kc_gain=row(qk_gain[li, 3]),
        ret_gain=row(out_gain[li, 0]),
        hgrn_gain=row(out_gain[li, 1]),
        lb=lb_all[li:li + 1],
        w_branch=w_branch[li].astype(BF16),
        w_o=w_o[li].astype(BF16),
        w_gu=[w_gu[li, i].astype(BF16) for i in range(2)],
        w_down=[w_down[li, i].astype(BF16) for i in range(2)],
        w_ple_proj=w_ple_proj[li].astype(BF16),
        w_ple_gate=w_ple_gate[li].astype(BF16),
    )


def _tiles(m):
    big = 512 if m % 512 == 0 else m
    small = 256 if m % 256 == 0 else m
    return big, small


def _projections(u, w, tm):
    q = _proj(u, w["w_q"], tm=tm, tn=512, out_dtype=BF16, gain=w["q_gain"], flag=w["q_flag"], name="proj_q")
    kvrows = _proj(u, w["w_kv"], tm=tm, tn=512, out_dtype=F32, gain=w["kv_gain"], flag=w["kv_flag"], name="proj_kv")
    winrows = _proj(u, w["w_win"], tm=tm, tn=512, out_dtype=F32, gain=w["win_gain"], flag=w["win_flag"],
                    name="proj_win")
    zplain = _proj(u, w["w_plain"], tm=tm, tn=1152, out_dtype=F32, name="proj_plain")
    return q, kvrows, winrows, zplain


def _layer_tail(h, o_nsa, o_ret, o_hgrn, zplain, p, w, tm, ts):
    h = _merge(o_nsa, o_ret, o_hgrn, zplain, h, w["w_branch"], w["w_o"], tm=ts)
    h, u = _ffn(h, w["norm"][2], w["w_gu"][1], w["w_down"][1], w["norm"][3], tm=tm, tf=512)
    return _ple(h, u, p, w["w_ple_gate"], w["w_ple_proj"], tm=ts)


def _prompt_layer(x, p, w, *, n, t):
    tm, ts = _tiles(n * t)
    h, u = _ffn(x, w["norm"][0], w["w_gu"][0], w["w_down"][0], w["norm"][1], tm=tm, tf=512)
    q, kvrows, winrows, zplain = _projections(u, w, tm)
    a0, a1 = _cmp_partial(kvrows, w["cmp_p0"], w["cmp_p1"], rb=min(1024, t))
    ng = t // CMP_STRIDE
    kc, vc = _cmp_final(a0.reshape(n, ng, -1), a1.reshape(n, ng, -1), w["w_phi_k"], w["w_phi_v"], w["kc_gain"])
    o_nsa = _nsa_prompt(q, zplain, kc, vc, kvrows, winrows, n=n, t=t, tq=LANES)
    o_ret, s_ret = _retention(zplain, w["ret_gain"], None, 0, n=n, t=t, c=LANES)
    o_hgrn, s_hgrn = _hgrn(zplain, w["lb"], w["hgrn_gain"], None, 0, n=n, t=t, c=LANES)
    h = _layer_tail(h, o_nsa, o_ret, o_hgrn, zplain, p, w, tm, ts)
    kv_out = kvrows.reshape(n, t, 4, NSA_KV_HEADS, HEAD_DIM)
    win_out = winrows.reshape(n, t, 2, NSA_KV_HEADS, HEAD_DIM)[:, t - min(WINDOW, t):]
    return h, kv_out, win_out, s_ret, s_hgrn


def _nsa_dec_cmp_body(q_ref, kc_ref, vc_ref, ocmp_ref, sel_ref, *, past, steps, n_slc):
    g = pl.program_id(1)
    nc = kc_ref.shape[0]
    hpg = HEADS_PER_GROUP
    rows = hpg * steps
    ri = lax.broadcasted_iota(jnp.int32, (rows, 1), 0)
    qpos_i = past + ri % steps
    slope = jnp.exp2(-(ri // steps + 1 + g * hpg).astype(F32))
    ci = lax.broadcasted_iota(jnp.int32, (rows, nc), 1)
    dist_i = qpos_i - (ci * CMP_STRIDE + CMP_BLOCK - 1)
    allow = (dist_i >= 0) & (ci < nc - 1)
    p = _masked_softmax(_dot_nt(q_ref[...], kc_ref[...]) * ATT_SCALE - slope * dist_i.astype(F32), allow)
    ocmp_ref[...] = _dot(p, vc_ref[...])
    imp = p[0:steps]
    for r in range(1, hpg):
        imp = imp + p[r * steps:(r + 1) * steps]

    width = -(-n_slc // LANES) * LANES
    qpos8 = past + lax.broadcasted_iota(jnp.int32, (steps, 1), 0)
    score = _block_scores(imp, qpos8, nc, n_slc, width)
    sj = lax.broadcasted_iota(jnp.int32, score.shape, 1)
    lane = lax.broadcasted_iota(jnp.int32, (steps, LANES), 1)
    picked = jnp.full((steps, LANES), -1, jnp.int32)
    for it in range(min(N_SEL, n_slc)):
        m = jnp.max(score, axis=-1, keepdims=True)
        idx = jnp.min(jnp.where(score == m, sj, width), axis=-1, keepdims=True)
        picked = jnp.where(lane == it, jnp.where(m >= 0.0, idx, -1), picked)
        score = jnp.where(sj == idx, -3.0, score)
    sel_ref[...] = picked


def _nsa_dec_cmp(q_rt, kc, vc, *, past, steps, n_slc):
    n, ng = kc.shape[0], kc.shape[1]
    rows = HEADS_PER_GROUP * steps
    grp = lambda i, g: (i, g, 0, 0)
    return pl.pallas_call(
        functools.partial(_nsa_dec_cmp_body, past=past, steps=steps, n_slc=n_slc),
        grid=(n, NSA_KV_HEADS),
        in_specs=[
            pl.BlockSpec((None, None, rows, HEAD_DIM), grp),
            pl.BlockSpec((None, ng, HEAD_DIM), lambda i, g: (i, 0, g)),
            pl.BlockSpec((None, ng, HEAD_DIM), lambda i, g: (i, 0, g)),
        ],
        out_specs=[
            pl.BlockSpec((None, None, rows, HEAD_DIM), grp),
            pl.BlockSpec((None, None, steps, LANES), grp),
        ],
        out_shape=[
            jax.ShapeDtypeStruct((n, NSA_KV_HEADS, rows, HEAD_DIM), F32),
            jax.ShapeDtypeStruct((n, NSA_KV_HEADS, steps, LANES), jnp.int32),
        ],
        compiler_params=_cp(("parallel", "parallel")),
        name="nsa_dec_cmp",
    )(q_rt, kc, vc)


ROW_PAD = 8


def _nsa_dec_attend_body(pt_ref, sel_ref, q_ref, gate_ref, ocmp_ref, kn_ref, vn_ref, kwp_ref, vwp_ref, kwn_ref,
                         vwn_ref, *refs, past, steps, n_sel, past_blocks):
    k_refs = refs[:steps]
    v_refs = refs[steps:2 * steps]
    o_ref, m_sc, l_sc, acc_sc = refs[2 * steps:]
    i = pl.program_id(0)
    g = pl.program_id(1)
    k = pl.program_id(2)
    rows = steps * ROW_PAD
    q = q_ref[...]
    ri = lax.broadcasted_iota(jnp.int32, (rows, 1), 0)
    step_i = ri // ROW_PAD
    qpos_i = past + step_i
    head = jnp.minimum(ri % ROW_PAD, HEADS_PER_GROUP - 1)
    slope = jnp.exp2(-(head + 1 + g * HEADS_PER_GROUP).astype(F32))
    new_i = lax.broadcasted_iota(jnp.int32, (1, steps), 1)
    new_allow = new_i <= step_i
    new_dist = (step_i - new_i).astype(F32)

    @pl.when(k == 0)
    def _():
        s = jnp.where(new_allow, _dot_nt(q, kn_ref[...]) * ATT_SCALE - slope * new_dist, NEG_INF)
        m = jnp.max(s, axis=-1, keepdims=True)
        e = jnp.where(new_allow, jnp.exp(s - m), 0.0)
        m_sc[...] = m
        l_sc[...] = jnp.sum(e, axis=-1, keepdims=True)
        acc_sc[...] = _dot(e, vn_ref[...])

    kall = jnp.concatenate([r[...] for r in k_refs], axis=0)
    vall = jnp.concatenate([r[...] for r in v_refs], axis=0)
    cols = steps * SEL_BLOCK
    col = lax.broadcasted_iota(jnp.int32, (1, cols), 1)
    col_step = col // SEL_BLOCK
    base = ((i * NSA_KV_HEADS + g) * steps) * n_sel + k
    blk = jnp.zeros((1, cols), jnp.int32)
    for t in range(steps):
        blk = jnp.where(col_step == t, sel_ref[base + t * n_sel], blk)
    kpos_i = blk * SEL_BLOCK + col % SEL_BLOCK
    dist_i = qpos_i - kpos_i
    allow = (col_step == step_i) & (blk >= 0) & (blk < past_blocks) & (dist_i >= 0)
    s = jnp.where(allow, _dot_nt(q, kall) * ATT_SCALE - slope * dist_i.astype(F32), NEG_INF)
    m_old = m_sc[...]
    m_new = jnp.maximum(m_old, jnp.max(s, axis=-1, keepdims=True))
    alpha = jnp.exp(m_old - m_new)
    e = jnp.where(allow, jnp.exp(s - m_new), 0.0)
    m_sc[...] = m_new
    l_sc[...] = alpha * l_sc[...] + jnp.sum(e, axis=-1, keepdims=True)
    acc_sc[...] = alpha * acc_sc[...] + _dot(e, vall)

    @pl.when(k == n_sel - 1)
    def _():
        o_slc = acc_sc[...] / l_sc[...]
        wb = kwp_ref.shape[0]
        wi = lax.broadcasted_iota(jnp.int32, (1, wb), 1)
        wdist_i = qpos_i - (past - wb + wi)
        wallow = (wdist_i >= 0) & (wdist_i <= WINDOW)
        sp = jnp.where(wallow, _dot_nt(q, kwp_ref[...]) * ATT_SCALE - slope * wdist_i.astype(F32), NEG_INF)
        sn = jnp.where(new_allow, _dot_nt(q, kwn_ref[...]) * ATT_SCALE - slope * new_dist, NEG_INF)
        m = jnp.maximum(jnp.max(sp, axis=-1, keepdims=True), jnp.max(sn, axis=-1, keepdims=True))
        ep = jnp.where(wallow, jnp.exp(sp - m), 0.0)
        en = jnp.where(new_allow, jnp.exp(sn - m), 0.0)
        l = jnp.sum(ep, axis=-1, keepdims=True) + jnp.sum(en, axis=-1, keepdims=True)
        o_win = (_dot(ep, vwp_ref[...]) + _dot(en, vwn_ref[...])) / l
        gates = _sigmoid(gate_ref[...])
        o = gates[:, 0:1] * ocmp_ref[...] + gates[:, 1:2] * o_slc + gates[:, 2:3] * o_win
        o_ref[...] = o.astype(o_ref.dtype)


def _nsa_dec_attend(page_table, sel_flat, q_tr, gates_tr, ocmp_tr, kvrows, winrows, cache_half, cache_win4, li, *,
                    past, steps, n_sel):
    n = q_tr.shape[0]
    rows = steps * ROW_PAD
    gb = NSA_KV_HEADS
    halves = cache_half.shape[2]
    past_blocks = past // SEL_BLOCK
    grp = lambda i, g, k, pt, sel: (i, g, 0, 0)
    new = lambda blk: (lambda i, g, k, pt, sel: (i, blk + g))
    winp = lambda blk: (lambda i, g, k, pt, sel: (li, i, 0, blk + g))

    def gather(t, blk):
        def index(i, g, k, pt, sel):
            j = jnp.clip(sel[((i * gb + g) * steps + t) * n_sel + k], 0, past_blocks - 1)
            return (li, pt[i, j // halves], j % halves, 0, blk + g)
        return pl.BlockSpec((None, None, None, SEL_BLOCK, HEAD_DIM), index)

    wb = cache_win4.shape[2]
    in_specs = [
        pl.BlockSpec((None, None, rows, HEAD_DIM), grp),
        pl.BlockSpec((None, None, rows, LANES), grp),
        pl.BlockSpec((None, None, rows, HEAD_DIM), grp),
        pl.BlockSpec((steps, HEAD_DIM), new(2 * gb)),
        pl.BlockSpec((steps, HEAD_DIM), new(3 * gb)),
        pl.BlockSpec((None, None, wb, HEAD_DIM), winp(0)),
        pl.BlockSpec((None, None, wb, HEAD_DIM), winp(gb)),
        pl.BlockSpec((steps, HEAD_DIM), new(0)),
        pl.BlockSpec((steps, HEAD_DIM), new(gb)),
    ]
    in_specs += [gather(t, 2 * gb) for t in range(steps)] + [gather(t, 3 * gb) for t in range(steps)]
    return pl.pallas_call(
        functools.partial(_nsa_dec_attend_body, past=past, steps=steps, n_sel=n_sel, past_blocks=past_blocks),
        grid_spec=pltpu.PrefetchScalarGridSpec(
            num_scalar_prefetch=2,
            grid=(n, gb, n_sel),
            in_specs=in_specs,
            out_specs=pl.BlockSpec((None, None, rows, HEAD_DIM), grp),
            scratch_shapes=[pltpu.VMEM((rows, 1), F32), pltpu.VMEM((rows, 1), F32), pltpu.VMEM((rows, HEAD_DIM), F32)],
        ),
        out_shape=jax.ShapeDtypeStruct((n, gb, rows, HEAD_DIM), BF16),
        compiler_params=_cp(("parallel", "parallel", "arbitrary")),
        name="nsa_dec_attend",
    )(page_table, sel_flat, q_tr, gates_tr, ocmp_tr, kvrows, kvrows, cache_win4, cache_win4, winrows, winrows,
      *([cache_half] * (2 * steps)))


def _decode_layer(x, p, w, li, cache4, cache_half, cache_win4, state_ret, state_hgrn, page_table, *, n, steps):
    m = n * steps
    hpg, gb = HEADS_PER_GROUP, NSA_KV_HEADS
    past = page_table.shape[1] * cache4.shape[2]
    n_slc = -(-(past + steps) // SEL_BLOCK)
    n_sel = min(N_SEL, n_slc)
    h, u = _ffn(x, w["norm"][0], w["w_gu"][0], w["w_down"][0], w["norm"][1], tm=m, tf=512)
    q, kvrows, winrows, zplain = _projections(u, w, m)
    a0, a1 = _cmp_partial_paged(cache4, page_table, w["cmp_p0"], w["cmp_p1"], li, pages=8)
    kc, vc = _cmp_final(a0, a1, w["w_phi_k"], w["w_phi_v"], w["kc_gain"])
    q5 = q.reshape(n, steps, gb, hpg, HEAD_DIM)
    q_rt = q5.transpose(0, 2, 3, 1, 4).reshape(n, gb, hpg * steps, HEAD_DIM)
    o_cmp, sel = _nsa_dec_cmp(q_rt, kc, vc, past=past, steps=steps, n_slc=n_slc)
    pad_heads = lambda a: jnp.pad(a, ((0, 0), (0, 0), (0, 0), (0, ROW_PAD - hpg), (0, 0)))
    rows = steps * ROW_PAD
    q_tr = pad_heads(q5.transpose(0, 2, 1, 3, 4)).reshape(n, gb, rows, HEAD_DIM)
    ocmp_tr = pad_heads(o_cmp.reshape(n, gb, hpg, steps, HEAD_DIM).transpose(0, 1, 3, 2, 4)).reshape(
        n, gb, rows, HEAD_DIM)
    zg = zplain[:, PLAIN_GATE_BLK * LANES:PLAIN_GATE_BLK * LANES + GATE_WIDTH].reshape(n, steps, gb, hpg, 3)
    gates_tr = pad_heads(zg.transpose(0, 2, 1, 3, 4)).reshape(n, gb, rows, 3)
    gates_tr = jnp.pad(gates_tr, ((0, 0), (0, 0), (0, 0), (0, LANES - 3)))
    sel_flat = sel[..., :n_sel].reshape(-1)
    o_tr = _nsa_dec_attend(page_table, sel_flat, q_tr, gates_tr, ocmp_tr, kvrows, winrows, cache_half, cache_win4,
                           li, past=past, steps=steps, n_sel=n_sel)
    o_nsa = o_tr.reshape(n, gb, steps, ROW_PAD, HEAD_DIM)[:, :, :, :hpg].transpose(0, 2, 1, 3, 4).reshape(
        m, NSA_WIDTH)
    o_ret, s_ret = _retention(zplain, w["ret_gain"], state_ret, li, n=n, t=steps, c=steps)
    o_hgrn, s_hgrn = _hgrn(zplain, w["lb"], w["hgrn_gain"], state_hgrn, li, n=n, t=steps, c=steps)
    h = _layer_tail(h, o_nsa, o_ret, o_hgrn, zplain, p, w, m, m)
    kv_out = kvrows.reshape(n, steps, 4, gb, HEAD_DIM)
    win_new = winrows.reshape(n, steps, 2, gb, HEAD_DIM)
    return h, kv_out, win_new, s_ret, s_hgrn


def kernel(x_prompt, x_sample, cache_kv, cache_win, state_ret, state_hgrn, page_table, p_prompt, p_sample, norms,
           w_in, qk_gain, cmp_pos, cmp_w, out_gain, hgrn_lb, w_branch, w_o, w_gu, w_down, w_ple_proj, w_ple_gate):
    n_p, t_p, d = x_prompt.shape
    n_d, t_d, _ = x_sample.shape
    depth, n_pool, page = cache_kv.shape[:3]
    row_w = cache_kv.shape[3] * cache_kv.shape[4] * cache_kv.shape[5]
    cache4 = cache_kv.reshape(depth, n_pool, page, row_w)
    cache_half = cache_kv.reshape(depth, n_pool, page // SEL_BLOCK, SEL_BLOCK, row_w)
    wb = cache_win.shape[2]
    cache_win4 = cache_win.reshape(depth, n_d, wb, -1)
    lb_all = _hgrn_lower_bounds(hgrn_lb)
    hp = x_prompt.reshape(n_p * t_p, d)
    hs = x_sample.reshape(n_d * t_d, d)
    outs = [[] for _ in range(8)]
    for li in range(depth):
        w = _layer_weights(li, norms, w_in, qk_gain, cmp_pos, cmp_w, out_gain, lb_all, w_branch, w_o, w_gu, w_down,
                           w_ple_proj, w_ple_gate)
        hp, kv, win, s_r, s_h = _prompt_layer(hp, p_prompt[li].reshape(n_p * t_p, -1), w, n=n_p, t=t_p)
        for slot, val in zip((0, 2, 4, 6), (kv, win, s_r, s_h)):
            outs[slot].append(val)
        hs, kv, win_new, s_r, s_h = _decode_layer(hs, p_sample[li].reshape(n_d * t_d, -1), w, li, cache4, cache_half,
                                                  cache_win4, state_ret, state_hgrn, page_table, n=n_d, steps=t_d)
        win_all = jnp.concatenate([cache_win[li], win_new], axis=1)
        win = win_all[:, wb + t_d - min(WINDOW, wb + t_d):]
        for slot, val in zip((1, 3, 5, 7), (kv, win, s_r, s_h)):
            outs[slot].append(val)
    return (hp.reshape(n_p, t_p, d), hs.reshape(n_d, t_d, d)) + tuple(jnp.stack(o) for o in outs)
```

```python
import functools
import math

import jax
import jax.numpy as jnp
import numpy as np
from jax import lax
from jax.experimental import pallas as pl
from jax.experimental.pallas import tpu as pltpu

F32 = jnp.float32
BF16 = jnp.bfloat16

D_MODEL = 2048
HEAD_DIM = 128
NSA_HEADS = 8
NSA_KV_HEADS = 2
HEADS_PER_GROUP = NSA_HEADS // NSA_KV_HEADS
CMP_BLOCK = 32
CMP_STRIDE = 16
SEL_BLOCK = 64
N_SEL = 16
WINDOW = 512
RET_HEADS = 4
HGRN_HEADS = 4
D_FF = 5632
PLE_DIM = 256
NSA_WIDTH = NSA_HEADS * HEAD_DIM
RET_WIDTH = RET_HEADS * HEAD_DIM
HGRN_WIDTH = HGRN_HEADS * HEAD_DIM
KV_WIDTH = 6 * NSA_KV_HEADS * HEAD_DIM
GATE_WIDTH = 3 * NSA_HEADS
NEG_INF = -1e30
FORCE_SCORE = 1e6
EPS = 1e-6
ATT_SCALE = HEAD_DIM ** -0.5

LANES = 128
PLAIN_RET_BLK = 0
PLAIN_HGRN_BLK = (4 * RET_WIDTH) // LANES
PLAIN_MERGE_OFF = 4 * RET_WIDTH + 4 * HGRN_WIDTH
PLAIN_GATE_BLK = (PLAIN_MERGE_OFF + 3 * D_MODEL) // LANES
PLAIN_WIDTH = PLAIN_MERGE_OFF + 3 * D_MODEL + LANES

VMEM_LIMIT = 56 * 1024 * 1024


def _cp(sem, vmem=VMEM_LIMIT):
    return pltpu.CompilerParams(dimension_semantics=sem, vmem_limit_bytes=vmem)


def _rms(x):
    return x * lax.rsqrt(jnp.mean(x * x, axis=-1, keepdims=True) + EPS)


def _dot(a, b):
    return jnp.dot(a.astype(BF16), b.astype(BF16), preferred_element_type=F32)


def _dot_nt(a, b):
    return lax.dot_general(a.astype(BF16), b.astype(BF16), (((1,), (1,)), ((), ())),
                           preferred_element_type=F32)


def _dot_tn(a, b):
    rows = a.shape[0]
    if rows % LANES:
        pad = LANES - rows % LANES
        a = jnp.concatenate([a, jnp.zeros((pad, a.shape[1]), a.dtype)], axis=0)
        b = jnp.concatenate([b, jnp.zeros((pad, b.shape[1]), b.dtype)], axis=0)
    return _dot(a.T, b)


def _split_bf16(x):
    hi = x.astype(BF16)
    lo = (x - hi.astype(F32)).astype(BF16)
    return hi, lo


def _sigmoid(x):
    return 1.0 / (1.0 + jnp.exp(-x))


def _ffn_body(x_ref, g1_ref, wg_ref, wv_ref, wd_ref, g2_ref, h_ref, u_ref, xn_sc, acc_sc, *, nf):
    j = pl.program_id(1)

    @pl.when(j == 0)
    def _():
        xn_sc[...] = (_rms(x_ref[...]) * g1_ref[...]).astype(BF16)
        acc_sc[...] = jnp.zeros_like(acc_sc)

    xn = xn_sc[...]
    g = jnp.dot(xn, wg_ref[...], preferred_element_type=F32)
    v = jnp.dot(xn, wv_ref[...], preferred_element_type=F32)
    a = (g * _sigmoid(g) * v).astype(BF16)
    acc_sc[...] += jnp.dot(a, wd_ref[...], preferred_element_type=F32)

    @pl.when(j == nf - 1)
    def _():
        h = x_ref[...] + 0.5 * acc_sc[...]
        h_ref[...] = h
        u_ref[...] = (_rms(h) * g2_ref[...]).astype(BF16)


def _ffn(x, g1, w_gu, w_down, g2, li, which, *, tm, tf):
    m, d = x.shape
    f = w_down.shape[2]
    nf = f // tf
    return pl.pallas_call(
        functools.partial(_ffn_body, nf=nf),
        grid=(m // tm, nf),
        in_specs=[
            pl.BlockSpec((tm, d), lambda i, j: (i, 0)),
            pl.BlockSpec((1, d), lambda i, j: (0, 0)),
            pl.BlockSpec((None, None, d, tf), lambda i, j: (li, which, 0, j)),
            pl.BlockSpec((None, None, d, tf), lambda i, j: (li, which, 0, j + nf)),
            pl.BlockSpec((None, None, tf, d), lambda i, j: (li, which, j, 0)),
            pl.BlockSpec((1, d), lambda i, j: (0, 0)),
        ],
        out_specs=[
            pl.BlockSpec((tm, d), lambda i, j: (i, 0)),
            pl.BlockSpec((tm, d), lambda i, j: (i, 0)),
        ],
        out_shape=[jax.ShapeDtypeStruct((m, d), F32), jax.ShapeDtypeStruct((m, d), BF16)],
        scratch_shapes=[pltpu.VMEM((tm, d), BF16), pltpu.VMEM((tm, d), F32)],
        compiler_params=_cp(("parallel", "arbitrary")),
        name="ffn",
    )(x, g1, w_gu, w_gu, w_down, g2)


def _proj_plain_body(u_ref, w_ref, o_ref):
    o_ref[...] = jnp.dot(u_ref[...], w_ref[...], preferred_element_type=F32).astype(o_ref.dtype)


def _proj_norm_body(u_ref, w_ref, gain_ref, flag_ref, o_ref, *, tn):
    z = jnp.dot(u_ref[...], w_ref[...], preferred_element_type=F32)
    for c in range(tn // LANES):
        sl = slice(c * LANES, (c + 1) * LANES)
        zc = z[:, sl]
        normed = _rms(zc) * gain_ref[:, sl]
        o_ref[:, sl] = jnp.where(flag_ref[:, sl] > 0.5, normed, zc).astype(o_ref.dtype)


def _proj(u, w, li, *, col0, n, tm, tn, out_dtype, gain=None, flag=None, name):
    m, k = u.shape
    cb = col0 // tn
    in_specs = [pl.BlockSpec((tm, k), lambda i, j: (i, 0)), pl.BlockSpec((None, k, tn), lambda i, j: (li, 0, cb + j))]
    args = [u, w]
    if gain is None:
        body = _proj_plain_body
    else:
        body = functools.partial(_proj_norm_body, tn=tn)
        in_specs += [pl.BlockSpec((1, tn), lambda i, j: (0, j)), pl.BlockSpec((1, tn), lambda i, j: (0, j))]
        args += [gain, flag]
    return pl.pallas_call(
        body,
        grid=(m // tm, n // tn),
        in_specs=in_specs,
        out_specs=pl.BlockSpec((tm, tn), lambda i, j: (i, j)),
        out_shape=jax.ShapeDtypeStruct((m, n), out_dtype),
        compiler_params=_cp(("parallel", "arbitrary")),
        name=name,
    )(*args)


def _merge_body(oa_ref, ob_ref, oc_ref, m0_ref, m1_ref, m2_ref, h_ref, wb_ref, wo_ref, out_ref):
    ya = jnp.dot(oa_ref[...], wb_ref[0:NSA_WIDTH, :], preferred_element_type=F32)
    yb = jnp.dot(ob_ref[...], wb_ref[NSA_WIDTH:NSA_WIDTH + RET_WIDTH, :], preferred_element_type=F32)
    yc = jnp.dot(oc_ref[...], wb_ref[NSA_WIDTH + RET_WIDTH:, :], preferred_element_type=F32)
    mixed = _sigmoid(m0_ref[...]) * ya + _sigmoid(m1_ref[...]) * yb + _sigmoid(m2_ref[...]) * yc
    out_ref[...] = h_ref[...] + jnp.dot(mixed.astype(BF16), wo_ref[...], preferred_element_type=F32)


def _merge(o_nsa, o_ret, o_hgrn, zplain, h, w_branch, w_o, li, *, tm):
    m, d = h.shape
    mb = PLAIN_MERGE_OFF // d
    row = lambda i: (i, 0)
    layer = lambda i: (li, 0, 0)
    return pl.pallas_call(
        _merge_body,
        grid=(m // tm,),
        in_specs=[
            pl.BlockSpec((tm, NSA_WIDTH), row),
            pl.BlockSpec((tm, RET_WIDTH), row),
            pl.BlockSpec((tm, HGRN_WIDTH), row),
            pl.BlockSpec((tm, d), lambda i: (i, mb)),
            pl.BlockSpec((tm, d), lambda i: (i, mb + 1)),
            pl.BlockSpec((tm, d), lambda i: (i, mb + 2)),
            pl.BlockSpec((tm, d), row),
            pl.BlockSpec((None,) + w_branch.shape[1:], layer),
            pl.BlockSpec((None,) + w_o.shape[1:], layer),
        ],
        out_specs=pl.BlockSpec((tm, d), row),
        out_shape=jax.ShapeDtypeStruct((m, d), F32),
        compiler_params=_cp(("parallel",)),
        name="merge",
    )(o_nsa, o_ret, o_hgrn, zplain, zplain, zplain, h, w_branch, w_o)


def _ple_body(h_ref, u_ref, p_ref, wg_ref, wp_ref, out_ref):
    gate = _sigmoid(jnp.dot(u_ref[...], wg_ref[...], preferred_element_type=F32))
    proj = jnp.dot(p_ref[...].astype(BF16), wp_ref[...], preferred_element_type=F32)
    out_ref[...] = h_ref[...] + gate * proj


def _ple(h, u, p, w_gate, w_proj, li, *, tm):
    m, d = h.shape
    row = lambda i: (i, 0)
    layer = lambda i: (li, 0, 0)
    return pl.pallas_call(
        _ple_body,
        grid=(m // tm,),
        in_specs=[
            pl.BlockSpec((tm, d), row),
            pl.BlockSpec((tm, d), row),
            pl.BlockSpec((tm, p.shape[1]), row),
            pl.BlockSpec((None,) + w_gate.shape[1:], layer),
            pl.BlockSpec((None,) + w_proj.shape[1:], layer),
        ],
        out_specs=pl.BlockSpec((tm, d), row),
        out_shape=jax.ShapeDtypeStruct((m, d), F32),
        compiler_params=_cp(("parallel",)),
        name="ple",
    )(h, u, p, w_gate, w_proj)


def _retention_tables(c):
    lg = np.log1p(-np.exp2(-5.0 - np.arange(RET_HEADS, dtype=np.float64)))
    i = np.arange(c, dtype=np.float64)
    diff = i[:, None] - i[None, :]
    dmat = np.where(diff >= 0, np.exp(lg[:, None, None] * np.maximum(diff, 0.0)), 0.0)
    q_dec = np.exp(lg[:, None] * (i + 1.0))[..., None] * np.ones((1, 1, HEAD_DIM))
    k_dec = np.exp(lg[:, None] * (c - 1.0 - i))[..., None] * np.ones((1, 1, HEAD_DIM))
    c_dec = np.exp(lg * c)[:, None, None] * np.ones((1, 8, HEAD_DIM))
    return tuple(jnp.asarray(a, F32) for a in (dmat, q_dec, k_dec, c_dec))


def _retention_body(*refs, has_state, nch, c):
    if has_state:
        q_ref, k_ref, v_ref, g_ref, dm_ref, qd_ref, kd_ref, cd_ref, gain_ref, s0_ref, o_ref, s_ref = refs
    else:
        q_ref, k_ref, v_ref, g_ref, dm_ref, qd_ref, kd_ref, cd_ref, gain_ref, o_ref, s_ref = refs

    def chunk(ci, s):
        rows = pl.ds(pl.multiple_of(ci * c, c), c)
        q = q_ref[rows, :]
        k = k_ref[rows, :] * ATT_SCALE
        v = v_ref[rows, :]
        a = _dot_nt(q, k) * dm_ref[...]
        o = _dot(a, v) + _dot(q * qd_ref[...], s)
        g = g_ref[rows, :]
        o_ref[rows, :] = (_rms(o) * gain_ref[...] * (g * _sigmoid(g))).astype(o_ref.dtype)
        return s * cd_ref[0:1, :] + _dot_tn(k * kd_ref[...], v)

    s_init = s0_ref[...] if has_state else jnp.zeros((LANES, LANES), F32)
    s_ref[...] = lax.fori_loop(0, nch, chunk, s_init)


def _retention(zplain, gain, s0, li, *, n, t, c):
    nch = t // c
    dmat, q_dec, k_dec, c_dec = _retention_tables(c)
    hb = RET_HEADS
    seq = lambda w: (lambda i, h: (i, PLAIN_RET_BLK + w * hb + h))
    tab = lambda i, h: (h, 0, 0)
    in_specs = [pl.BlockSpec((t, LANES), seq(w)) for w in range(4)]
    in_specs += [
        pl.BlockSpec((None, c, c), tab),
        pl.BlockSpec((None, c, LANES), tab),
        pl.BlockSpec((None, c, LANES), tab),
        pl.BlockSpec((None, 8, LANES), tab),
        pl.BlockSpec((1, LANES), lambda i, h: (0, h)),
    ]
    args = [zplain] * 4 + [dmat, q_dec, k_dec, c_dec, gain]
    if s0 is not None:
        in_specs.append(pl.BlockSpec((None, None, None, LANES, LANES), lambda i, h: (li, i, h, 0, 0)))
        args.append(s0)
    return pl.pallas_call(
        functools.partial(_retention_body, has_state=s0 is not None, nch=nch, c=c),
        grid=(n, hb),
        in_specs=in_specs,
        out_specs=[
            pl.BlockSpec((t, LANES), lambda i, h: (i, h)),
            pl.BlockSpec((None, None, LANES, LANES), lambda i, h: (i, h, 0, 0)),
        ],
        out_shape=[
            jax.ShapeDtypeStruct((n * t, RET_WIDTH), BF16),
            jax.ShapeDtypeStruct((n, hb, LANES, LANES), F32),
        ],
        compiler_params=_cp(("parallel", "parallel")),
        name="retention",
    )(*args)


HGRN_SUB = 16


def _hgrn_body(*refs, has_state, nch, c):
    if has_state:
        q_ref, f_ref, v_ref, g_ref, lb_ref, gain_ref, s0_ref, o_ref, s_ref, st_sc, b_sc, k_sc, v_sc, o_sc = refs
    else:
        q_ref, f_ref, v_ref, g_ref, lb_ref, gain_ref, o_ref, s_ref, st_sc, b_sc, k_sc, v_sc, o_sc = refs
    ci = pl.program_id(2)

    @pl.when(ci == 0)
    def _():
        st_sc[...] = s0_ref[...].T if has_state else jnp.zeros_like(st_sc)

    lb = lb_ref[...]
    f = lb + (1.0 - lb) * _sigmoid(f_ref[...])
    logf = jnp.log(f)
    ri = lax.broadcasted_iota(jnp.int32, (c, c), 0)
    si = lax.broadcasted_iota(jnp.int32, (c, c), 1)
    tri = jnp.where(ri >= si, 1.0, 0.0).astype(BF16)
    hi = logf.astype(BF16)
    r1 = logf - hi.astype(F32)
    mid = r1.astype(BF16)
    lo = (r1 - mid.astype(F32)).astype(BF16)
    b = (jnp.dot(tri, hi, preferred_element_type=F32) + jnp.dot(tri, mid, preferred_element_type=F32)
         + jnp.dot(tri, lo, preferred_element_type=F32))
    q = q_ref[...]
    kk = 1.0 - f
    v = v_ref[...]
    b_sc[...] = b
    k_sc[...] = kk
    v_sc[...] = v
    st = st_sc[...]
    o_sc[...] = _dot_nt(q * jnp.exp(b), st)

    sub = min(HGRN_SUB, c)
    row_id = lax.broadcasted_iota(jnp.int32, (sub, LANES), 0)
    for blk in range(c // sub):
        r0 = blk * sub
        b_i = b[r0:r0 + sub]
        q_i = q[r0:r0 + sub]
        acc = jnp.zeros((sub, LANES), F32)
        if blk > 0:
            ref = b_sc[r0 - 1:r0, :]
            qt = q_i * jnp.exp(b_i - ref)
            kt = kk[0:r0] * jnp.exp(ref - b[0:r0])
            acc = _dot(_dot_nt(qt, kt), v[0:r0])
        for s in range(sub):
            b_s = b_sc[r0 + s:r0 + s + 1, :]
            k_s = k_sc[r0 + s:r0 + s + 1, :]
            v_s = v_sc[r0 + s:r0 + s + 1, :]
            w = q_i * jnp.exp(jnp.minimum(b_i - b_s, 0.0)) * k_s
            w = jnp.where(row_id >= s, w, 0.0)
            acc = acc + jnp.sum(w, axis=-1, keepdims=True) * v_s
        o_sc[r0:r0 + sub, :] += acc

    b_last = b_sc[c - 1:c, :]
    st_sc[...] = st * jnp.exp(b_last) + _dot_tn(v, kk * jnp.exp(b_last - b))
    g = g_ref[...]
    o_ref[...] = (_rms(o_sc[...]) * gain_ref[...] * (g * _sigmoid(g))).astype(o_ref.dtype)

    @pl.when(ci == nch - 1)
    def _():
        s_ref[...] = st_sc[...].T


def _hgrn(zplain, lb, gain, s0, li, *, n, t, c):
    nch = t // c
    hb = HGRN_HEADS
    row = lambda w: (lambda i, h, j: (i * nch + j, PLAIN_HGRN_BLK + w * hb + h))
    head = lambda i, h, j: (0, h)
    in_specs = [pl.BlockSpec((c, LANES), row(w)) for w in range(4)]
    in_specs += [pl.BlockSpec((1, LANES), head), pl.BlockSpec((1, LANES), head)]
    args = [zplain] * 4 + [lb, gain]
    if s0 is not None:
        in_specs.append(pl.BlockSpec((None, None, None, LANES, LANES), lambda i, h, j: (li, i, h, 0, 0)))
        args.append(s0)
    return pl.pallas_call(
        functools.partial(_hgrn_body, has_state=s0 is not None, nch=nch, c=c),
        grid=(n, hb, nch),
        in_specs=in_specs,
        out_specs=[
            pl.BlockSpec((c, LANES), lambda i, h, j: (i * nch + j, h)),
            pl.BlockSpec((None, None, LANES, LANES), lambda i, h, j: (i, h, 0, 0)),
        ],
        out_shape=[
            jax.ShapeDtypeStruct((n * t, HGRN_WIDTH), BF16),
            jax.ShapeDtypeStruct((n, hb, LANES, LANES), F32),
        ],
        scratch_shapes=[pltpu.VMEM((LANES, LANES), F32)] + [pltpu.VMEM((c, LANES), F32)] * 4,
        compiler_params=_cp(("parallel", "parallel", "arbitrary")),
        name="hgrn",
    )(*args)


def _lb_body(x_ref, o_ref):
    x = x_ref[...]
    e = jnp.exp(x - jnp.max(x, axis=0, keepdims=True))
    sm = e / jnp.sum(e, axis=0, keepdims=True)
    acc = jnp.zeros_like(sm[0:1])
    o_ref[0:1, :] = acc
    for layer in range(1, x.shape[0]):
        acc = acc + sm[layer:layer + 1]
        o_ref[layer:layer + 1, :] = acc


def _hgrn_lower_bounds(hgrn_lb):
    return pl.pallas_call(
        _lb_body,
        out_shape=jax.ShapeDtypeStruct(hgrn_lb.shape, F32),
        name="hgrn_lb",
    )(hgrn_lb.astype(F32))


def _group_sums(x, p0, p1):
    rows, width = x.shape
    xg = x.reshape(rows // CMP_STRIDE, CMP_STRIDE, width)
    return jnp.sum(xg * p0[None], axis=1), jnp.sum(xg * p1[None], axis=1)


def _cmp_partial_body(x_ref, p0_ref, p1_ref, a0_ref, a1_ref):
    a0, a1 = _group_sums(x_ref[...], p0_ref[...], p1_ref[...])
    a0_ref[...] = a0
    a1_ref[...] = a1


def _cmp_partial(kvrows, p0, p1, *, rb):
    m = kvrows.shape[0]
    w = p0.shape[1]
    const = lambda i: (0, 0)
    out = jax.ShapeDtypeStruct((m // CMP_STRIDE, w), F32)
    return pl.pallas_call(
        _cmp_partial_body,
        grid=(m // rb,),
        in_specs=[pl.BlockSpec((rb, w), lambda i: (i, 0)), pl.BlockSpec(p0.shape, const), pl.BlockSpec(p1.shape, const)],
        out_specs=[pl.BlockSpec((rb // CMP_STRIDE, w), lambda i: (i, 0))] * 2,
        out_shape=[out, out],
        compiler_params=_cp(("parallel",)),
        name="cmp_partial",
    )(kvrows, p0, p1)


CACHE_SUB = 4 * NSA_KV_HEADS


def _cmp_partial_paged_body(pt_ref, *refs, pages, page):
    x_refs = refs[:pages]
    p0_ref, p1_ref, a0_ref, a1_ref = refs[pages:]
    gp = page // CMP_STRIDE
    for k in range(pages):
        for s in range(2 * NSA_KV_HEADS):
            sl = slice(s * HEAD_DIM, (s + 1) * HEAD_DIM)
            x = x_refs[k][pl.ds(s, page, stride=CACHE_SUB), :]
            a0, a1 = _group_sums(x, p0_ref[:, sl], p1_ref[:, sl])
            a0_ref[k * gp:(k + 1) * gp, sl] = a0
            a1_ref[k * gp:(k + 1) * gp, sl] = a1


def _cmp_partial_paged(cache_rows, page_table, p0, p1, li, *, pages):
    n, n_pages = page_table.shape
    page = cache_rows.shape[2] // CACHE_SUB
    w = p0.shape[1]
    gp = page // CMP_STRIDE
    const = lambda i, c, pt: (0, 0)
    page_spec = lambda k: pl.BlockSpec((None, None, page * CACHE_SUB, HEAD_DIM),
                                       lambda i, c, pt: (li, pt[i, c * pages + k], 0, 0))
    out = jax.ShapeDtypeStruct((n, n_pages * gp, w), F32)
    return pl.pallas_call(
        functools.partial(_cmp_partial_paged_body, pages=pages, page=page),
        grid_spec=pltpu.PrefetchScalarGridSpec(
            num_scalar_prefetch=1,
            grid=(n, n_pages // pages),
            in_specs=[page_spec(k) for k in range(pages)] + [pl.BlockSpec(p0.shape, const), pl.BlockSpec(p1.shape, const)],
            out_specs=[pl.BlockSpec((None, pages * gp, w), lambda i, c, pt: (i, c, 0))] * 2,
        ),
        out_shape=[out, out],
        compiler_params=_cp(("parallel", "arbitrary")),
        name="cmp_partial_paged",
    )(page_table, *([cache_rows] * pages), p0, p1)


def _cmp_final_body(a0_ref, a1_ref, wk_ref, wv_ref, gain_ref, kc_ref, vc_ref):
    ng = a0_ref.shape[0]
    agg = a0_ref[...] + pltpu.roll(a1_ref[...], ng - 1, 0)
    for g in range(NSA_KV_HEADS):
        sl = slice(g * HEAD_DIM, (g + 1) * HEAD_DIM)
        ak = agg[:, sl]
        av = agg[:, NSA_KV_HEADS * HEAD_DIM + g * HEAD_DIM:NSA_KV_HEADS * HEAD_DIM + (g + 1) * HEAD_DIM]
        kc_ref[:, sl] = (_rms(_dot(ak, wk_ref[...])) * gain_ref[...]).astype(kc_ref.dtype)
        vc_ref[:, sl] = _dot(av, wv_ref[...]).astype(vc_ref.dtype)


def _cmp_final(a0, a1, w_k, w_v, gain):
    n, ng, w = a0.shape
    const = lambda i: (0, 0)
    out = jax.ShapeDtypeStruct((n, ng, NSA_KV_HEADS * HEAD_DIM), BF16)
    return pl.pallas_call(
        _cmp_final_body,
        grid=(n,),
        in_specs=[
            pl.BlockSpec((None, ng, w), lambda i: (i, 0, 0)),
            pl.BlockSpec((None, ng, w), lambda i: (i, 0, 0)),
            pl.BlockSpec(w_k.shape, const),
            pl.BlockSpec(w_v.shape, const),
            pl.BlockSpec(gain.shape, const),
        ],
        out_specs=[pl.BlockSpec((None, ng, NSA_KV_HEADS * HEAD_DIM), lambda i: (i, 0, 0))] * 2,
        out_shape=[out, out],
        compiler_params=_cp(("parallel",)),
        name="cmp_final",
    )(a0, a1, w_k, w_v, gain)


def _overlap_matrix(n_cmp_pad, n_slc, width):
    ci = lax.broadcasted_iota(jnp.int32, (n_cmp_pad, width), 0)
    sj = lax.broadcasted_iota(jnp.int32, (n_cmp_pad, width), 1)
    c_start = ci * CMP_STRIDE
    s_start = sj * SEL_BLOCK
    hit = ((c_start < s_start + SEL_BLOCK) & (c_start + CMP_BLOCK > s_start)
           & (ci < n_cmp_pad - 1) & (sj < n_slc))
    return jnp.where(hit, 1.0, 0.0).astype(BF16)


def _block_scores(imp, qpos_i, n_cmp_pad, n_slc, width):
    hi, lo = _split_bf16(imp)
    ov = _overlap_matrix(n_cmp_pad, n_slc, width)
    score = jnp.dot(hi, ov, preferred_element_type=F32) + jnp.dot(lo, ov, preferred_element_type=F32)
    sj = lax.broadcasted_iota(jnp.int32, score.shape, 1)
    cur = qpos_i // SEL_BLOCK
    forced = (sj == 0) | (sj == cur) | (sj == cur - 1)
    score = jnp.where(forced, FORCE_SCORE, score)
    score = jnp.where(sj * SEL_BLOCK <= qpos_i, score, -1.0)
    return jnp.where(sj < n_slc, score, -2.0)


def _masked_softmax(s, allow):
    s = jnp.where(allow, s, NEG_INF)
    m = jnp.max(s, axis=-1, keepdims=True)
    e = jnp.where(allow, jnp.exp(s - m), 0.0)
    return e / jnp.maximum(jnp.sum(e, axis=-1, keepdims=True), 1e-30)


SEL_KEY_BLOCK = 256


def _nsa_prompt_body(q_ref, zg_ref, kc_ref, vc_ref, ks_ref, vs_ref, kw_ref, vw_ref, o_ref, *, t_len, tq):
    g = pl.program_id(1)
    t0 = pl.program_id(2) * tq
    n_cmp_pad = kc_ref.shape[0]
    n_slc = -(-t_len // SEL_BLOCK)
    hpg = HEADS_PER_GROUP
    qpos_i = t0 + lax.broadcasted_iota(jnp.int32, (tq, 1), 0)
    slopes = [jnp.where(g == 0, 2.0 ** -(r + 1), 2.0 ** -(r + 1 + hpg)) for r in range(hpg)]
    qs = [q_ref[:, r * HEAD_DIM:(r + 1) * HEAD_DIM] for r in range(hpg)]

    ci = lax.broadcasted_iota(jnp.int32, (tq, n_cmp_pad), 1)
    cdist_i = qpos_i - (ci * CMP_STRIDE + CMP_BLOCK - 1)
    callow = (cdist_i >= 0) & (ci < n_cmp_pad - 1)
    cdist = cdist_i.astype(F32)
    kc = kc_ref[...]
    vc = vc_ref[...]
    imp = jnp.zeros((tq, n_cmp_pad), F32)
    o_cmp = []
    for r in range(hpg):
        p = _masked_softmax(_dot_nt(qs[r], kc) * ATT_SCALE - slopes[r] * cdist, callow)
        o_cmp.append(_dot(p, vc))
        imp = imp + p

    score = _block_scores(imp, qpos_i, n_cmp_pad, n_slc, LANES)
    s_t = score.T[0:n_slc]
    jrow = lax.broadcasted_iota(jnp.int32, (n_slc, tq), 0)
    rank = jnp.zeros((n_slc, tq), jnp.int32)
    for jp in range(n_slc):
        row = s_t[jp:jp + 1, :]
        beats = (row > s_t) | ((row == s_t) & (jp < jrow))
        rank = rank + jnp.where(beats, 1, 0)
    sel_t = jnp.where((rank < N_SEL) & (s_t >= 0.0), 1.0, 0.0)
    sel_t = jnp.concatenate([sel_t, jnp.zeros((LANES - n_slc, tq), F32)], axis=0)
    sel = sel_t.T.astype(BF16)

    kb = SEL_KEY_BLOCK
    nkb = (t0 + tq + kb - 1) // kb

    def sel_step(i, carry):
        k0 = pl.multiple_of(i * kb, kb)
        kblk = ks_ref[pl.ds(k0, kb), :].astype(BF16)
        vblk = vs_ref[pl.ds(k0, kb), :].astype(BF16)
        kpos_i = k0 + lax.broadcasted_iota(jnp.int32, (1, kb), 1)
        ej = lax.broadcasted_iota(jnp.int32, (LANES, kb), 0)
        ec = lax.broadcasted_iota(jnp.int32, (LANES, kb), 1)
        expand = jnp.where(ej == (k0 + ec) // SEL_BLOCK, 1.0, 0.0).astype(BF16)
        allow = (jnp.dot(sel, expand, preferred_element_type=F32) > 0.5) & (kpos_i <= qpos_i)
        kpos = kpos_i.astype(F32)
        out = []
        for r in range(hpg):
            m_old, l_old, acc_old = carry[3 * r:3 * r + 3]
            s = jnp.where(allow, _dot_nt(qs[r], kblk) * ATT_SCALE + slopes[r] * kpos, NEG_INF)
            m_new = jnp.maximum(m_old, jnp.max(s, axis=-1, keepdims=True))
            alpha = jnp.exp(m_old - m_new)
            e = jnp.exp(s - m_new)
            out += [m_new, alpha * l_old + jnp.sum(e, axis=-1, keepdims=True),
                    alpha * acc_old + _dot(e, vblk)]
        return tuple(out)

    init = (jnp.full((tq, 1), NEG_INF, F32), jnp.zeros((tq, 1), F32), jnp.zeros((tq, HEAD_DIM), F32)) * hpg
    fin = lax.fori_loop(0, nkb, sel_step, init)
    o_slc = [fin[3 * r + 2] / fin[3 * r + 1] for r in range(hpg)]

    wk = WINDOW + tq
    ws = pl.multiple_of(jnp.clip(t0 - WINDOW, 0, t_len - wk), LANES)
    kw = kw_ref[pl.ds(ws, wk), :].astype(BF16)
    vw = vw_ref[pl.ds(ws, wk), :].astype(BF16)
    wpos_i = ws + lax.broadcasted_iota(jnp.int32, (1, wk), 1)
    wdist_i = qpos_i - wpos_i
    wallow = (wdist_i >= 0) & (wdist_i <= WINDOW)
    wpos = wpos_i.astype(F32)
    gates = _sigmoid(zg_ref[...])
    for r in range(hpg):
        s = jnp.where(wallow, _dot_nt(qs[r], kw) * ATT_SCALE + slopes[r] * wpos, NEG_INF)
        e = jnp.exp(s - jnp.max(s, axis=-1, keepdims=True))
        o_win = _dot(e, vw) / jnp.sum(e, axis=-1, keepdims=True)
        gate = [jnp.where(g == 0, gates[:, 3 * r + b:3 * r + b + 1],
                          gates[:, 3 * (r + hpg) + b:3 * (r + hpg) + b + 1]) for b in range(3)]
        o = gate[0] * o_cmp[r] + gate[1] * o_slc[r] + gate[2] * o_win
        o_ref[:, r * HEAD_DIM:(r + 1) * HEAD_DIM] = o.astype(o_ref.dtype)


def _nsa_prompt(q, zplain, kc, vc, kvrows, winrows, *, n, t, tq):
    nt = t // tq
    gw = HEADS_PER_GROUP * HEAD_DIM
    n_cmp_pad = kc.shape[1]
    g_blocks = NSA_KV_HEADS
    qrow = lambda i, g, j: (i * nt + j, g)
    seq = lambda blk: (lambda i, g, j: (i, blk + g))
    return pl.pallas_call(
        functools.partial(_nsa_prompt_body, t_len=t, tq=tq),
        grid=(n, NSA_KV_HEADS, nt),
        in_specs=[
            pl.BlockSpec((tq, gw), qrow),
            pl.BlockSpec((tq, LANES), lambda i, g, j: (i * nt + j, PLAIN_GATE_BLK)),
            pl.BlockSpec((None, n_cmp_pad, HEAD_DIM), lambda i, g, j: (i, 0, g)),
            pl.BlockSpec((None, n_cmp_pad, HEAD_DIM), lambda i, g, j: (i, 0, g)),
            pl.BlockSpec((t, HEAD_DIM), seq(2 * g_blocks)),
            pl.BlockSpec((t, HEAD_DIM), seq(3 * g_blocks)),
            pl.BlockSpec((t, HEAD_DIM), seq(0)),
            pl.BlockSpec((t, HEAD_DIM), seq(g_blocks)),
        ],
        out_specs=pl.BlockSpec((tq, gw), qrow),
        out_shape=jax.ShapeDtypeStruct((n * t, NSA_WIDTH), BF16),
        compiler_params=_cp(("parallel", "parallel", "arbitrary")),
        name="nsa_prompt",
    )(q, zplain, kc, vc, kvrows, kvrows, winrows, winrows)


KV_OFF = NSA_WIDTH
GATE_OFF = KV_OFF + KV_WIDTH
PLAIN_OFF = GATE_OFF + GATE_WIDTH
SLOT_WIDTH = NSA_KV_HEADS * HEAD_DIM


def _stacked_weights(w_in, cmp_w, w_branch, w_o, w_gu, w_down, w_ple_proj, w_ple_gate):
    w_gate = jnp.pad(w_in[:, :, GATE_OFF:PLAIN_OFF], ((0, 0), (0, 0), (0, LANES - GATE_WIDTH)))
    return dict(
        w_head=w_in[:, :, :GATE_OFF].astype(BF16),
        w_plain=jnp.concatenate([w_in[:, :, PLAIN_OFF:], w_gate], axis=2).astype(BF16),
        w_phi=cmp_w.astype(BF16),
        w_branch=w_branch.astype(BF16),
        w_o=w_o.astype(BF16),
        w_gu=w_gu.astype(BF16),
        w_down=w_down.astype(BF16),
        w_ple_proj=w_ple_proj.astype(BF16),
        w_ple_gate=w_ple_gate.astype(BF16),
    )


def _layer_weights(li, stacked, norms, qk_gain, cmp_pos, out_gain, lb_all):
    ones = jnp.ones((SLOT_WIDTH,), F32)
    zeros = jnp.zeros((SLOT_WIDTH,), F32)
    gain = lambda i: jnp.tile(qk_gain[li, i], NSA_KV_HEADS)
    row = lambda v: v.reshape(1, -1).astype(F32)
    cp = cmp_pos[li]
    half = lambda w, m: w[m * CMP_STRIDE:(m + 1) * CMP_STRIDE]
    pos = lambda m: jnp.concatenate([half(cp[0], m)] * NSA_KV_HEADS + [half(cp[1], m)] * NSA_KV_HEADS, axis=1)
    w = dict(stacked)
    w.update(
        li=li,
        norm=[row(norms[li, i]) for i in range(4)],
        q_gain=row(jnp.tile(qk_gain[li, 0], NSA_HEADS)),
        q_flag=jnp.ones((1, NSA_WIDTH), F32),
        kv_gain=row(jnp.concatenate([ones, ones, gain(1), ones])),
        kv_flag=row(jnp.concatenate([zeros, zeros, ones, zeros])),
        win_gain=row(jnp.concatenate([gain(2), ones])),
        win_flag=row(jnp.concatenate([ones, zeros])),
        cmp_p0=pos(0).astype(F32),
        cmp_p1=pos(1).astype(F32),
        w_phi_k=stacked["w_phi"][li, 0],
        w_phi_v=stacked["w_phi"][li, 1],
        kc_gain=row(qk_gain[li, 3]),
        ret_gain=row(out_gain[li, 0]),
        hgrn_gain=row(out_gain[li, 1]),
        lb=lb_all[li:li + 1],
    )
    return w


def _tiles(m):
    big = 512 if m % 512 == 0 else m
    small = 256 if m % 256 == 0 else m
    return big, small


def _projections(u, w, tm):
    li = w["li"]
    head = functools.partial(_proj, u, w["w_head"], li, tm=tm, tn=512)
    q = head(col0=0, n=NSA_WIDTH, out_dtype=BF16, gain=w["q_gain"], flag=w["q_flag"], name="proj_q")
    kvrows = head(col0=KV_OFF, n=4 * SLOT_WIDTH, out_dtype=F32, gain=w["kv_gain"], flag=w["kv_flag"], name="proj_kv")
    winrows = head(col0=KV_OFF + 4 * SLOT_WIDTH, n=2 * SLOT_WIDTH, out_dtype=F32, gain=w["win_gain"],
                   flag=w["win_flag"], name="proj_win")
    zplain = _proj(u, w["w_plain"], li, col0=0, n=PLAIN_WIDTH, tm=tm, tn=1152, out_dtype=F32, name="proj_plain")
    return q, kvrows, winrows, zplain


def _layer_tail(h, o_nsa, o_ret, o_hgrn, zplain, p, w, tm, ts):
    li = w["li"]
    h = _merge(o_nsa, o_ret, o_hgrn, zplain, h, w["w_branch"], w["w_o"], li, tm=ts)
    h, u = _ffn(h, w["norm"][2], w["w_gu"], w["w_down"], w["norm"][3], li, 1, tm=tm, tf=512)
    return _ple(h, u, p, w["w_ple_gate"], w["w_ple_proj"], li, tm=ts)


def _prompt_layer(x, p, w, *, n, t):
    tm, ts = _tiles(n * t)
    h, u = _ffn(x, w["norm"][0], w["w_gu"], w["w_down"], w["norm"][1], w["li"], 0, tm=tm, tf=512)
    q, kvrows, winrows, zplain = _projections(u, w, 1024 if (n * t) % 1024 == 0 else tm)
    a0, a1 = _cmp_partial(kvrows, w["cmp_p0"], w["cmp_p1"], rb=min(1024, t))
    ng = t // CMP_STRIDE
    kc, vc = _cmp_final(a0.reshape(n, ng, -1), a1.reshape(n, ng, -1), w["w_phi_k"], w["w_phi_v"], w["kc_gain"])
    o_nsa = _nsa_prompt(q, zplain, kc, vc, kvrows, winrows, n=n, t=t, tq=2 * LANES)
    o_ret, s_ret = _retention(zplain, w["ret_gain"], None, 0, n=n, t=t, c=LANES)
    o_hgrn, s_hgrn = _hgrn(zplain, w["lb"], w["hgrn_gain"], None, 0, n=n, t=t, c=LANES)
    h = _layer_tail(h, o_nsa, o_ret, o_hgrn, zplain, p, w, tm, ts)
    kv_out = kvrows.reshape(n, t, 4, NSA_KV_HEADS, HEAD_DIM)
    win_out = winrows.reshape(n, t, 2, NSA_KV_HEADS, HEAD_DIM)[:, t - min(WINDOW, t):]
    return h, kv_out, win_out, s_ret, s_hgrn


def _nsa_dec_cmp_body(q_ref, kc_ref, vc_ref, ocmp_ref, sel_ref, *, past, steps, n_slc):
    g = pl.program_id(1)
    nc = kc_ref.shape[0]
    hpg = HEADS_PER_GROUP
    rows = hpg * steps
    ri = lax.broadcasted_iota(jnp.int32, (rows, 1), 0)
    qpos_i = past + ri % steps
    slope = jnp.exp2(-(ri // steps + 1 + g * hpg).astype(F32))
    ci = lax.broadcasted_iota(jnp.int32, (rows, nc), 1)
    dist_i = qpos_i - (ci * CMP_STRIDE + CMP_BLOCK - 1)
    allow = (dist_i >= 0) & (ci < nc - 1)
    p = _masked_softmax(_dot_nt(q_ref[...], kc_ref[...]) * ATT_SCALE - slope * dist_i.astype(F32), allow)
    ocmp_ref[...] = _dot(p, vc_ref[...])
    imp = p[0:steps]
    for r in range(1, hpg):
        imp = imp + p[r * steps:(r + 1) * steps]

    width = -(-n_slc // LANES) * LANES
    qpos8 = past + lax.broadcasted_iota(jnp.int32, (steps, 1), 0)
    score = _block_scores(imp, qpos8, nc, n_slc, width)
    sj = lax.broadcasted_iota(jnp.int32, score.shape, 1)
    lane = lax.broadcasted_iota(jnp.int32, (steps, LANES), 1)
    picked = jnp.full((steps, LANES), -1, jnp.int32)
    for it in range(min(N_SEL, n_slc)):
        m = jnp.max(score, axis=-1, keepdims=True)
        idx = jnp.min(jnp.where(score == m, sj, width), axis=-1, keepdims=True)
        picked = jnp.where(lane == it, jnp.where(m >= 0.0, idx, -1), picked)
        score = jnp.where(sj == idx, -3.0, score)
    sel_ref[...] = picked


def _nsa_dec_cmp(q_rt, kc, vc, *, past, steps, n_slc):
    n, ng = kc.shape[0], kc.shape[1]
    rows = HEADS_PER_GROUP * steps
    grp = lambda i, g: (i, g, 0, 0)
    return pl.pallas_call(
        functools.partial(_nsa_dec_cmp_body, past=past, steps=steps, n_slc=n_slc),
        grid=(n, NSA_KV_HEADS),
        in_specs=[
            pl.BlockSpec((None, None, rows, HEAD_DIM), grp),
            pl.BlockSpec((None, ng, HEAD_DIM), lambda i, g: (i, 0, g)),
            pl.BlockSpec((None, ng, HEAD_DIM), lambda i, g: (i, 0, g)),
        ],
        out_specs=[
            pl.BlockSpec((None, None, rows, HEAD_DIM), grp),
            pl.BlockSpec((None, None, steps, LANES), grp),
        ],
        out_shape=[
            jax.ShapeDtypeStruct((n, NSA_KV_HEADS, rows, HEAD_DIM), F32),
            jax.ShapeDtypeStruct((n, NSA_KV_HEADS, steps, LANES), jnp.int32),
        ],
        compiler_params=_cp(("parallel", "parallel")),
        name="nsa_dec_cmp",
    )(q_rt, kc, vc)


ROW_PAD = 8


def _nsa_dec_attend_body(pt_ref, sel_ref, q_ref, gate_ref, ocmp_ref, kvn_ref, wn_ref, wp_ref, *refs,
                         past, steps, n_sel, past_blocks):
    gb = NSA_KV_HEADS
    blk_refs = refs[:gb * n_sel]
    o_ref = refs[gb * n_sel]
    i = pl.program_id(0)
    t = pl.program_id(1)
    qpos = past + t
    head = jnp.minimum(lax.broadcasted_iota(jnp.int32, (ROW_PAD, 1), 0), HEADS_PER_GROUP - 1)
    new_i = lax.broadcasted_iota(jnp.int32, (1, steps), 1)
    new_allow = new_i <= t
    new_dist = (t - new_i).astype(F32)
    cols = n_sel * SEL_BLOCK
    col = lax.broadcasted_iota(jnp.int32, (1, cols), 1)
    col_pick = col // SEL_BLOCK
    head_cols = lambda ref, blk: ref[:, blk * HEAD_DIM:(blk + 1) * HEAD_DIM]

    def two_part_attention(s_a, allow_a, v_a, s_b, allow_b, v_b):
        s_a = jnp.where(allow_a, s_a, NEG_INF)
        s_b = jnp.where(allow_b, s_b, NEG_INF)
        m = jnp.maximum(jnp.max(s_a, axis=-1, keepdims=True), jnp.max(s_b, axis=-1, keepdims=True))
        e_a = jnp.where(allow_a, jnp.exp(s_a - m), 0.0)
        e_b = jnp.where(allow_b, jnp.exp(s_b - m), 0.0)
        l = jnp.sum(e_a, axis=-1, keepdims=True) + jnp.sum(e_b, axis=-1, keepdims=True)
        return (_dot(e_a, v_a) + _dot(e_b, v_b)) / l

    for g in range(gb):
        q = q_ref[g]
        slope = jnp.exp2(-(head + 1 + g * HEADS_PER_GROUP).astype(F32))
        picked = [blk_refs[g * n_sel + k] for k in range(n_sel)]
        kall = jnp.concatenate([r[pl.ds(2 * gb + g, SEL_BLOCK, stride=CACHE_SUB), :] for r in picked], axis=0)
        vall = jnp.concatenate([r[pl.ds(3 * gb + g, SEL_BLOCK, stride=CACHE_SUB), :] for r in picked], axis=0)
        base = ((i * gb + g) * steps + t) * n_sel
        blk = jnp.zeros((1, cols), jnp.int32)
        for k in range(n_sel):
            blk = jnp.where(col_pick == k, sel_ref[base + k], blk)
        dist_i = qpos - (blk * SEL_BLOCK + col % SEL_BLOCK)
        allow = (blk >= 0) & (blk < past_blocks) & (dist_i >= 0)
        s_sel = _dot_nt(q, kall) * ATT_SCALE - slope * dist_i.astype(F32)
        s_new = _dot_nt(q, head_cols(kvn_ref, 2 * gb + g)) * ATT_SCALE - slope * new_dist
        o_slc = two_part_attention(s_sel, allow, vall, s_new, new_allow, head_cols(kvn_ref, 3 * gb + g))

        wb = wp_ref.shape[0] // (2 * gb)
        kwp = wp_ref[pl.ds(g, wb, stride=2 * gb), :]
        vwp = wp_ref[pl.ds(gb + g, wb, stride=2 * gb), :]
        wdist_i = qpos - (past - wb + lax.broadcasted_iota(jnp.int32, (1, wb), 1))
        wallow = (wdist_i >= 0) & (wdist_i <= WINDOW)
        s_wp = _dot_nt(q, kwp) * ATT_SCALE - slope * wdist_i.astype(F32)
        s_wn = _dot_nt(q, head_cols(wn_ref, g)) * ATT_SCALE - slope * new_dist
        o_win = two_part_attention(s_wp, wallow, vwp, s_wn, new_allow, head_cols(wn_ref, gb + g))

        gates = _sigmoid(gate_ref[g])
        o = gates[:, 0:1] * ocmp_ref[g] + gates[:, 1:2] * o_slc + gates[:, 2:3] * o_win
        o_ref[g] = o.astype(o_ref.dtype)


def _nsa_dec_attend(page_table, sel_flat, q_tr, gates_tr, ocmp_tr, kvrows, winrows, cache_half, win_rows_view, li, *,
                    past, steps, n_sel):
    n = q_tr.shape[0]
    rows = steps * ROW_PAD
    gb = NSA_KV_HEADS
    halves = cache_half.shape[2]
    past_blocks = past // SEL_BLOCK
    step = lambda i, t, pt, sel: (i, 0, t, 0)
    new = lambda i, t, pt, sel: (i, 0)

    def gather(g, k):
        def index(i, t, pt, sel):
            j = jnp.clip(sel[((i * gb + g) * steps + t) * n_sel + k], 0, past_blocks - 1)
            return (li, pt[i, j // halves], j % halves, 0, 0)
        return pl.BlockSpec((None, None, None, SEL_BLOCK * CACHE_SUB, HEAD_DIM), index)

    in_specs = [
        pl.BlockSpec((None, gb, ROW_PAD, HEAD_DIM), step),
        pl.BlockSpec((None, gb, ROW_PAD, LANES), step),
        pl.BlockSpec((None, gb, ROW_PAD, HEAD_DIM), step),
        pl.BlockSpec((steps, kvrows.shape[1]), new),
        pl.BlockSpec((steps, winrows.shape[1]), new),
        pl.BlockSpec((None, None, win_rows_view.shape[2], HEAD_DIM), lambda i, t, pt, sel: (li, i, 0, 0)),
    ]
    in_specs += [gather(g, k) for g in range(gb) for k in range(n_sel)]
    return pl.pallas_call(
        functools.partial(_nsa_dec_attend_body, past=past, steps=steps, n_sel=n_sel, past_blocks=past_blocks),
        grid_spec=pltpu.PrefetchScalarGridSpec(
            num_scalar_prefetch=2,
            grid=(n, steps),
            in_specs=in_specs,
            out_specs=pl.BlockSpec((None, gb, ROW_PAD, HEAD_DIM), step),
        ),
        out_shape=jax.ShapeDtypeStruct((n, gb, rows, HEAD_DIM), BF16),
        compiler_params=_cp(("parallel", "arbitrary")),
        name="nsa_dec_attend",
    )(page_table, sel_flat, q_tr, gates_tr, ocmp_tr, kvrows, winrows, win_rows_view, *([cache_half] * (gb * n_sel)))


def _decode_layer(x, p, w, cache_rows, cache_half, win_rows_view, state_ret, state_hgrn, page_table, *, n, steps):
    m = n * steps
    li = w["li"]
    hpg, gb = HEADS_PER_GROUP, NSA_KV_HEADS
    past = page_table.shape[1] * (cache_rows.shape[2] // CACHE_SUB)
    n_slc = -(-(past + steps) // SEL_BLOCK)
    n_sel = min(N_SEL, n_slc)
    h, u = _ffn(x, w["norm"][0], w["w_gu"], w["w_down"], w["norm"][1], li, 0, tm=m, tf=512)
    q, kvrows, winrows, zplain = _projections(u, w, m)
    a0, a1 = _cmp_partial_paged(cache_rows, page_table, w["cmp_p0"], w["cmp_p1"], li, pages=8)
    kc, vc = _cmp_final(a0, a1, w["w_phi_k"], w["w_phi_v"], w["kc_gain"])
    q5 = q.reshape(n, steps, gb, hpg, HEAD_DIM)
    q_rt = q5.transpose(0, 2, 3, 1, 4).reshape(n, gb, hpg * steps, HEAD_DIM)
    o_cmp, sel = _nsa_dec_cmp(q_rt, kc, vc, past=past, steps=steps, n_slc=n_slc)
    pad_heads = lambda a: jnp.pad(a, ((0, 0), (0, 0), (0, 0), (0, ROW_PAD - hpg), (0, 0)))
    rows = steps * ROW_PAD
    q_tr = pad_heads(q5.transpose(0, 2, 1, 3, 4)).reshape(n, gb, rows, HEAD_DIM)
    ocmp_tr = pad_heads(o_cmp.reshape(n, gb, hpg, steps, HEAD_DIM).transpose(0, 1, 3, 2, 4)).reshape(
        n, gb, rows, HEAD_DIM)
    zg = zplain[:, PLAIN_GATE_BLK * LANES:PLAIN_GATE_BLK * LANES + GATE_WIDTH].reshape(n, steps, gb, hpg, 3)
    gates_tr = pad_heads(zg.transpose(0, 2, 1, 3, 4)).reshape(n, gb, rows, 3)
    gates_tr = jnp.pad(gates_tr, ((0, 0), (0, 0), (0, 0), (0, LANES - 3)))
    sel_flat = sel[..., :n_sel].reshape(-1)
    o_tr = _nsa_dec_attend(page_table, sel_flat, q_tr, gates_tr, ocmp_tr, kvrows, winrows, cache_half, win_rows_view,
                           li, past=past, steps=steps, n_sel=n_sel)
    o_nsa = o_tr.reshape(n, gb, steps, ROW_PAD, HEAD_DIM)[:, :, :, :hpg].transpose(0, 2, 1, 3, 4).reshape(
        m, NSA_WIDTH)
    o_ret, s_ret = _retention(zplain, w["ret_gain"], state_ret, li, n=n, t=steps, c=steps)
    o_hgrn, s_hgrn = _hgrn(zplain, w["lb"], w["hgrn_gain"], state_hgrn, li, n=n, t=steps, c=steps)
    h = _layer_tail(h, o_nsa, o_ret, o_hgrn, zplain, p, w, m, m)
    kv_out = kvrows.reshape(n, steps, 4, gb, HEAD_DIM)
    win_new = winrows.reshape(n, steps, 2, gb, HEAD_DIM)
    return h, kv_out, win_new, s_ret, s_hgrn


def kernel(x_prompt, x_sample, cache_kv, cache_win, state_ret, state_hgrn, page_table, p_prompt, p_sample, norms,
           w_in, qk_gain, cmp_pos, cmp_w, out_gain, hgrn_lb, w_branch, w_o, w_gu, w_down, w_ple_proj, w_ple_gate):
    n_p, t_p, d = x_prompt.shape
    n_d, t_d, _ = x_sample.shape
    depth, n_pool, page = cache_kv.shape[:3]
    cache_rows = cache_kv.reshape(depth, n_pool, page * CACHE_SUB, HEAD_DIM)
    cache_half = cache_kv.reshape(depth, n_pool, page // SEL_BLOCK, SEL_BLOCK * CACHE_SUB, HEAD_DIM)
    wb = cache_win.shape[2]
    win_rows_view = cache_win.reshape(depth, n_d, wb * 2 * NSA_KV_HEADS, HEAD_DIM)
    lb_all = _hgrn_lower_bounds(hgrn_lb)
    stacked = _stacked_weights(w_in, cmp_w, w_branch, w_o, w_gu, w_down, w_ple_proj, w_ple_gate)
    hp = x_prompt.reshape(n_p * t_p, d)
    hs = x_sample.reshape(n_d * t_d, d)
    outs = [[] for _ in range(8)]
    for li in range(depth):
        w = _layer_weights(li, stacked, norms, qk_gain, cmp_pos, out_gain, lb_all)
        hp, kv, win, s_r, s_h = _prompt_layer(hp, p_prompt[li].reshape(n_p * t_p, -1), w, n=n_p, t=t_p)
        for slot, val in zip((0, 2, 4, 6), (kv, win, s_r, s_h)):
            outs[slot].append(val)
        hs, kv, win_new, s_r, s_h = _decode_layer(hs, p_sample[li].reshape(n_d * t_d, -1), w, cache_rows, cache_half,
                                                  win_rows_view, state_ret, state_hgrn, page_table, n=n_d, steps=t_d)
        win_all = jnp.concatenate([cache_win[li], win_new], axis=1)
        win = win_all[:, wb + t_d - min(WINDOW, wb + t_d):]
        for slot, val in zip((1, 3, 5, 7), (kv, win, s_r, s_h)):
            outs[slot].append(val)
    return (hp.reshape(n_p, t_p, d), hs.reshape(n_d, t_d, d)) + tuple(jnp.stack(o) for o in outs)
```

```python
import functools
import math

import jax
import jax.numpy as jnp
import numpy as np
from jax import lax
from jax.experimental import pallas as pl
from jax.experimental.pallas import tpu as pltpu

F32 = jnp.float32
BF16 = jnp.bfloat16

D_MODEL = 2048
HEAD_DIM = 128
NSA_HEADS = 8
NSA_KV_HEADS = 2
HEADS_PER_GROUP = NSA_HEADS // NSA_KV_HEADS
CMP_BLOCK = 32
CMP_STRIDE = 16
SEL_BLOCK = 64
N_SEL = 16
WINDOW = 512
RET_HEADS = 4
HGRN_HEADS = 4
D_FF = 5632
PLE_DIM = 256
NSA_WIDTH = NSA_HEADS * HEAD_DIM
RET_WIDTH = RET_HEADS * HEAD_DIM
HGRN_WIDTH = HGRN_HEADS * HEAD_DIM
KV_WIDTH = 6 * NSA_KV_HEADS * HEAD_DIM
GATE_WIDTH = 3 * NSA_HEADS
NEG_INF = -1e30
FORCE_SCORE = 1e6
EPS = 1e-6
ATT_SCALE = HEAD_DIM ** -0.5

LANES = 128
PLAIN_RET_BLK = 0
PLAIN_HGRN_BLK = (4 * RET_WIDTH) // LANES
PLAIN_MERGE_OFF = 4 * RET_WIDTH + 4 * HGRN_WIDTH
PLAIN_GATE_BLK = (PLAIN_MERGE_OFF + 3 * D_MODEL) // LANES
PLAIN_WIDTH = PLAIN_MERGE_OFF + 3 * D_MODEL + LANES

VMEM_LIMIT = 56 * 1024 * 1024


def _cp(sem, vmem=VMEM_LIMIT):
    return pltpu.CompilerParams(dimension_semantics=sem, vmem_limit_bytes=vmem)


def _rms(x):
    return x * lax.rsqrt(jnp.mean(x * x, axis=-1, keepdims=True) + EPS)


def _dot(a, b):
    return jnp.dot(a.astype(BF16), b.astype(BF16), preferred_element_type=F32)


def _dot_nt(a, b):
    return lax.dot_general(a.astype(BF16), b.astype(BF16), (((1,), (1,)), ((), ())),
                           preferred_element_type=F32)


def _dot_tn(a, b):
    rows = a.shape[0]
    if rows % LANES:
        pad = LANES - rows % LANES
        a = jnp.concatenate([a, jnp.zeros((pad, a.shape[1]), a.dtype)], axis=0)
        b = jnp.concatenate([b, jnp.zeros((pad, b.shape[1]), b.dtype)], axis=0)
    return _dot(a.T, b)


def _split_bf16(x):
    hi = x.astype(BF16)
    lo = (x - hi.astype(F32)).astype(BF16)
    return hi, lo


def _sigmoid(x):
    return 1.0 / (1.0 + jnp.exp(-x))


def _ffn_body(x_ref, g1_ref, wg_ref, wv_ref, wd_ref, g2_ref, h_ref, u_ref, xn_sc, acc_sc, *, nf):
    j = pl.program_id(1)

    @pl.when(j == 0)
    def _():
        xn_sc[...] = (_rms(x_ref[...]) * g1_ref[...]).astype(BF16)
        acc_sc[...] = jnp.zeros_like(acc_sc)

    xn = xn_sc[...]
    g = jnp.dot(xn, wg_ref[...], preferred_element_type=F32)
    v = jnp.dot(xn, wv_ref[...], preferred_element_type=F32)
    a = (g * _sigmoid(g) * v).astype(BF16)
    acc_sc[...] += jnp.dot(a, wd_ref[...], preferred_element_type=F32)

    @pl.when(j == nf - 1)
    def _():
        h = x_ref[...] + 0.5 * acc_sc[...]
        h_ref[...] = h
        u_ref[...] = (_rms(h) * g2_ref[...]).astype(BF16)


def _ffn(x, g1, w_gu, w_down, g2, li, which, *, tm, tf):
    m, d = x.shape
    f = w_down.shape[2]
    nf = f // tf
    return pl.pallas_call(
        functools.partial(_ffn_body, nf=nf),
        grid=(m // tm, nf),
        in_specs=[
            pl.BlockSpec((tm, d), lambda i, j: (i, 0)),
            pl.BlockSpec((1, d), lambda i, j: (0, 0)),
            pl.BlockSpec((None, None, d, tf), lambda i, j: (li, which, 0, j)),
            pl.BlockSpec((None, None, d, tf), lambda i, j: (li, which, 0, j + nf)),
            pl.BlockSpec((None, None, tf, d), lambda i, j: (li, which, j, 0)),
            pl.BlockSpec((1, d), lambda i, j: (0, 0)),
        ],
        out_specs=[
            pl.BlockSpec((tm, d), lambda i, j: (i, 0)),
            pl.BlockSpec((tm, d), lambda i, j: (i, 0)),
        ],
        out_shape=[jax.ShapeDtypeStruct((m, d), F32), jax.ShapeDtypeStruct((m, d), BF16)],
        scratch_shapes=[pltpu.VMEM((tm, d), BF16), pltpu.VMEM((tm, d), F32)],
        compiler_params=_cp(("parallel", "arbitrary")),
        name="ffn",
    )(x, g1, w_gu, w_gu, w_down, g2)


def _proj_plain_body(u_ref, w_ref, o_ref):
    o_ref[...] = jnp.dot(u_ref[...], w_ref[...], preferred_element_type=F32).astype(o_ref.dtype)


def _proj_norm_body(u_ref, w_ref, gain_ref, flag_ref, *refs, tn, rows_out):
    o_ref = refs[-2] if rows_out else refs[-1]
    z = jnp.dot(u_ref[...], w_ref[...], preferred_element_type=F32)
    chunks = tn // LANES
    for c in range(chunks):
        sl = slice(c * LANES, (c + 1) * LANES)
        zc = z[:, sl]
        normed = _rms(zc) * gain_ref[:, sl]
        val = jnp.where(flag_ref[:, sl] > 0.5, normed, zc)
        o_ref[:, sl] = val.astype(o_ref.dtype)
        if rows_out:
            refs[-1][pl.ds(c, z.shape[0], stride=chunks), :] = val


def _proj(u, w, li, *, col0, n, tm, tn, out_dtype, gain=None, flag=None, name, rows_out=None):
    m, k = u.shape
    cb = col0 // tn
    in_specs = [pl.BlockSpec((tm, k), lambda i, j: (i, 0)), pl.BlockSpec((None, k, tn), lambda i, j: (li, 0, cb + j))]
    args = [u, w]
    out_specs = pl.BlockSpec((tm, tn), lambda i, j: (i, j))
    out_shape = jax.ShapeDtypeStruct((m, n), out_dtype)
    aliases = {}
    if gain is None:
        body = _proj_plain_body
    else:
        body = functools.partial(_proj_norm_body, tn=tn, rows_out=rows_out is not None)
        in_specs += [pl.BlockSpec((1, tn), lambda i, j: (0, j)), pl.BlockSpec((1, tn), lambda i, j: (0, j))]
        args += [gain, flag]
    if rows_out is not None:
        assert n == tn, "the row-major store needs the whole width in one column tile"
        depth, prev = rows_out
        chunks = n // LANES
        out_specs = [out_specs, pl.BlockSpec((None, tm * chunks, LANES), lambda i, j: (li, i, 0))]
        out_shape = [out_shape, jax.ShapeDtypeStruct((depth, m * chunks, LANES), F32)]
        if prev is not None:
            in_specs.append(pl.BlockSpec(memory_space=pl.ANY))
            args.append(prev)
            aliases = {len(args) - 1: 1}
    return pl.pallas_call(
        body,
        grid=(m // tm, n // tn),
        in_specs=in_specs,
        out_specs=out_specs,
        out_shape=out_shape,
        input_output_aliases=aliases,
        compiler_params=_cp(("parallel", "arbitrary")),
        name=name,
    )(*args)


def _merge_body(oa_ref, ob_ref, oc_ref, m0_ref, m1_ref, m2_ref, h_ref, wb_ref, wo_ref, out_ref):
    ya = jnp.dot(oa_ref[...], wb_ref[0:NSA_WIDTH, :], preferred_element_type=F32)
    yb = jnp.dot(ob_ref[...], wb_ref[NSA_WIDTH:NSA_WIDTH + RET_WIDTH, :], preferred_element_type=F32)
    yc = jnp.dot(oc_ref[...], wb_ref[NSA_WIDTH + RET_WIDTH:, :], preferred_element_type=F32)
    mixed = _sigmoid(m0_ref[...]) * ya + _sigmoid(m1_ref[...]) * yb + _sigmoid(m2_ref[...]) * yc
    out_ref[...] = h_ref[...] + jnp.dot(mixed.astype(BF16), wo_ref[...], preferred_element_type=F32)


def _merge(o_nsa, o_ret, o_hgrn, zplain, h, w_branch, w_o, li, *, tm):
    m, d = h.shape
    mb = PLAIN_MERGE_OFF // d
    row = lambda i: (i, 0)
    layer = lambda i: (li, 0, 0)
    return pl.pallas_call(
        _merge_body,
        grid=(m // tm,),
        in_specs=[
            pl.BlockSpec((tm, NSA_WIDTH), row),
            pl.BlockSpec((tm, RET_WIDTH), row),
            pl.BlockSpec((tm, HGRN_WIDTH), row),
            pl.BlockSpec((tm, d), lambda i: (i, mb)),
            pl.BlockSpec((tm, d), lambda i: (i, mb + 1)),
            pl.BlockSpec((tm, d), lambda i: (i, mb + 2)),
            pl.BlockSpec((tm, d), row),
            pl.BlockSpec((None,) + w_branch.shape[1:], layer),
            pl.BlockSpec((None,) + w_o.shape[1:], layer),
        ],
        out_specs=pl.BlockSpec((tm, d), row),
        out_shape=jax.ShapeDtypeStruct((m, d), F32),
        compiler_params=_cp(("parallel",)),
        name="merge",
    )(o_nsa, o_ret, o_hgrn, zplain, zplain, zplain, h, w_branch, w_o)


def _ple_body(h_ref, u_ref, p_ref, wg_ref, wp_ref, out_ref):
    gate = _sigmoid(jnp.dot(u_ref[...], wg_ref[...], preferred_element_type=F32))
    proj = jnp.dot(p_ref[...].astype(BF16), wp_ref[...], preferred_element_type=F32)
    out_ref[...] = h_ref[...] + gate * proj


def _ple(h, u, p, w_gate, w_proj, li, *, tm):
    m, d = h.shape
    row = lambda i: (i, 0)
    layer = lambda i: (li, 0, 0)
    return pl.pallas_call(
        _ple_body,
        grid=(m // tm,),
        in_specs=[
            pl.BlockSpec((tm, d), row),
            pl.BlockSpec((tm, d), row),
            pl.BlockSpec((tm, p.shape[1]), row),
            pl.BlockSpec((None,) + w_gate.shape[1:], layer),
            pl.BlockSpec((None,) + w_proj.shape[1:], layer),
        ],
        out_specs=pl.BlockSpec((tm, d), row),
        out_shape=jax.ShapeDtypeStruct((m, d), F32),
        compiler_params=_cp(("parallel",)),
        name="ple",
    )(h, u, p, w_gate, w_proj)


def _retention_tables(c):
    lg = np.log1p(-np.exp2(-5.0 - np.arange(RET_HEADS, dtype=np.float64)))
    i = np.arange(c, dtype=np.float64)
    diff = i[:, None] - i[None, :]
    dmat = np.where(diff >= 0, np.exp(lg[:, None, None] * np.maximum(diff, 0.0)), 0.0)
    q_dec = np.exp(lg[:, None] * (i + 1.0))[..., None] * np.ones((1, 1, HEAD_DIM))
    k_dec = np.exp(lg[:, None] * (c - 1.0 - i))[..., None] * np.ones((1, 1, HEAD_DIM))
    c_dec = np.exp(lg * c)[:, None, None] * np.ones((1, 8, HEAD_DIM))
    return tuple(jnp.asarray(a, F32) for a in (dmat, q_dec, k_dec, c_dec))


def _retention_body(*refs, has_state, nch, c):
    if has_state:
        q_ref, k_ref, v_ref, g_ref, dm_ref, qd_ref, kd_ref, cd_ref, gain_ref, s0_ref, o_ref, s_ref = refs
    else:
        q_ref, k_ref, v_ref, g_ref, dm_ref, qd_ref, kd_ref, cd_ref, gain_ref, o_ref, s_ref = refs

    def chunk(ci, s):
        rows = pl.ds(pl.multiple_of(ci * c, c), c)
        q = q_ref[rows, :]
        k = k_ref[rows, :] * ATT_SCALE
        v = v_ref[rows, :]
        a = _dot_nt(q, k) * dm_ref[...]
        o = _dot(a, v) + _dot(q * qd_ref[...], s)
        g = g_ref[rows, :]
        o_ref[rows, :] = (_rms(o) * gain_ref[...] * (g * _sigmoid(g))).astype(o_ref.dtype)
        return s * cd_ref[0:1, :] + _dot_tn(k * kd_ref[...], v)

    s_init = s0_ref[...] if has_state else jnp.zeros((LANES, LANES), F32)
    s_ref[...] = lax.fori_loop(0, nch, chunk, s_init)


def _retention(zplain, gain, s0, li, *, n, t, c):
    nch = t // c
    dmat, q_dec, k_dec, c_dec = _retention_tables(c)
    hb = RET_HEADS
    seq = lambda w: (lambda i, h: (i, PLAIN_RET_BLK + w * hb + h))
    tab = lambda i, h: (h, 0, 0)
    in_specs = [pl.BlockSpec((t, LANES), seq(w)) for w in range(4)]
    in_specs += [
        pl.BlockSpec((None, c, c), tab),
        pl.BlockSpec((None, c, LANES), tab),
        pl.BlockSpec((None, c, LANES), tab),
        pl.BlockSpec((None, 8, LANES), tab),
        pl.BlockSpec((1, LANES), lambda i, h: (0, h)),
    ]
    args = [zplain] * 4 + [dmat, q_dec, k_dec, c_dec, gain]
    if s0 is not None:
        in_specs.append(pl.BlockSpec((None, None, None, LANES, LANES), lambda i, h: (li, i, h, 0, 0)))
        args.append(s0)
    return pl.pallas_call(
        functools.partial(_retention_body, has_state=s0 is not None, nch=nch, c=c),
        grid=(n, hb),
        in_specs=in_specs,
        out_specs=[
            pl.BlockSpec((t, LANES), lambda i, h: (i, h)),
            pl.BlockSpec((None, None, LANES, LANES), lambda i, h: (i, h, 0, 0)),
        ],
        out_shape=[
            jax.ShapeDtypeStruct((n * t, RET_WIDTH), BF16),
            jax.ShapeDtypeStruct((n, hb, LANES, LANES), F32),
        ],
        compiler_params=_cp(("parallel", "parallel")),
        name="retention",
    )(*args)


HGRN_SUB = 16


def _hgrn_body(*refs, has_state, nch, c):
    if has_state:
        q_ref, f_ref, v_ref, g_ref, lb_ref, gain_ref, s0_ref, o_ref, s_ref, st_sc, b_sc, k_sc, v_sc, o_sc = refs
    else:
        q_ref, f_ref, v_ref, g_ref, lb_ref, gain_ref, o_ref, s_ref, st_sc, b_sc, k_sc, v_sc, o_sc = refs
    ci = pl.program_id(2)

    @pl.when(ci == 0)
    def _():
        st_sc[...] = s0_ref[...].T if has_state else jnp.zeros_like(st_sc)

    lb = lb_ref[...]
    f = lb + (1.0 - lb) * _sigmoid(f_ref[...])
    logf = jnp.log(f)
    ri = lax.broadcasted_iota(jnp.int32, (c, c), 0)
    si = lax.broadcasted_iota(jnp.int32, (c, c), 1)
    tri = jnp.where(ri >= si, 1.0, 0.0).astype(BF16)
    hi = logf.astype(BF16)
    r1 = logf - hi.astype(F32)
    mid = r1.astype(BF16)
    lo = (r1 - mid.astype(F32)).astype(BF16)
    b = (jnp.dot(tri, hi, preferred_element_type=F32) + jnp.dot(tri, mid, preferred_element_type=F32)
         + jnp.dot(tri, lo, preferred_element_type=F32))
    q = q_ref[...]
    kk = 1.0 - f
    v = v_ref[...]
    b_sc[...] = b
    k_sc[...] = kk
    v_sc[...] = v
    st = st_sc[...]
    o_sc[...] = _dot_nt(q * jnp.exp(b), st)

    sub = min(HGRN_SUB, c)
    row_id = lax.broadcasted_iota(jnp.int32, (sub, LANES), 0)
    for blk in range(c // sub):
        r0 = blk * sub
        b_i = b[r0:r0 + sub]
        q_i = q[r0:r0 + sub]
        acc = jnp.zeros((sub, LANES), F32)
        if blk > 0:
            ref = b_sc[r0 - 1:r0, :]
            qt = q_i * jnp.exp(b_i - ref)
            kt = kk[0:r0] * jnp.exp(ref - b[0:r0])
            acc = _dot(_dot_nt(qt, kt), v[0:r0])
        for s in range(sub):
            b_s = b_sc[r0 + s:r0 + s + 1, :]
            k_s = k_sc[r0 + s:r0 + s + 1, :]
            v_s = v_sc[r0 + s:r0 + s + 1, :]
            w = q_i * jnp.exp(jnp.minimum(b_i - b_s, 0.0)) * k_s
            w = jnp.where(row_id >= s, w, 0.0)
            acc = acc + jnp.sum(w, axis=-1, keepdims=True) * v_s
        o_sc[r0:r0 + sub, :] += acc

    b_last = b_sc[c - 1:c, :]
    st_sc[...] = st * jnp.exp(b_last) + _dot_tn(v, kk * jnp.exp(b_last - b))
    g = g_ref[...]
    o_ref[...] = (_rms(o_sc[...]) * gain_ref[...] * (g * _sigmoid(g))).astype(o_ref.dtype)

    @pl.when(ci == nch - 1)
    def _():
        s_ref[...] = st_sc[...].T


def _hgrn(zplain, lb, gain, s0, li, *, n, t, c):
    nch = t // c
    hb = HGRN_HEADS
    row = lambda w: (lambda i, h, j: (i * nch + j, PLAIN_HGRN_BLK + w * hb + h))
    head = lambda i, h, j: (0, h)
    in_specs = [pl.BlockSpec((c, LANES), row(w)) for w in range(4)]
    in_specs += [pl.BlockSpec((1, LANES), head), pl.BlockSpec((1, LANES), head)]
    args = [zplain] * 4 + [lb, gain]
    if s0 is not None:
        in_specs.append(pl.BlockSpec((None, None, None, LANES, LANES), lambda i, h, j: (li, i, h, 0, 0)))
        args.append(s0)
    return pl.pallas_call(
        functools.partial(_hgrn_body, has_state=s0 is not None, nch=nch, c=c),
        grid=(n, hb, nch),
        in_specs=in_specs,
        out_specs=[
            pl.BlockSpec((c, LANES), lambda i, h, j: (i * nch + j, h)),
            pl.BlockSpec((None, None, LANES, LANES), lambda i, h, j: (i, h, 0, 0)),
        ],
        out_shape=[
            jax.ShapeDtypeStruct((n * t, HGRN_WIDTH), BF16),
            jax.ShapeDtypeStruct((n, hb, LANES, LANES), F32),
        ],
        scratch_shapes=[pltpu.VMEM((LANES, LANES), F32)] + [pltpu.VMEM((c, LANES), F32)] * 4,
        compiler_params=_cp(("parallel", "parallel", "arbitrary")),
        name="hgrn",
    )(*args)


def _lb_body(x_ref, o_ref):
    x = x_ref[...]
    e = jnp.exp(x - jnp.max(x, axis=0, keepdims=True))
    sm = e / jnp.sum(e, axis=0, keepdims=True)
    acc = jnp.zeros_like(sm[0:1])
    o_ref[0:1, :] = acc
    for layer in range(1, x.shape[0]):
        acc = acc + sm[layer:layer + 1]
        o_ref[layer:layer + 1, :] = acc


def _hgrn_lower_bounds(hgrn_lb):
    return pl.pallas_call(
        _lb_body,
        out_shape=jax.ShapeDtypeStruct(hgrn_lb.shape, F32),
        name="hgrn_lb",
    )(hgrn_lb.astype(F32))


def _group_sums(x, p0, p1):
    rows, width = x.shape
    xg = x.reshape(rows // CMP_STRIDE, CMP_STRIDE, width)
    return jnp.sum(xg * p0[None], axis=1), jnp.sum(xg * p1[None], axis=1)


def _cmp_partial_body(x_ref, p0_ref, p1_ref, a0_ref, a1_ref):
    a0, a1 = _group_sums(x_ref[...], p0_ref[...], p1_ref[...])
    a0_ref[...] = a0
    a1_ref[...] = a1


def _cmp_partial(kvrows, p0, p1, *, rb):
    m = kvrows.shape[0]
    w = p0.shape[1]
    const = lambda i: (0, 0)
    out = jax.ShapeDtypeStruct((m // CMP_STRIDE, w), F32)
    return pl.pallas_call(
        _cmp_partial_body,
        grid=(m // rb,),
        in_specs=[pl.BlockSpec((rb, w), lambda i: (i, 0)), pl.BlockSpec(p0.shape, const), pl.BlockSpec(p1.shape, const)],
        out_specs=[pl.BlockSpec((rb // CMP_STRIDE, w), lambda i: (i, 0))] * 2,
        out_shape=[out, out],
        compiler_params=_cp(("parallel",)),
        name="cmp_partial",
    )(kvrows, p0, p1)


CACHE_SUB = 4 * NSA_KV_HEADS


def _page_group_sums(x_refs, p0_ref, p1_ref, a0_ref, a1_ref, page):
    gp = page // CMP_STRIDE
    for k, x_ref in enumerate(x_refs):
        for s in range(2 * NSA_KV_HEADS):
            sl = slice(s * HEAD_DIM, (s + 1) * HEAD_DIM)
            x = x_ref[pl.ds(s, page, stride=CACHE_SUB), :]
            a0, a1 = _group_sums(x, p0_ref[:, sl], p1_ref[:, sl])
            a0_ref[k * gp:(k + 1) * gp, sl] = a0
            a1_ref[k * gp:(k + 1) * gp, sl] = a1


def _cmp_partial_paged_body(pt_ref, *refs, pages, page):
    p0_ref, p1_ref, a0_ref, a1_ref = refs[pages:]
    _page_group_sums(refs[:pages], p0_ref, p1_ref, a0_ref, a1_ref, page)


def _proj_cache_sums_body(pt_ref, u_ref, w_ref, *refs, pages, page):
    p0_ref, p1_ref, o_ref, a0_ref, a1_ref = refs[pages:]
    o_ref[...] = jnp.dot(u_ref[...], w_ref[...], preferred_element_type=F32)
    _page_group_sums(refs[:pages], p0_ref, p1_ref, a0_ref, a1_ref, page)


def _proj_with_cache_sums(u, w, li, cache_rows, page_table, p0, p1, *, n, tm, tn, pages):
    m, k = u.shape
    nd, n_pages = page_table.shape
    page = cache_rows.shape[2] // CACHE_SUB
    width = p0.shape[1]
    gp = page // CMP_STRIDE
    chunks = n_pages // pages
    units = nd * chunks
    ni, nj = m // tm, n // tn
    unit = lambda i, j: jnp.minimum(i * nj + j, units - 1)
    const = lambda i, j, pt: (0, 0)
    page_spec = lambda kk: pl.BlockSpec(
        (None, None, page * CACHE_SUB, HEAD_DIM),
        lambda i, j, pt: (li, pt[unit(i, j) // chunks, (unit(i, j) % chunks) * pages + kk], 0, 0))
    sums_spec = pl.BlockSpec((None, pages * gp, width), lambda i, j, pt: (unit(i, j) // chunks, unit(i, j) % chunks, 0))
    sums = jax.ShapeDtypeStruct((nd, n_pages * gp, width), F32)
    return pl.pallas_call(
        functools.partial(_proj_cache_sums_body, pages=pages, page=page),
        grid_spec=pltpu.PrefetchScalarGridSpec(
            num_scalar_prefetch=1,
            grid=(ni, nj),
            in_specs=[pl.BlockSpec((tm, k), lambda i, j, pt: (i, 0)),
                      pl.BlockSpec((None, k, tn), lambda i, j, pt: (li, 0, j))]
            + [page_spec(kk) for kk in range(pages)]
            + [pl.BlockSpec(p0.shape, const), pl.BlockSpec(p1.shape, const)],
            out_specs=[pl.BlockSpec((tm, tn), lambda i, j, pt: (i, j)), sums_spec, sums_spec],
        ),
        out_shape=[jax.ShapeDtypeStruct((m, n), F32), sums, sums],
        compiler_params=_cp(("arbitrary", "arbitrary")),
        name="proj_plain_cache_sums",
    )(page_table, u, w, *([cache_rows] * pages), p0, p1)


def _cache_sum_pages(m, tm, n, tn, page_table):
    nd, n_pages = page_table.shape
    steps = (m // tm) * (n // tn)
    for pages in (4, 8, 16, 32):
        if n_pages % pages == 0 and nd * (n_pages // pages) <= steps:
            return pages
    return None


def _cmp_partial_paged(cache_rows, page_table, p0, p1, li, *, pages):
    n, n_pages = page_table.shape
    page = cache_rows.shape[2] // CACHE_SUB
    w = p0.shape[1]
    gp = page // CMP_STRIDE
    const = lambda i, c, pt: (0, 0)
    page_spec = lambda k: pl.BlockSpec((None, None, page * CACHE_SUB, HEAD_DIM),
                                       lambda i, c, pt: (li, pt[i, c * pages + k], 0, 0))
    out = jax.ShapeDtypeStruct((n, n_pages * gp, w), F32)
    return pl.pallas_call(
        functools.partial(_cmp_partial_paged_body, pages=pages, page=page),
        grid_spec=pltpu.PrefetchScalarGridSpec(
            num_scalar_prefetch=1,
            grid=(n, n_pages // pages),
            in_specs=[page_spec(k) for k in range(pages)] + [pl.BlockSpec(p0.shape, const), pl.BlockSpec(p1.shape, const)],
            out_specs=[pl.BlockSpec((None, pages * gp, w), lambda i, c, pt: (i, c, 0))] * 2,
        ),
        out_shape=[out, out],
        compiler_params=_cp(("parallel", "arbitrary")),
        name="cmp_partial_paged",
    )(page_table, *([cache_rows] * pages), p0, p1)


def _cmp_final_body(a0_ref, a1_ref, wk_ref, wv_ref, gain_ref, kc_ref, vc_ref):
    ng = a0_ref.shape[0]
    agg = a0_ref[...] + pltpu.roll(a1_ref[...], ng - 1, 0)
    for g in range(NSA_KV_HEADS):
        sl = slice(g * HEAD_DIM, (g + 1) * HEAD_DIM)
        ak = agg[:, sl]
        av = agg[:, NSA_KV_HEADS * HEAD_DIM + g * HEAD_DIM:NSA_KV_HEADS * HEAD_DIM + (g + 1) * HEAD_DIM]
        kc_ref[:, sl] = (_rms(_dot(ak, wk_ref[...])) * gain_ref[...]).astype(kc_ref.dtype)
        vc_ref[:, sl] = _dot(av, wv_ref[...]).astype(vc_ref.dtype)


def _cmp_final(a0, a1, w_k, w_v, gain):
    n, ng, w = a0.shape
    const = lambda i: (0, 0)
    out = jax.ShapeDtypeStruct((n, ng, NSA_KV_HEADS * HEAD_DIM), BF16)
    return pl.pallas_call(
        _cmp_final_body,
        grid=(n,),
        in_specs=[
            pl.BlockSpec((None, ng, w), lambda i: (i, 0, 0)),
            pl.BlockSpec((None, ng, w), lambda i: (i, 0, 0)),
            pl.BlockSpec(w_k.shape, const),
            pl.BlockSpec(w_v.shape, const),
            pl.BlockSpec(gain.shape, const),
        ],
        out_specs=[pl.BlockSpec((None, ng, NSA_KV_HEADS * HEAD_DIM), lambda i: (i, 0, 0))] * 2,
        out_shape=[out, out],
        compiler_params=_cp(("parallel",)),
        name="cmp_final",
    )(a0, a1, w_k, w_v, gain)


def _overlap_matrix(n_cmp_pad, n_slc, width):
    ci = np.arange(n_cmp_pad)[:, None]
    sj = np.arange(width)[None, :]
    c_start = ci * CMP_STRIDE
    s_start = sj * SEL_BLOCK
    hit = ((c_start < s_start + SEL_BLOCK) & (c_start + CMP_BLOCK > s_start)
           & (ci < n_cmp_pad - 1) & (sj < n_slc))
    return jnp.asarray(hit, BF16)


def _block_scores(imp, qpos_i, ov, n_slc):
    hi, lo = _split_bf16(imp)
    score = jnp.dot(hi, ov, preferred_element_type=F32) + jnp.dot(lo, ov, preferred_element_type=F32)
    sj = lax.broadcasted_iota(jnp.int32, score.shape, 1)
    cur = qpos_i // SEL_BLOCK
    forced = (sj == 0) | (sj == cur) | (sj == cur - 1)
    score = jnp.where(forced, FORCE_SCORE, score)
    score = jnp.where(sj * SEL_BLOCK <= qpos_i, score, -1.0)
    return jnp.where(sj < n_slc, score, -2.0)


def _masked_softmax(s, allow):
    s = jnp.where(allow, s, NEG_INF)
    m = jnp.max(s, axis=-1, keepdims=True)
    e = jnp.where(allow, jnp.exp(s - m), 0.0)
    return e / jnp.maximum(jnp.sum(e, axis=-1, keepdims=True), 1e-30)


SEL_KEY_BLOCK = 256


def _nsa_prompt_body(q_ref, zg_ref, kc_ref, vc_ref, ov_ref, ks_ref, vs_ref, kw_ref, vw_ref, o_ref, *, t_len, tq):
    g = pl.program_id(1)
    t0 = pl.program_id(2) * tq
    n_cmp_pad = kc_ref.shape[0]
    n_slc = -(-t_len // SEL_BLOCK)
    hpg = HEADS_PER_GROUP
    qpos_i = t0 + lax.broadcasted_iota(jnp.int32, (tq, 1), 0)
    slopes = [jnp.where(g == 0, 2.0 ** -(r + 1), 2.0 ** -(r + 1 + hpg)) for r in range(hpg)]
    qs = [q_ref[:, r * HEAD_DIM:(r + 1) * HEAD_DIM] for r in range(hpg)]

    ci = lax.broadcasted_iota(jnp.int32, (tq, n_cmp_pad), 1)
    cdist_i = qpos_i - (ci * CMP_STRIDE + CMP_BLOCK - 1)
    callow = (cdist_i >= 0) & (ci < n_cmp_pad - 1)
    cdist = cdist_i.astype(F32)
    kc = kc_ref[...]
    vc = vc_ref[...]
    imp = jnp.zeros((tq, n_cmp_pad), F32)
    o_cmp = []
    for r in range(hpg):
        p = _masked_softmax(_dot_nt(qs[r], kc) * ATT_SCALE - slopes[r] * cdist, callow)
        o_cmp.append(_dot(p, vc))
        imp = imp + p

    score = _block_scores(imp, qpos_i, ov_ref[...], n_slc)
    s_t = score.T[0:n_slc]
    jrow = lax.broadcasted_iota(jnp.int32, (n_slc, tq), 0)
    rank = jnp.zeros((n_slc, tq), jnp.int32)
    for jp in range(n_slc):
        row = s_t[jp:jp + 1, :]
        beats = (row > s_t) | ((row == s_t) & (jp < jrow))
        rank = rank + jnp.where(beats, 1, 0)
    sel_t = jnp.where((rank < N_SEL) & (s_t >= 0.0), 1.0, 0.0)
    sel_t = jnp.concatenate([sel_t, jnp.zeros((LANES - n_slc, tq), F32)], axis=0)
    sel = sel_t.T.astype(BF16)

    kb = SEL_KEY_BLOCK
    nkb = (t0 + tq + kb - 1) // kb

    def sel_step(i, carry):
        k0 = pl.multiple_of(i * kb, kb)
        kblk = ks_ref[pl.ds(k0, kb), :].astype(BF16)
        vblk = vs_ref[pl.ds(k0, kb), :].astype(BF16)
        kpos_i = k0 + lax.broadcasted_iota(jnp.int32, (1, kb), 1)
        ej = lax.broadcasted_iota(jnp.int32, (LANES, kb), 0)
        ec = lax.broadcasted_iota(jnp.int32, (LANES, kb), 1)
        expand = jnp.where(ej == (k0 + ec) // SEL_BLOCK, 1.0, 0.0).astype(BF16)
        allow = (jnp.dot(sel, expand, preferred_element_type=F32) > 0.5) & (kpos_i <= qpos_i)
        kpos = kpos_i.astype(F32)
        out = []
        for r in range(hpg):
            m_old, l_old, acc_old = carry[3 * r:3 * r + 3]
            s = jnp.where(allow, _dot_nt(qs[r], kblk) * ATT_SCALE + slopes[r] * kpos, NEG_INF)
            m_new = jnp.maximum(m_old, jnp.max(s, axis=-1, keepdims=True))
            alpha = jnp.exp(m_old - m_new)
            e = jnp.exp(s - m_new)
            out += [m_new, alpha * l_old + jnp.sum(e, axis=-1, keepdims=True),
                    alpha * acc_old + _dot(e, vblk)]
        return tuple(out)

    init = (jnp.full((tq, 1), NEG_INF, F32), jnp.zeros((tq, 1), F32), jnp.zeros((tq, HEAD_DIM), F32)) * hpg
    fin = lax.fori_loop(0, nkb, sel_step, init)
    o_slc = [fin[3 * r + 2] / fin[3 * r + 1] for r in range(hpg)]

    wk = WINDOW + tq
    ws = pl.multiple_of(jnp.clip(t0 - WINDOW, 0, t_len - wk), LANES)
    kw = kw_ref[pl.ds(ws, wk), :].astype(BF16)
    vw = vw_ref[pl.ds(ws, wk), :].astype(BF16)
    wpos_i = ws + lax.broadcasted_iota(jnp.int32, (1, wk), 1)
    wdist_i = qpos_i - wpos_i
    wallow = (wdist_i >= 0) & (wdist_i <= WINDOW)
    wpos = wpos_i.astype(F32)
    gates = _sigmoid(zg_ref[...])
    for r in range(hpg):
        s = jnp.where(wallow, _dot_nt(qs[r], kw) * ATT_SCALE + slopes[r] * wpos, NEG_INF)
        e = jnp.exp(s - jnp.max(s, axis=-1, keepdims=True))
        o_win = _dot(e, vw) / jnp.sum(e, axis=-1, keepdims=True)
        gate = [jnp.where(g == 0, gates[:, 3 * r + b:3 * r + b + 1],
                          gates[:, 3 * (r + hpg) + b:3 * (r + hpg) + b + 1]) for b in range(3)]
        o = gate[0] * o_cmp[r] + gate[1] * o_slc[r] + gate[2] * o_win
        o_ref[:, r * HEAD_DIM:(r + 1) * HEAD_DIM] = o.astype(o_ref.dtype)


def _nsa_prompt(q, zplain, kc, vc, kvrows, winrows, *, n, t, tq):
    nt = t // tq
    gw = HEADS_PER_GROUP * HEAD_DIM
    n_cmp_pad = kc.shape[1]
    g_blocks = NSA_KV_HEADS
    qrow = lambda i, g, j: (i * nt + j, g)
    seq = lambda blk: (lambda i, g, j: (i, blk + g))
    ov = _overlap_matrix(n_cmp_pad, -(-t // SEL_BLOCK), LANES)
    return pl.pallas_call(
        functools.partial(_nsa_prompt_body, t_len=t, tq=tq),
        grid=(n, NSA_KV_HEADS, nt),
        in_specs=[
            pl.BlockSpec((tq, gw), qrow),
            pl.BlockSpec((tq, LANES), lambda i, g, j: (i * nt + j, PLAIN_GATE_BLK)),
            pl.BlockSpec((None, n_cmp_pad, HEAD_DIM), lambda i, g, j: (i, 0, g)),
            pl.BlockSpec((None, n_cmp_pad, HEAD_DIM), lambda i, g, j: (i, 0, g)),
            pl.BlockSpec(ov.shape, lambda i, g, j: (0, 0)),
            pl.BlockSpec((t, HEAD_DIM), seq(2 * g_blocks)),
            pl.BlockSpec((t, HEAD_DIM), seq(3 * g_blocks)),
            pl.BlockSpec((t, HEAD_DIM), seq(0)),
            pl.BlockSpec((t, HEAD_DIM), seq(g_blocks)),
        ],
        out_specs=pl.BlockSpec((tq, gw), qrow),
        out_shape=jax.ShapeDtypeStruct((n * t, NSA_WIDTH), BF16),
        compiler_params=_cp(("parallel", "parallel", "arbitrary")),
        name="nsa_prompt",
    )(q, zplain, kc, vc, ov, kvrows, kvrows, winrows, winrows)


KV_OFF = NSA_WIDTH
GATE_OFF = KV_OFF + KV_WIDTH
PLAIN_OFF = GATE_OFF + GATE_WIDTH
SLOT_WIDTH = NSA_KV_HEADS * HEAD_DIM


def _stacked_weights(w_in, cmp_w, w_branch, w_o, w_gu, w_down, w_ple_proj, w_ple_gate):
    w_gate = jnp.pad(w_in[:, :, GATE_OFF:PLAIN_OFF], ((0, 0), (0, 0), (0, LANES - GATE_WIDTH)))
    return dict(
        w_head=w_in[:, :, :GATE_OFF].astype(BF16),
        w_plain=jnp.concatenate([w_in[:, :, PLAIN_OFF:], w_gate], axis=2).astype(BF16),
        w_phi=cmp_w.astype(BF16),
        w_branch=w_branch.astype(BF16),
        w_o=w_o.astype(BF16),
        w_gu=w_gu.astype(BF16),
        w_down=w_down.astype(BF16),
        w_ple_proj=w_ple_proj.astype(BF16),
        w_ple_gate=w_ple_gate.astype(BF16),
    )


def _layer_weights(li, stacked, norms, qk_gain, cmp_pos, out_gain, lb_all):
    ones = jnp.ones((SLOT_WIDTH,), F32)
    zeros = jnp.zeros((SLOT_WIDTH,), F32)
    gain = lambda i: jnp.tile(qk_gain[li, i], NSA_KV_HEADS)
    row = lambda v: v.reshape(1, -1).astype(F32)
    cp = cmp_pos[li]
    half = lambda w, m: w[m * CMP_STRIDE:(m + 1) * CMP_STRIDE]
    pos = lambda m: jnp.concatenate([half(cp[0], m)] * NSA_KV_HEADS + [half(cp[1], m)] * NSA_KV_HEADS, axis=1)
    w = dict(stacked)
    w.update(
        li=li,
        norm=[row(norms[li, i]) for i in range(4)],
        q_gain=row(jnp.tile(qk_gain[li, 0], NSA_HEADS)),
        q_flag=jnp.ones((1, NSA_WIDTH), F32),
        kv_gain=row(jnp.concatenate([ones, ones, gain(1), ones])),
        kv_flag=row(jnp.concatenate([zeros, zeros, ones, zeros])),
        win_gain=row(jnp.concatenate([gain(2), ones])),
        win_flag=row(jnp.concatenate([ones, zeros])),
        cmp_p0=pos(0).astype(F32),
        cmp_p1=pos(1).astype(F32),
        w_phi_k=stacked["w_phi"][li, 0],
        w_phi_v=stacked["w_phi"][li, 1],
        kc_gain=row(qk_gain[li, 3]),
        ret_gain=row(out_gain[li, 0]),
        hgrn_gain=row(out_gain[li, 1]),
        lb=lb_all[li:li + 1],
    )
    return w


def _tiles(m):
    big = 512 if m % 512 == 0 else m
    small = 256 if m % 256 == 0 else m
    return big, small


def _projections(u, w, tm, row_bufs, cache=None):
    li, depth = w["li"], w["depth"]
    head = functools.partial(_proj, u, w["w_head"], li, tm=tm)
    q = head(col0=0, n=NSA_WIDTH, tn=512, out_dtype=BF16, gain=w["q_gain"], flag=w["q_flag"], name="proj_q")
    kvrows, kv_buf = head(col0=KV_OFF, n=4 * SLOT_WIDTH, tn=4 * SLOT_WIDTH, out_dtype=F32, gain=w["kv_gain"],
                          flag=w["kv_flag"], name="proj_kv", rows_out=(depth, row_bufs[0]))
    winrows, win_buf = head(col0=KV_OFF + 4 * SLOT_WIDTH, n=2 * SLOT_WIDTH, tn=2 * SLOT_WIDTH, out_dtype=F32,
                            gain=w["win_gain"], flag=w["win_flag"], name="proj_win", rows_out=(depth, row_bufs[1]))
    tn = 1152
    pages = None if cache is None else _cache_sum_pages(u.shape[0], tm, PLAIN_WIDTH, tn, cache[1])
    if pages is None:
        zplain = _proj(u, w["w_plain"], li, col0=0, n=PLAIN_WIDTH, tm=tm, tn=tn, out_dtype=F32, name="proj_plain")
        return q, kvrows, winrows, zplain, (kv_buf, win_buf), None
    zplain, a0, a1 = _proj_with_cache_sums(u, w["w_plain"], li, cache[0], cache[1], w["cmp_p0"], w["cmp_p1"],
                                           n=PLAIN_WIDTH, tm=tm, tn=tn, pages=pages)
    return q, kvrows, winrows, zplain, (kv_buf, win_buf), (a0, a1)


def _layer_tail(h, o_nsa, o_ret, o_hgrn, zplain, p, w, tm, ts):
    li = w["li"]
    h = _merge(o_nsa, o_ret, o_hgrn, zplain, h, w["w_branch"], w["w_o"], li, tm=ts)
    h, u = _ffn(h, w["norm"][2], w["w_gu"], w["w_down"], w["norm"][3], li, 1, tm=tm, tf=512)
    return _ple(h, u, p, w["w_ple_gate"], w["w_ple_proj"], li, tm=ts)


def _prompt_layer(x, p, w, row_bufs, *, n, t, cache=None):
    tm, ts = _tiles(n * t)
    h, u = _ffn(x, w["norm"][0], w["w_gu"], w["w_down"], w["norm"][1], w["li"], 0, tm=tm, tf=512)
    q, kvrows, winrows, zplain, row_bufs, cache_sums = _projections(
        u, w, 1024 if (n * t) % 1024 == 0 else tm, row_bufs, cache)
    a0, a1 = _cmp_partial(kvrows, w["cmp_p0"], w["cmp_p1"], rb=min(1024, t))
    ng = t // CMP_STRIDE
    kc, vc = _cmp_final(a0.reshape(n, ng, -1), a1.reshape(n, ng, -1), w["w_phi_k"], w["w_phi_v"], w["kc_gain"])
    o_nsa = _nsa_prompt(q, zplain, kc, vc, kvrows, winrows, n=n, t=t, tq=2 * LANES)
    o_ret, s_ret = _retention(zplain, w["ret_gain"], None, 0, n=n, t=t, c=LANES)
    o_hgrn, s_hgrn = _hgrn(zplain, w["lb"], w["hgrn_gain"], None, 0, n=n, t=t, c=LANES)
    h = _layer_tail(h, o_nsa, o_ret, o_hgrn, zplain, p, w, tm, ts)
    return h, row_bufs, s_ret, s_hgrn, cache_sums


def _nsa_dec_cmp_body(q_ref, kc_ref, vc_ref, ov_ref, ocmp_ref, sel_ref, *, past, steps, n_slc):
    n, gb = q_ref.shape[0], q_ref.shape[1]
    nc = kc_ref.shape[1]
    hpg = HEADS_PER_GROUP
    rows = hpg * steps
    ri = lax.broadcasted_iota(jnp.int32, (rows, 1), 0)
    qpos_i = past + ri % steps
    ci = lax.broadcasted_iota(jnp.int32, (rows, nc), 1)
    dist_i = qpos_i - (ci * CMP_STRIDE + CMP_BLOCK - 1)
    allow = (dist_i >= 0) & (ci < nc - 1)
    dist = dist_i.astype(F32)
    imps = []
    for i in range(n):
        for g in range(gb):
            slope = jnp.exp2(-(ri // steps + 1 + g * hpg).astype(F32))
            sl = slice(g * HEAD_DIM, (g + 1) * HEAD_DIM)
            p = _masked_softmax(_dot_nt(q_ref[i, g], kc_ref[i, :, sl]) * ATT_SCALE - slope * dist, allow)
            ocmp_ref[i, g] = _dot(p, vc_ref[i, :, sl])
            imp = p[0:steps]
            for r in range(1, hpg):
                imp = imp + p[r * steps:(r + 1) * steps]
            imps.append(imp)
    imp_all = jnp.concatenate(imps, axis=0)

    width = ov_ref.shape[1]
    all_rows = n * gb * steps
    qpos_all = past + lax.broadcasted_iota(jnp.int32, (all_rows, 1), 0) % steps
    score = _block_scores(imp_all, qpos_all, ov_ref[...], n_slc)
    sj = lax.broadcasted_iota(jnp.int32, score.shape, 1)
    lane = lax.broadcasted_iota(jnp.int32, (all_rows, LANES), 1)
    picked = jnp.full((all_rows, LANES), -1, jnp.int32)
    for it in range(min(N_SEL, n_slc)):
        m = jnp.max(score, axis=-1, keepdims=True)
        idx = jnp.min(jnp.where(score == m, sj, width), axis=-1, keepdims=True)
        picked = jnp.where(lane == it, jnp.where(m >= 0.0, idx, -1), picked)
        score = jnp.where(sj == idx, -3.0, score)
    sel_ref[...] = picked


def _nsa_dec_cmp(q_rt, kc, vc, *, past, steps, n_slc):
    n, ng = kc.shape[0], kc.shape[1]
    gb = NSA_KV_HEADS
    ov = _overlap_matrix(ng, n_slc, -(-n_slc // LANES) * LANES)
    o_cmp, sel = pl.pallas_call(
        functools.partial(_nsa_dec_cmp_body, past=past, steps=steps, n_slc=n_slc),
        out_shape=[
            jax.ShapeDtypeStruct((n, gb, HEADS_PER_GROUP * steps, HEAD_DIM), F32),
            jax.ShapeDtypeStruct((n * gb * steps, LANES), jnp.int32),
        ],
        compiler_params=pltpu.CompilerParams(vmem_limit_bytes=VMEM_LIMIT),
        name="nsa_dec_cmp",
    )(q_rt, kc, vc, ov)
    return o_cmp, sel.reshape(n, gb, steps, LANES)


ROW_PAD = 8


def _nsa_dec_attend_body(pt_ref, sel_ref, q_ref, gate_ref, ocmp_ref, kvn_ref, wn_ref, wp_ref, *refs,
                         past, steps, n_sel, past_blocks):
    gb = NSA_KV_HEADS
    blk_refs = refs[:gb * n_sel]
    o_ref = refs[gb * n_sel]
    i = pl.program_id(0)
    t = pl.program_id(1)
    qpos = past + t
    head = jnp.minimum(lax.broadcasted_iota(jnp.int32, (ROW_PAD, 1), 0), HEADS_PER_GROUP - 1)
    new_i = lax.broadcasted_iota(jnp.int32, (1, steps), 1)
    new_allow = new_i <= t
    new_dist = (t - new_i).astype(F32)
    cols = n_sel * SEL_BLOCK
    col = lax.broadcasted_iota(jnp.int32, (1, cols), 1)
    col_pick = col // SEL_BLOCK
    head_cols = lambda ref, blk: ref[:, blk * HEAD_DIM:(blk + 1) * HEAD_DIM]

    def two_part_attention(s_a, allow_a, v_a, s_b, allow_b, v_b):
        s_a = jnp.where(allow_a, s_a, NEG_INF)
        s_b = jnp.where(allow_b, s_b, NEG_INF)
        m = jnp.maximum(jnp.max(s_a, axis=-1, keepdims=True), jnp.max(s_b, axis=-1, keepdims=True))
        e_a = jnp.where(allow_a, jnp.exp(s_a - m), 0.0)
        e_b = jnp.where(allow_b, jnp.exp(s_b - m), 0.0)
        l = jnp.sum(e_a, axis=-1, keepdims=True) + jnp.sum(e_b, axis=-1, keepdims=True)
        return (_dot(e_a, v_a) + _dot(e_b, v_b)) / l

    for g in range(gb):
        q = q_ref[g]
        slope = jnp.exp2(-(head + 1 + g * HEADS_PER_GROUP).astype(F32))
        picked = [blk_refs[g * n_sel + k] for k in range(n_sel)]
        kall = jnp.concatenate([r[pl.ds(2 * gb + g, SEL_BLOCK, stride=CACHE_SUB), :] for r in picked], axis=0)
        vall = jnp.concatenate([r[pl.ds(3 * gb + g, SEL_BLOCK, stride=CACHE_SUB), :] for r in picked], axis=0)
        base = ((i * gb + g) * steps + t) * n_sel
        blk = jnp.zeros((1, cols), jnp.int32)
        for k in range(n_sel):
            blk = jnp.where(col_pick == k, sel_ref[base + k], blk)
        dist_i = qpos - (blk * SEL_BLOCK + col % SEL_BLOCK)
        allow = (blk >= 0) & (blk < past_blocks) & (dist_i >= 0)
        s_sel = _dot_nt(q, kall) * ATT_SCALE - slope * dist_i.astype(F32)
        s_new = _dot_nt(q, head_cols(kvn_ref, 2 * gb + g)) * ATT_SCALE - slope * new_dist
        o_slc = two_part_attention(s_sel, allow, vall, s_new, new_allow, head_cols(kvn_ref, 3 * gb + g))

        wb = wp_ref.shape[0] // (2 * gb)
        kwp = wp_ref[pl.ds(g, wb, stride=2 * gb), :]
        vwp = wp_ref[pl.ds(gb + g, wb, stride=2 * gb), :]
        wdist_i = qpos - (past - wb + lax.broadcasted_iota(jnp.int32, (1, wb), 1))
        wallow = (wdist_i >= 0) & (wdist_i <= WINDOW)
        s_wp = _dot_nt(q, kwp) * ATT_SCALE - slope * wdist_i.astype(F32)
        s_wn = _dot_nt(q, head_cols(wn_ref, g)) * ATT_SCALE - slope * new_dist
        o_win = two_part_attention(s_wp, wallow, vwp, s_wn, new_allow, head_cols(wn_ref, gb + g))

        gates = _sigmoid(gate_ref[g])
        o = gates[:, 0:1] * ocmp_ref[g] + gates[:, 1:2] * o_slc + gates[:, 2:3] * o_win
        o_ref[g] = o.astype(o_ref.dtype)


def _nsa_dec_attend(page_table, sel_flat, q_tr, gates_tr, ocmp_tr, kvrows, winrows, cache_half, win_rows_view, li, *,
                    past, steps, n_sel):
    n = q_tr.shape[0]
    rows = steps * ROW_PAD
    gb = NSA_KV_HEADS
    halves = cache_half.shape[2]
    past_blocks = past // SEL_BLOCK
    step = lambda i, t, pt, sel: (i, 0, t, 0)
    new = lambda i, t, pt, sel: (i, 0)

    def gather(g, k):
        def index(i, t, pt, sel):
            j = jnp.clip(sel[((i * gb + g) * steps + t) * n_sel + k], 0, past_blocks - 1)
            return (li, pt[i, j // halves], j % halves, 0, 0)
        return pl.BlockSpec((None, None, None, SEL_BLOCK * CACHE_SUB, HEAD_DIM), index)

    in_specs = [
        pl.BlockSpec((None, gb, ROW_PAD, HEAD_DIM), step),
        pl.BlockSpec((None, gb, ROW_PAD, LANES), step),
        pl.BlockSpec((None, gb, ROW_PAD, HEAD_DIM), step),
        pl.BlockSpec((steps, kvrows.shape[1]), new),
        pl.BlockSpec((steps, winrows.shape[1]), new),
        pl.BlockSpec((None, None, win_rows_view.shape[2], HEAD_DIM), lambda i, t, pt, sel: (li, i, 0, 0)),
    ]
    in_specs += [gather(g, k) for g in range(gb) for k in range(n_sel)]
    return pl.pallas_call(
        functools.partial(_nsa_dec_attend_body, past=past, steps=steps, n_sel=n_sel, past_blocks=past_blocks),
        grid_spec=pltpu.PrefetchScalarGridSpec(
            num_scalar_prefetch=2,
            grid=(n, steps),
            in_specs=in_specs,
            out_specs=pl.BlockSpec((None, gb, ROW_PAD, HEAD_DIM), step),
        ),
        out_shape=jax.ShapeDtypeStruct((n, gb, rows, HEAD_DIM), BF16),
        compiler_params=_cp(("parallel", "arbitrary")),
        name="nsa_dec_attend",
    )(page_table, sel_flat, q_tr, gates_tr, ocmp_tr, kvrows, winrows, win_rows_view, *([cache_half] * (gb * n_sel)))


def _decode_layer(x, p, w, row_bufs, cache_rows, cache_half, win_rows_view, state_ret, state_hgrn, page_table,
                  cache_sums, *, n, steps):
    m = n * steps
    li = w["li"]
    hpg, gb = HEADS_PER_GROUP, NSA_KV_HEADS
    past = page_table.shape[1] * (cache_rows.shape[2] // CACHE_SUB)
    n_slc = -(-(past + steps) // SEL_BLOCK)
    n_sel = min(N_SEL, n_slc)
    h, u = _ffn(x, w["norm"][0], w["w_gu"], w["w_down"], w["norm"][1], li, 0, tm=m, tf=512)
    q, kvrows, winrows, zplain, row_bufs, _ = _projections(u, w, m, row_bufs)
    if cache_sums is None:
        cache_sums = _cmp_partial_paged(cache_rows, page_table, w["cmp_p0"], w["cmp_p1"], li, pages=8)
    kc, vc = _cmp_final(*cache_sums, w["w_phi_k"], w["w_phi_v"], w["kc_gain"])
    q5 = q.reshape(n, steps, gb, hpg, HEAD_DIM)
    q_rt = q5.transpose(0, 2, 3, 1, 4).reshape(n, gb, hpg * steps, HEAD_DIM)
    o_cmp, sel = _nsa_dec_cmp(q_rt, kc, vc, past=past, steps=steps, n_slc=n_slc)
    pad_heads = lambda a: jnp.pad(a, ((0, 0), (0, 0), (0, 0), (0, ROW_PAD - hpg), (0, 0)))
    rows = steps * ROW_PAD
    q_tr = pad_heads(q5.transpose(0, 2, 1, 3, 4)).reshape(n, gb, rows, HEAD_DIM)
    ocmp_tr = pad_heads(o_cmp.reshape(n, gb, hpg, steps, HEAD_DIM).transpose(0, 1, 3, 2, 4)).reshape(
        n, gb, rows, HEAD_DIM)
    zg = zplain[:, PLAIN_GATE_BLK * LANES:PLAIN_GATE_BLK * LANES + GATE_WIDTH].reshape(n, steps, gb, hpg, 3)
    gates_tr = pad_heads(zg.transpose(0, 2, 1, 3, 4)).reshape(n, gb, rows, 3)
    gates_tr = jnp.pad(gates_tr, ((0, 0), (0, 0), (0, 0), (0, LANES - 3)))
    sel_flat = sel[..., :n_sel].reshape(-1)
    o_tr = _nsa_dec_attend(page_table, sel_flat, q_tr, gates_tr, ocmp_tr, kvrows, winrows, cache_half, win_rows_view,
                           li, past=past, steps=steps, n_sel=n_sel)
    o_nsa = o_tr.reshape(n, gb, steps, ROW_PAD, HEAD_DIM)[:, :, :, :hpg].transpose(0, 2, 1, 3, 4).reshape(
        m, NSA_WIDTH)
    o_ret, s_ret = _retention(zplain, w["ret_gain"], state_ret, li, n=n, t=steps, c=steps)
    o_hgrn, s_hgrn = _hgrn(zplain, w["lb"], w["hgrn_gain"], state_hgrn, li, n=n, t=steps, c=steps)
    h = _layer_tail(h, o_nsa, o_ret, o_hgrn, zplain, p, w, m, m)
    return h, row_bufs, s_ret, s_hgrn


def kernel(x_prompt, x_sample, cache_kv, cache_win, state_ret, state_hgrn, page_table, p_prompt, p_sample, norms,
           w_in, qk_gain, cmp_pos, cmp_w, out_gain, hgrn_lb, w_branch, w_o, w_gu, w_down, w_ple_proj, w_ple_gate):
    n_p, t_p, d = x_prompt.shape
    n_d, t_d, _ = x_sample.shape
    depth, n_pool, page = cache_kv.shape[:3]
    cache_rows = cache_kv.reshape(depth, n_pool, page * CACHE_SUB, HEAD_DIM)
    cache_half = cache_kv.reshape(depth, n_pool, page // SEL_BLOCK, SEL_BLOCK * CACHE_SUB, HEAD_DIM)
    wb = cache_win.shape[2]
    win_rows_view = cache_win.reshape(depth, n_d, wb * 2 * NSA_KV_HEADS, HEAD_DIM)
    lb_all = _hgrn_lower_bounds(hgrn_lb)
    stacked = _stacked_weights(w_in, cmp_w, w_branch, w_o, w_gu, w_down, w_ple_proj, w_ple_gate)
    hp = x_prompt.reshape(n_p * t_p, d)
    hs = x_sample.reshape(n_d * t_d, d)
    bufs_p = bufs_s = (None, None)
    states = [[] for _ in range(4)]
    for li in range(depth):
        w = _layer_weights(li, stacked, norms, qk_gain, cmp_pos, out_gain, lb_all)
        w["depth"] = depth
        hp, bufs_p, s_r, s_h, cache_sums = _prompt_layer(hp, p_prompt[li].reshape(n_p * t_p, -1), w, bufs_p,
                                                         n=n_p, t=t_p, cache=(cache_rows, page_table))
        states[0].append(s_r)
        states[2].append(s_h)
        hs, bufs_s, s_r, s_h = _decode_layer(hs, p_sample[li].reshape(n_d * t_d, -1), w, bufs_s, cache_rows,
                                             cache_half, win_rows_view, state_ret, state_hgrn, page_table,
                                             cache_sums, n=n_d, steps=t_d)
        states[1].append(s_r)
        states[3].append(s_h)
    gb = NSA_KV_HEADS
    kv_p = bufs_p[0].reshape(depth, n_p, t_p, 4, gb, HEAD_DIM)
    kv_s = bufs_s[0].reshape(depth, n_d, t_d, 4, gb, HEAD_DIM)
    win_p = bufs_p[1].reshape(depth, n_p, t_p, 2, gb, HEAD_DIM)[:, :, t_p - min(WINDOW, t_p):]
    win_all = jnp.concatenate([cache_win, bufs_s[1].reshape(depth, n_d, t_d, 2, gb, HEAD_DIM)], axis=2)
    win_s = win_all[:, :, wb + t_d - min(WINDOW, wb + t_d):]
    ret_p, ret_s, hg_p, hg_s = (jnp.stack(s) for s in states)
    return (hp.reshape(n_p, t_p, d), hs.reshape(n_d, t_d, d), kv_p, kv_s, win_p, win_s, ret_p, ret_s, hg_p, hg_s)
```

```python
import functools
import math

import jax
import jax.numpy as jnp
import numpy as np
from jax import lax
from jax.experimental import pallas as pl
from jax.experimental.pallas import tpu as pltpu

F32 = jnp.float32
BF16 = jnp.bfloat16

D_MODEL = 2048
HEAD_DIM = 128
NSA_HEADS = 8
NSA_KV_HEADS = 2
HEADS_PER_GROUP = NSA_HEADS // NSA_KV_HEADS
CMP_BLOCK = 32
CMP_STRIDE = 16
SEL_BLOCK = 64
N_SEL = 16
WINDOW = 512
RET_HEADS = 4
HGRN_HEADS = 4
D_FF = 5632
PLE_DIM = 256
NSA_WIDTH = NSA_HEADS * HEAD_DIM
RET_WIDTH = RET_HEADS * HEAD_DIM
HGRN_WIDTH = HGRN_HEADS * HEAD_DIM
KV_WIDTH = 6 * NSA_KV_HEADS * HEAD_DIM
GATE_WIDTH = 3 * NSA_HEADS
NEG_INF = -1e30
FORCE_SCORE = 1e6
EPS = 1e-6
ATT_SCALE = HEAD_DIM ** -0.5

LANES = 128
PLAIN_RET_BLK = 0
PLAIN_HGRN_BLK = (4 * RET_WIDTH) // LANES
PLAIN_MERGE_OFF = 4 * RET_WIDTH + 4 * HGRN_WIDTH
PLAIN_GATE_BLK = (PLAIN_MERGE_OFF + 3 * D_MODEL) // LANES
PLAIN_WIDTH = PLAIN_MERGE_OFF + 3 * D_MODEL + LANES

VMEM_LIMIT = 56 * 1024 * 1024


def _cp(sem, vmem=VMEM_LIMIT):
    return pltpu.CompilerParams(dimension_semantics=sem, vmem_limit_bytes=vmem)


def _rms(x):
    return x * lax.rsqrt(jnp.mean(x * x, axis=-1, keepdims=True) + EPS)


def _dot(a, b):
    return jnp.dot(a.astype(BF16), b.astype(BF16), preferred_element_type=F32)


def _dot_nt(a, b):
    return lax.dot_general(a.astype(BF16), b.astype(BF16), (((1,), (1,)), ((), ())),
                           preferred_element_type=F32)


def _dot_tn(a, b):
    rows = a.shape[0]
    if rows % LANES:
        pad = LANES - rows % LANES
        a = jnp.concatenate([a, jnp.zeros((pad, a.shape[1]), a.dtype)], axis=0)
        b = jnp.concatenate([b, jnp.zeros((pad, b.shape[1]), b.dtype)], axis=0)
    return _dot(a.T, b)


def _split_bf16(x):
    hi = x.astype(BF16)
    lo = (x - hi.astype(F32)).astype(BF16)
    return hi, lo


def _sigmoid(x):
    return 1.0 / (1.0 + jnp.exp(-x))


def _ffn_body(x_ref, g1_ref, wg_ref, wv_ref, wd_ref, g2_ref, h_ref, u_ref, xn_sc, acc_sc, *, nf):
    j = pl.program_id(1)

    @pl.when(j == 0)
    def _():
        xn_sc[...] = (_rms(x_ref[...]) * g1_ref[...]).astype(BF16)
        acc_sc[...] = jnp.zeros_like(acc_sc)

    xn = xn_sc[...]
    g = jnp.dot(xn, wg_ref[...], preferred_element_type=F32)
    v = jnp.dot(xn, wv_ref[...], preferred_element_type=F32)
    a = (g * _sigmoid(g) * v).astype(BF16)
    acc_sc[...] += jnp.dot(a, wd_ref[...], preferred_element_type=F32)

    @pl.when(j == nf - 1)
    def _():
        h = x_ref[...] + 0.5 * acc_sc[...]
        h_ref[...] = h
        u_ref[...] = (_rms(h) * g2_ref[...]).astype(BF16)


def _ffn(x, g1, w_gu, w_down, g2, li, which, *, tm, tf):
    m, d = x.shape
    f = w_down.shape[2]
    nf = f // tf
    return pl.pallas_call(
        functools.partial(_ffn_body, nf=nf),
        grid=(m // tm, nf),
        in_specs=[
            pl.BlockSpec((tm, d), lambda i, j: (i, 0)),
            pl.BlockSpec((1, d), lambda i, j: (0, 0)),
            pl.BlockSpec((None, None, d, tf), lambda i, j: (li, which, 0, j)),
            pl.BlockSpec((None, None, d, tf), lambda i, j: (li, which, 0, j + nf)),
            pl.BlockSpec((None, None, tf, d), lambda i, j: (li, which, j, 0)),
            pl.BlockSpec((1, d), lambda i, j: (0, 0)),
        ],
        out_specs=[
            pl.BlockSpec((tm, d), lambda i, j: (i, 0)),
            pl.BlockSpec((tm, d), lambda i, j: (i, 0)),
        ],
        out_shape=[jax.ShapeDtypeStruct((m, d), F32), jax.ShapeDtypeStruct((m, d), BF16)],
        scratch_shapes=[pltpu.VMEM((tm, d), BF16), pltpu.VMEM((tm, d), F32)],
        compiler_params=_cp(("parallel", "arbitrary")),
        name="ffn",
    )(x, g1, w_gu, w_gu, w_down, g2)


def _proj_plain_body(u_ref, w_ref, o_ref):
    o_ref[...] = jnp.dot(u_ref[...], w_ref[...], preferred_element_type=F32).astype(o_ref.dtype)


def _proj_norm_body(u_ref, w_ref, gain_ref, flag_ref, *refs, tn, rows_out):
    o_ref = refs[-2] if rows_out else refs[-1]
    z = jnp.dot(u_ref[...], w_ref[...], preferred_element_type=F32)
    chunks = tn // LANES
    for c in range(chunks):
        sl = slice(c * LANES, (c + 1) * LANES)
        zc = z[:, sl]
        normed = _rms(zc) * gain_ref[:, sl]
        val = jnp.where(flag_ref[:, sl] > 0.5, normed, zc)
        o_ref[:, sl] = val.astype(o_ref.dtype)
        if rows_out:
            refs[-1][pl.ds(c, z.shape[0], stride=chunks), :] = val


def _proj(u, w, li, *, col0, n, tm, tn, out_dtype, gain=None, flag=None, name, rows_out=None):
    m, k = u.shape
    cb = col0 // tn
    in_specs = [pl.BlockSpec((tm, k), lambda i, j: (i, 0)), pl.BlockSpec((None, k, tn), lambda i, j: (li, 0, cb + j))]
    args = [u, w]
    out_specs = pl.BlockSpec((tm, tn), lambda i, j: (i, j))
    out_shape = jax.ShapeDtypeStruct((m, n), out_dtype)
    aliases = {}
    if gain is None:
        body = _proj_plain_body
    else:
        body = functools.partial(_proj_norm_body, tn=tn, rows_out=rows_out is not None)
        in_specs += [pl.BlockSpec((1, tn), lambda i, j: (0, j)), pl.BlockSpec((1, tn), lambda i, j: (0, j))]
        args += [gain, flag]
    if rows_out is not None:
        assert n == tn, "the row-major store needs the whole width in one column tile"
        depth, prev = rows_out
        chunks = n // LANES
        out_specs = [out_specs, pl.BlockSpec((None, tm * chunks, LANES), lambda i, j: (li, i, 0))]
        out_shape = [out_shape, jax.ShapeDtypeStruct((depth, m * chunks, LANES), F32)]
        if prev is not None:
            in_specs.append(pl.BlockSpec(memory_space=pl.ANY))
            args.append(prev)
            aliases = {len(args) - 1: 1}
    return pl.pallas_call(
        body,
        grid=(m // tm, n // tn),
        in_specs=in_specs,
        out_specs=out_specs,
        out_shape=out_shape,
        input_output_aliases=aliases,
        compiler_params=_cp(("parallel", "arbitrary")),
        name=name,
    )(*args)


def _merge_body(oa_ref, ob_ref, oc_ref, m0_ref, m1_ref, m2_ref, h_ref, wb_ref, wo_ref, out_ref):
    ya = jnp.dot(oa_ref[...], wb_ref[0:NSA_WIDTH, :], preferred_element_type=F32)
    yb = jnp.dot(ob_ref[...], wb_ref[NSA_WIDTH:NSA_WIDTH + RET_WIDTH, :], preferred_element_type=F32)
    yc = jnp.dot(oc_ref[...], wb_ref[NSA_WIDTH + RET_WIDTH:, :], preferred_element_type=F32)
    mixed = _sigmoid(m0_ref[...]) * ya + _sigmoid(m1_ref[...]) * yb + _sigmoid(m2_ref[...]) * yc
    out_ref[...] = h_ref[...] + jnp.dot(mixed.astype(BF16), wo_ref[...], preferred_element_type=F32)


def _merge(o_nsa, o_ret, o_hgrn, zplain, h, w_branch, w_o, li, *, tm):
    m, d = h.shape
    mb = PLAIN_MERGE_OFF // d
    row = lambda i: (i, 0)
    layer = lambda i: (li, 0, 0)
    return pl.pallas_call(
        _merge_body,
        grid=(m // tm,),
        in_specs=[
            pl.BlockSpec((tm, NSA_WIDTH), row),
            pl.BlockSpec((tm, RET_WIDTH), row),
            pl.BlockSpec((tm, HGRN_WIDTH), row),
            pl.BlockSpec((tm, d), lambda i: (i, mb)),
            pl.BlockSpec((tm, d), lambda i: (i, mb + 1)),
            pl.BlockSpec((tm, d), lambda i: (i, mb + 2)),
            pl.BlockSpec((tm, d), row),
            pl.BlockSpec((None,) + w_branch.shape[1:], layer),
            pl.BlockSpec((None,) + w_o.shape[1:], layer),
        ],
        out_specs=pl.BlockSpec((tm, d), row),
        out_shape=jax.ShapeDtypeStruct((m, d), F32),
        compiler_params=_cp(("parallel",)),
        name="merge",
    )(o_nsa, o_ret, o_hgrn, zplain, zplain, zplain, h, w_branch, w_o)


def _ple_body(h_ref, u_ref, p_ref, wg_ref, wp_ref, out_ref):
    gate = _sigmoid(jnp.dot(u_ref[...], wg_ref[...], preferred_element_type=F32))
    proj = jnp.dot(p_ref[...].astype(BF16), wp_ref[...], preferred_element_type=F32)
    out_ref[...] = h_ref[...] + gate * proj


def _ple(h, u, p, w_gate, w_proj, li, *, tm):
    m, d = h.shape
    row = lambda i: (i, 0)
    layer = lambda i: (li, 0, 0)
    return pl.pallas_call(
        _ple_body,
        grid=(m // tm,),
        in_specs=[
            pl.BlockSpec((tm, d), row),
            pl.BlockSpec((tm, d), row),
            pl.BlockSpec((tm, p.shape[1]), row),
            pl.BlockSpec((None,) + w_gate.shape[1:], layer),
            pl.BlockSpec((None,) + w_proj.shape[1:], layer),
        ],
        out_specs=pl.BlockSpec((tm, d), row),
        out_shape=jax.ShapeDtypeStruct((m, d), F32),
        compiler_params=_cp(("parallel",)),
        name="ple",
    )(h, u, p, w_gate, w_proj)


def _retention_tables(c):
    lg = np.log1p(-np.exp2(-5.0 - np.arange(RET_HEADS, dtype=np.float64)))
    i = np.arange(c, dtype=np.float64)
    diff = i[:, None] - i[None, :]
    dmat = np.where(diff >= 0, np.exp(lg[:, None, None] * np.maximum(diff, 0.0)), 0.0)
    q_dec = np.exp(lg[:, None] * (i + 1.0))[..., None] * np.ones((1, 1, HEAD_DIM))
    k_dec = np.exp(lg[:, None] * (c - 1.0 - i))[..., None] * np.ones((1, 1, HEAD_DIM))
    c_dec = np.exp(lg * c)[:, None, None] * np.ones((1, 8, HEAD_DIM))
    return tuple(jnp.asarray(a, F32) for a in (dmat, q_dec, k_dec, c_dec))


def _retention_body(*refs, has_state, nch, c):
    if has_state:
        q_ref, k_ref, v_ref, g_ref, dm_ref, qd_ref, kd_ref, cd_ref, gain_ref, s0_ref, o_ref, s_ref = refs
    else:
        q_ref, k_ref, v_ref, g_ref, dm_ref, qd_ref, kd_ref, cd_ref, gain_ref, o_ref, s_ref = refs
    heads = range(RET_HEADS)
    cols = lambda hh: slice(hh * HEAD_DIM, (hh + 1) * HEAD_DIM)

    def chunk(ci, states):
        rows = pl.ds(pl.multiple_of(ci * c, c), c)
        out = []
        for hh in heads:
            q = q_ref[rows, cols(hh)]
            k = k_ref[rows, cols(hh)] * ATT_SCALE
            v = v_ref[rows, cols(hh)]
            a = _dot_nt(q, k) * dm_ref[hh]
            o = _dot(a, v) + _dot(q * qd_ref[hh], states[hh])
            g = g_ref[rows, cols(hh)]
            o_ref[rows, cols(hh)] = (_rms(o) * gain_ref[:, cols(hh)] * (g * _sigmoid(g))).astype(o_ref.dtype)
            out.append(states[hh] * cd_ref[hh, 0:1, :] + _dot_tn(k * kd_ref[hh], v))
        return tuple(out)

    init = tuple(s0_ref[hh] if has_state else jnp.zeros((LANES, LANES), F32) for hh in heads)
    final = lax.fori_loop(0, nch, chunk, init)
    for hh in heads:
        s_ref[hh] = final[hh]


def _retention(zplain, gain, s0, li, *, n, t, c):
    nch = t // c
    tables = _retention_tables(c)
    hb = RET_HEADS
    width = hb * HEAD_DIM
    first = PLAIN_RET_BLK * LANES // width
    in_specs = [pl.BlockSpec((t, width), (lambda w: (lambda i: (i, first + w)))(w)) for w in range(4)]
    in_specs += [pl.BlockSpec(tab.shape, lambda i: (0, 0, 0)) for tab in tables]
    in_specs.append(pl.BlockSpec((1, width), lambda i: (0, 0)))
    args = [zplain] * 4 + list(tables) + [gain]
    if s0 is not None:
        in_specs.append(pl.BlockSpec((None, None, hb, LANES, LANES), lambda i: (li, i, 0, 0, 0)))
        args.append(s0)
    return pl.pallas_call(
        functools.partial(_retention_body, has_state=s0 is not None, nch=nch, c=c),
        grid=(n,),
        in_specs=in_specs,
        out_specs=[
            pl.BlockSpec((t, width), lambda i: (i, 0)),
            pl.BlockSpec((None, hb, LANES, LANES), lambda i: (i, 0, 0, 0)),
        ],
        out_shape=[
            jax.ShapeDtypeStruct((n * t, RET_WIDTH), BF16),
            jax.ShapeDtypeStruct((n, hb, LANES, LANES), F32),
        ],
        compiler_params=_cp(("parallel",)),
        name="retention",
    )(*args)


HGRN_SUB = 16


def _hgrn_body(*refs, has_state, nch, c):
    if has_state:
        q_ref, f_ref, v_ref, g_ref, lb_ref, gain_ref, s0_ref, o_ref, s_ref, st_sc, b_sc, k_sc, v_sc, o_sc = refs
    else:
        q_ref, f_ref, v_ref, g_ref, lb_ref, gain_ref, o_ref, s_ref, st_sc, b_sc, k_sc, v_sc, o_sc = refs
    ci = pl.program_id(1)
    heads = range(HGRN_HEADS)
    cols = lambda hh: slice(hh * HEAD_DIM, (hh + 1) * HEAD_DIM)

    @pl.when(ci == 0)
    def _():
        for hh in heads:
            st_sc[hh] = s0_ref[hh].T if has_state else jnp.zeros((LANES, LANES), F32)

    ri = lax.broadcasted_iota(jnp.int32, (c, c), 0)
    si = lax.broadcasted_iota(jnp.int32, (c, c), 1)
    tri = jnp.where(ri >= si, 1.0, 0.0).astype(BF16)
    q, kk, v, b, st = [], [], [], [], []
    for hh in heads:
        lb = lb_ref[:, cols(hh)]
        f = lb + (1.0 - lb) * _sigmoid(f_ref[:, cols(hh)])
        logf = jnp.log(f)
        hi = logf.astype(BF16)
        r1 = logf - hi.astype(F32)
        mid = r1.astype(BF16)
        lo = (r1 - mid.astype(F32)).astype(BF16)
        b.append(jnp.dot(tri, hi, preferred_element_type=F32) + jnp.dot(tri, mid, preferred_element_type=F32)
                 + jnp.dot(tri, lo, preferred_element_type=F32))
        q.append(q_ref[:, cols(hh)])
        kk.append(1.0 - f)
        v.append(v_ref[:, cols(hh)])
        b_sc[hh] = b[hh]
        k_sc[hh] = kk[hh]
        v_sc[hh] = v[hh]
        st.append(st_sc[hh])
        o_sc[hh] = _dot_nt(q[hh] * jnp.exp(b[hh]), st[hh])

    sub = min(HGRN_SUB, c)
    row_id = lax.broadcasted_iota(jnp.int32, (sub, LANES), 0)
    for blk in range(c // sub):
        r0 = blk * sub
        b_i = [b[hh][r0:r0 + sub] for hh in heads]
        q_i = [q[hh][r0:r0 + sub] for hh in heads]
        acc = [jnp.zeros((sub, LANES), F32) for _ in heads]
        if blk > 0:
            for hh in heads:
                ref = b_sc[hh, r0 - 1:r0, :]
                qt = q_i[hh] * jnp.exp(b_i[hh] - ref)
                kt = kk[hh][0:r0] * jnp.exp(ref - b[hh][0:r0])
                acc[hh] = _dot(_dot_nt(qt, kt), v[hh][0:r0])
        for s in range(sub):
            for hh in heads:
                b_s = b_sc[hh, r0 + s:r0 + s + 1, :]
                k_s = k_sc[hh, r0 + s:r0 + s + 1, :]
                v_s = v_sc[hh, r0 + s:r0 + s + 1, :]
                w = q_i[hh] * jnp.exp(jnp.minimum(b_i[hh] - b_s, 0.0)) * k_s
                w = jnp.where(row_id >= s, w, 0.0)
                acc[hh] = acc[hh] + jnp.sum(w, axis=-1, keepdims=True) * v_s
        for hh in heads:
            o_sc[hh, r0:r0 + sub, :] += acc[hh]

    for hh in heads:
        b_last = b_sc[hh, c - 1:c, :]
        st_sc[hh] = st[hh] * jnp.exp(b_last) + _dot_tn(v[hh], kk[hh] * jnp.exp(b_last - b[hh]))
        g = g_ref[:, cols(hh)]
        o_ref[:, cols(hh)] = (_rms(o_sc[hh]) * gain_ref[:, cols(hh)] * (g * _sigmoid(g))).astype(o_ref.dtype)

    @pl.when(ci == nch - 1)
    def _():
        for hh in heads:
            s_ref[hh] = st_sc[hh].T


def _hgrn(zplain, lb, gain, s0, li, *, n, t, c):
    nch = t // c
    hb = HGRN_HEADS
    width = hb * HEAD_DIM
    first = PLAIN_HGRN_BLK * LANES // width
    row = lambda w: (lambda i, j: (i * nch + j, first + w))
    const = lambda i, j: (0, 0)
    in_specs = [pl.BlockSpec((c, width), row(w)) for w in range(4)]
    in_specs += [pl.BlockSpec((1, width), const), pl.BlockSpec((1, width), const)]
    args = [zplain] * 4 + [lb, gain]
    if s0 is not None:
        in_specs.append(pl.BlockSpec((None, None, hb, LANES, LANES), lambda i, j: (li, i, 0, 0, 0)))
        args.append(s0)
    return pl.pallas_call(
        functools.partial(_hgrn_body, has_state=s0 is not None, nch=nch, c=c),
        grid=(n, nch),
        in_specs=in_specs,
        out_specs=[
            pl.BlockSpec((c, width), lambda i, j: (i * nch + j, 0)),
            pl.BlockSpec((None, hb, LANES, LANES), lambda i, j: (i, 0, 0, 0)),
        ],
        out_shape=[
            jax.ShapeDtypeStruct((n * t, HGRN_WIDTH), BF16),
            jax.ShapeDtypeStruct((n, hb, LANES, LANES), F32),
        ],
        scratch_shapes=[pltpu.VMEM((hb, LANES, LANES), F32)] + [pltpu.VMEM((hb, c, LANES), F32)] * 4,
        compiler_params=_cp(("parallel", "arbitrary")),
        name="hgrn",
    )(*args)


def _lb_body(x_ref, o_ref):
    x = x_ref[...]
    e = jnp.exp(x - jnp.max(x, axis=0, keepdims=True))
    sm = e / jnp.sum(e, axis=0, keepdims=True)
    acc = jnp.zeros_like(sm[0:1])
    o_ref[0:1, :] = acc
    for layer in range(1, x.shape[0]):
        acc = acc + sm[layer:layer + 1]
        o_ref[layer:layer + 1, :] = acc


def _hgrn_lower_bounds(hgrn_lb):
    return pl.pallas_call(
        _lb_body,
        out_shape=jax.ShapeDtypeStruct(hgrn_lb.shape, F32),
        name="hgrn_lb",
    )(hgrn_lb.astype(F32))


def _group_sums(x, p0, p1):
    rows, width = x.shape
    xg = x.reshape(rows // CMP_STRIDE, CMP_STRIDE, width)
    return jnp.sum(xg * p0[None], axis=1), jnp.sum(xg * p1[None], axis=1)


def _cmp_partial_body(x_ref, p0_ref, p1_ref, a0_ref, a1_ref):
    a0, a1 = _group_sums(x_ref[...], p0_ref[...], p1_ref[...])
    a0_ref[...] = a0
    a1_ref[...] = a1


def _cmp_partial(kvrows, p0, p1, *, rb):
    m = kvrows.shape[0]
    w = p0.shape[1]
    const = lambda i: (0, 0)
    out = jax.ShapeDtypeStruct((m // CMP_STRIDE, w), F32)
    return pl.pallas_call(
        _cmp_partial_body,
        grid=(m // rb,),
        in_specs=[pl.BlockSpec((rb, w), lambda i: (i, 0)), pl.BlockSpec(p0.shape, const), pl.BlockSpec(p1.shape, const)],
        out_specs=[pl.BlockSpec((rb // CMP_STRIDE, w), lambda i: (i, 0))] * 2,
        out_shape=[out, out],
        compiler_params=_cp(("parallel",)),
        name="cmp_partial",
    )(kvrows, p0, p1)


CACHE_SUB = 4 * NSA_KV_HEADS


def _page_group_sums(x_refs, p0_ref, p1_ref, a0_ref, a1_ref, page):
    gp = page // CMP_STRIDE
    for k, x_ref in enumerate(x_refs):
        for s in range(2 * NSA_KV_HEADS):
            sl = slice(s * HEAD_DIM, (s + 1) * HEAD_DIM)
            x = x_ref[pl.ds(s, page, stride=CACHE_SUB), :]
            a0, a1 = _group_sums(x, p0_ref[:, sl], p1_ref[:, sl])
            a0_ref[k * gp:(k + 1) * gp, sl] = a0
            a1_ref[k * gp:(k + 1) * gp, sl] = a1


def _cmp_partial_paged_body(pt_ref, *refs, pages, page):
    p0_ref, p1_ref, a0_ref, a1_ref = refs[pages:]
    _page_group_sums(refs[:pages], p0_ref, p1_ref, a0_ref, a1_ref, page)


def _proj_cache_sums_body(pt_ref, u_ref, w_ref, *refs, pages, page):
    p0_ref, p1_ref, o_ref, a0_ref, a1_ref = refs[pages:]
    o_ref[...] = jnp.dot(u_ref[...], w_ref[...], preferred_element_type=F32)
    _page_group_sums(refs[:pages], p0_ref, p1_ref, a0_ref, a1_ref, page)


def _proj_with_cache_sums(u, w, li, cache_rows, page_table, p0, p1, *, n, tm, tn, pages):
    m, k = u.shape
    nd, n_pages = page_table.shape
    page = cache_rows.shape[2] // CACHE_SUB
    width = p0.shape[1]
    gp = page // CMP_STRIDE
    chunks = n_pages // pages
    units = nd * chunks
    ni, nj = m // tm, n // tn
    unit = lambda i, j: jnp.minimum(i * nj + j, units - 1)
    const = lambda i, j, pt: (0, 0)
    page_spec = lambda kk: pl.BlockSpec(
        (None, None, page * CACHE_SUB, HEAD_DIM),
        lambda i, j, pt: (li, pt[unit(i, j) // chunks, (unit(i, j) % chunks) * pages + kk], 0, 0))
    sums_spec = pl.BlockSpec((None, pages * gp, width), lambda i, j, pt: (unit(i, j) // chunks, unit(i, j) % chunks, 0))
    sums = jax.ShapeDtypeStruct((nd, n_pages * gp, width), F32)
    return pl.pallas_call(
        functools.partial(_proj_cache_sums_body, pages=pages, page=page),
        grid_spec=pltpu.PrefetchScalarGridSpec(
            num_scalar_prefetch=1,
            grid=(ni, nj),
            in_specs=[pl.BlockSpec((tm, k), lambda i, j, pt: (i, 0)),
                      pl.BlockSpec((None, k, tn), lambda i, j, pt: (li, 0, j))]
            + [page_spec(kk) for kk in range(pages)]
            + [pl.BlockSpec(p0.shape, const), pl.BlockSpec(p1.shape, const)],
            out_specs=[pl.BlockSpec((tm, tn), lambda i, j, pt: (i, j)), sums_spec, sums_spec],
        ),
        out_shape=[jax.ShapeDtypeStruct((m, n), F32), sums, sums],
        compiler_params=_cp(("arbitrary", "arbitrary")),
        name="proj_plain_cache_sums",
    )(page_table, u, w, *([cache_rows] * pages), p0, p1)


def _cache_sum_pages(m, tm, n, tn, page_table):
    nd, n_pages = page_table.shape
    steps = (m // tm) * (n // tn)
    for pages in (4, 8, 16, 32):
        if n_pages % pages == 0 and nd * (n_pages // pages) <= steps:
            return pages
    return None


def _cmp_partial_paged(cache_rows, page_table, p0, p1, li, *, pages):
    n, n_pages = page_table.shape
    page = cache_rows.shape[2] // CACHE_SUB
    w = p0.shape[1]
    gp = page // CMP_STRIDE
    const = lambda i, c, pt: (0, 0)
    page_spec = lambda k: pl.BlockSpec((None, None, page * CACHE_SUB, HEAD_DIM),
                                       lambda i, c, pt: (li, pt[i, c * pages + k], 0, 0))
    out = jax.ShapeDtypeStruct((n, n_pages * gp, w), F32)
    return pl.pallas_call(
        functools.partial(_cmp_partial_paged_body, pages=pages, page=page),
        grid_spec=pltpu.PrefetchScalarGridSpec(
            num_scalar_prefetch=1,
            grid=(n, n_pages // pages),
            in_specs=[page_spec(k) for k in range(pages)] + [pl.BlockSpec(p0.shape, const), pl.BlockSpec(p1.shape, const)],
            out_specs=[pl.BlockSpec((None, pages * gp, w), lambda i, c, pt: (i, c, 0))] * 2,
        ),
        out_shape=[out, out],
        compiler_params=_cp(("parallel", "arbitrary")),
        name="cmp_partial_paged",
    )(page_table, *([cache_rows] * pages), p0, p1)


def _cmp_final_body(a0_ref, a1_ref, wk_ref, wv_ref, gain_ref, kc_ref, vc_ref):
    ng = a0_ref.shape[0]
    agg = a0_ref[...] + pltpu.roll(a1_ref[...], ng - 1, 0)
    for g in range(NSA_KV_HEADS):
        sl = slice(g * HEAD_DIM, (g + 1) * HEAD_DIM)
        ak = agg[:, sl]
        av = agg[:, NSA_KV_HEADS * HEAD_DIM + g * HEAD_DIM:NSA_KV_HEADS * HEAD_DIM + (g + 1) * HEAD_DIM]
        kc_ref[:, sl] = (_rms(_dot(ak, wk_ref[...])) * gain_ref[...]).astype(kc_ref.dtype)
        vc_ref[:, sl] = _dot(av, wv_ref[...]).astype(vc_ref.dtype)


def _cmp_final(a0, a1, w_k, w_v, gain):
    n, ng, w = a0.shape
    const = lambda i: (0, 0)
    out = jax.ShapeDtypeStruct((n, ng, NSA_KV_HEADS * HEAD_DIM), BF16)
    return pl.pallas_call(
        _cmp_final_body,
        grid=(n,),
        in_specs=[
            pl.BlockSpec((None, ng, w), lambda i: (i, 0, 0)),
            pl.BlockSpec((None, ng, w), lambda i: (i, 0, 0)),
            pl.BlockSpec(w_k.shape, const),
            pl.BlockSpec(w_v.shape, const),
            pl.BlockSpec(gain.shape, const),
        ],
        out_specs=[pl.BlockSpec((None, ng, NSA_KV_HEADS * HEAD_DIM), lambda i: (i, 0, 0))] * 2,
        out_shape=[out, out],
        compiler_params=_cp(("parallel",)),
        name="cmp_final",
    )(a0, a1, w_k, w_v, gain)


def _overlap_matrix(n_cmp_pad, n_slc, width):
    ci = np.arange(n_cmp_pad)[:, None]
    sj = np.arange(width)[None, :]
    c_start = ci * CMP_STRIDE
    s_start = sj * SEL_BLOCK
    hit = ((c_start < s_start + SEL_BLOCK) & (c_start + CMP_BLOCK > s_start)
           & (ci < n_cmp_pad - 1) & (sj < n_slc))
    return jnp.asarray(hit, BF16)


def _block_scores(imp, qpos_i, ov, n_slc):
    hi, lo = _split_bf16(imp)
    score = jnp.dot(hi, ov, preferred_element_type=F32) + jnp.dot(lo, ov, preferred_element_type=F32)
    sj = lax.broadcasted_iota(jnp.int32, score.shape, 1)
    cur = qpos_i // SEL_BLOCK
    forced = (sj == 0) | (sj == cur) | (sj == cur - 1)
    score = jnp.where(forced, FORCE_SCORE, score)
    score = jnp.where(sj * SEL_BLOCK <= qpos_i, score, -1.0)
    return jnp.where(sj < n_slc, score, -2.0)


def _masked_softmax(s, allow):
    s = jnp.where(allow, s, NEG_INF)
    m = jnp.max(s, axis=-1, keepdims=True)
    e = jnp.where(allow, jnp.exp(s - m), 0.0)
    return e / jnp.maximum(jnp.sum(e, axis=-1, keepdims=True), 1e-30)


SEL_KEY_BLOCK = 256


def _nsa_prompt_body(q_ref, zg_ref, kc_ref, vc_ref, ov_ref, ks_ref, vs_ref, kw_ref, vw_ref, o_ref, *, t_len, tq):
    g = pl.program_id(1)
    t0 = pl.program_id(2) * tq
    n_cmp_pad = kc_ref.shape[0]
    n_slc = -(-t_len // SEL_BLOCK)
    hpg = HEADS_PER_GROUP
    qpos_i = t0 + lax.broadcasted_iota(jnp.int32, (tq, 1), 0)
    slopes = [jnp.where(g == 0, 2.0 ** -(r + 1), 2.0 ** -(r + 1 + hpg)) for r in range(hpg)]
    qs = [q_ref[:, r * HEAD_DIM:(r + 1) * HEAD_DIM] for r in range(hpg)]

    ci = lax.broadcasted_iota(jnp.int32, (tq, n_cmp_pad), 1)
    cdist_i = qpos_i - (ci * CMP_STRIDE + CMP_BLOCK - 1)
    callow = (cdist_i >= 0) & (ci < n_cmp_pad - 1)
    cdist = cdist_i.astype(F32)
    kc = kc_ref[...]
    vc = vc_ref[...]
    imp = jnp.zeros((tq, n_cmp_pad), F32)
    o_cmp = []
    for r in range(hpg):
        p = _masked_softmax(_dot_nt(qs[r], kc) * ATT_SCALE - slopes[r] * cdist, callow)
        o_cmp.append(_dot(p, vc))
        imp = imp + p

    score = _block_scores(imp, qpos_i, ov_ref[...], n_slc)
    s_t = score.T[0:n_slc]
    jrow = lax.broadcasted_iota(jnp.int32, (n_slc, tq), 0)
    rank = jnp.zeros((n_slc, tq), jnp.int32)
    for jp in range(n_slc):
        row = s_t[jp:jp + 1, :]
        beats = (row > s_t) | ((row == s_t) & (jp < jrow))
        rank = rank + jnp.where(beats, 1, 0)
    sel_t = jnp.where((rank < N_SEL) & (s_t >= 0.0), 1.0, 0.0)
    sel_t = jnp.concatenate([sel_t, jnp.zeros((LANES - n_slc, tq), F32)], axis=0)
    sel = sel_t.T.astype(BF16)

    kb = SEL_KEY_BLOCK
    nkb = (t0 + tq + kb - 1) // kb

    def sel_step(i, carry):
        k0 = pl.multiple_of(i * kb, kb)
        kblk = ks_ref[pl.ds(k0, kb), :].astype(BF16)
        vblk = vs_ref[pl.ds(k0, kb), :].astype(BF16)
        kpos_i = k0 + lax.broadcasted_iota(jnp.int32, (1, kb), 1)
        ej = lax.broadcasted_iota(jnp.int32, (LANES, kb), 0)
        ec = lax.broadcasted_iota(jnp.int32, (LANES, kb), 1)
        expand = jnp.where(ej == (k0 + ec) // SEL_BLOCK, 1.0, 0.0).astype(BF16)
        allow = (jnp.dot(sel, expand, preferred_element_type=F32) > 0.5) & (kpos_i <= qpos_i)
        kpos = kpos_i.astype(F32)
        out = []
        for r in range(hpg):
            m_old, l_old, acc_old = carry[3 * r:3 * r + 3]
            s = jnp.where(allow, _dot_nt(qs[r], kblk) * ATT_SCALE + slopes[r] * kpos, NEG_INF)
            m_new = jnp.maximum(m_old, jnp.max(s, axis=-1, keepdims=True))
            alpha = jnp.exp(m_old - m_new)
            e = jnp.exp(s - m_new)
            out += [m_new, alpha * l_old + jnp.sum(e, axis=-1, keepdims=True),
                    alpha * acc_old + _dot(e, vblk)]
        return tuple(out)

    init = (jnp.full((tq, 1), NEG_INF, F32), jnp.zeros((tq, 1), F32), jnp.zeros((tq, HEAD_DIM), F32)) * hpg
    fin = lax.fori_loop(0, nkb, sel_step, init)
    o_slc = [fin[3 * r + 2] / fin[3 * r + 1] for r in range(hpg)]

    wk = WINDOW + tq
    ws = pl.multiple_of(jnp.clip(t0 - WINDOW, 0, t_len - wk), LANES)
    kw = kw_ref[pl.ds(ws, wk), :].astype(BF16)
    vw = vw_ref[pl.ds(ws, wk), :].astype(BF16)
    wpos_i = ws + lax.broadcasted_iota(jnp.int32, (1, wk), 1)
    wdist_i = qpos_i - wpos_i
    wallow = (wdist_i >= 0) & (wdist_i <= WINDOW)
    wpos = wpos_i.astype(F32)
    gates = _sigmoid(zg_ref[...])
    for r in range(hpg):
        s = jnp.where(wallow, _dot_nt(qs[r], kw) * ATT_SCALE + slopes[r] * wpos, NEG_INF)
        e = jnp.exp(s - jnp.max(s, axis=-1, keepdims=True))
        o_win = _dot(e, vw) / jnp.sum(e, axis=-1, keepdims=True)
        gate = [jnp.where(g == 0, gates[:, 3 * r + b:3 * r + b + 1],
                          gates[:, 3 * (r + hpg) + b:3 * (r + hpg) + b + 1]) for b in range(3)]
        o = gate[0] * o_cmp[r] + gate[1] * o_slc[r] + gate[2] * o_win
        o_ref[:, r * HEAD_DIM:(r + 1) * HEAD_DIM] = o.astype(o_ref.dtype)


def _nsa_prompt(q, zplain, kc, vc, kvrows, winrows, *, n, t, tq):
    nt = t // tq
    gw = HEADS_PER_GROUP * HEAD_DIM
    n_cmp_pad = kc.shape[1]
    g_blocks = NSA_KV_HEADS
    qrow = lambda i, g, j: (i * nt + j, g)
    seq = lambda blk: (lambda i, g, j: (i, blk + g))
    ov = _overlap_matrix(n_cmp_pad, -(-t // SEL_BLOCK), LANES)
    return pl.pallas_call(
        functools.partial(_nsa_prompt_body, t_len=t, tq=tq),
        grid=(n, NSA_KV_HEADS, nt),
        in_specs=[
            pl.BlockSpec((tq, gw), qrow),
            pl.BlockSpec((tq, LANES), lambda i, g, j: (i * nt + j, PLAIN_GATE_BLK)),
            pl.BlockSpec((None, n_cmp_pad, HEAD_DIM), lambda i, g, j: (i, 0, g)),
            pl.BlockSpec((None, n_cmp_pad, HEAD_DIM), lambda i, g, j: (i, 0, g)),
            pl.BlockSpec(ov.shape, lambda i, g, j: (0, 0)),
            pl.BlockSpec((t, HEAD_DIM), seq(2 * g_blocks)),
            pl.BlockSpec((t, HEAD_DIM), seq(3 * g_blocks)),
            pl.BlockSpec((t, HEAD_DIM), seq(0)),
            pl.BlockSpec((t, HEAD_DIM), seq(g_blocks)),
        ],
        out_specs=pl.BlockSpec((tq, gw), qrow),
        out_shape=jax.ShapeDtypeStruct((n * t, NSA_WIDTH), BF16),
        compiler_params=_cp(("parallel", "parallel", "arbitrary")),
        name="nsa_prompt",
    )(q, zplain, kc, vc, ov, kvrows, kvrows, winrows, winrows)


KV_OFF = NSA_WIDTH
GATE_OFF = KV_OFF + KV_WIDTH
PLAIN_OFF = GATE_OFF + GATE_WIDTH
SLOT_WIDTH = NSA_KV_HEADS * HEAD_DIM


def _w_in_split_body(w_ref, head_ref, plain_ref):
    x = w_ref[...]
    rows = x.shape[0]
    head_ref[...] = x[:, :GATE_OFF].astype(BF16)
    plain_ref[:, :PLAIN_WIDTH - LANES] = x[:, PLAIN_OFF:].astype(BF16)
    gate = jnp.concatenate([x[:, GATE_OFF:PLAIN_OFF], jnp.zeros((rows, LANES - GATE_WIDTH), F32)], axis=1)
    plain_ref[:, PLAIN_WIDTH - LANES:] = gate.astype(BF16)


def _w_in_split(w_in, *, tk):
    depth, k, width = w_in.shape
    return pl.pallas_call(
        _w_in_split_body,
        grid=(depth, k // tk),
        in_specs=[pl.BlockSpec((None, tk, width), lambda l, i: (l, i, 0))],
        out_specs=[pl.BlockSpec((None, tk, GATE_OFF), lambda l, i: (l, i, 0)),
                   pl.BlockSpec((None, tk, PLAIN_WIDTH), lambda l, i: (l, i, 0))],
        out_shape=[jax.ShapeDtypeStruct((depth, k, GATE_OFF), BF16),
                   jax.ShapeDtypeStruct((depth, k, PLAIN_WIDTH), BF16)],
        compiler_params=_cp(("parallel", "parallel")),
        name="w_in_split",
    )(w_in)


def _stacked_weights(w_in, cmp_w, w_branch, w_o, w_gu, w_down, w_ple_proj, w_ple_gate):
    w_head, w_plain = _w_in_split(w_in, tk=128)
    return dict(
        w_head=w_head,
        w_plain=w_plain,
        w_phi=cmp_w.astype(BF16),
        w_branch=w_branch.astype(BF16),
        w_o=w_o.astype(BF16),
        w_gu=w_gu.astype(BF16),
        w_down=w_down.astype(BF16),
        w_ple_proj=w_ple_proj.astype(BF16),
        w_ple_gate=w_ple_gate.astype(BF16),
    )


def _layer_weights(li, stacked, norms, qk_gain, cmp_pos, out_gain, lb_all):
    ones = jnp.ones((SLOT_WIDTH,), F32)
    zeros = jnp.zeros((SLOT_WIDTH,), F32)
    gain = lambda i: jnp.tile(qk_gain[li, i], NSA_KV_HEADS)
    row = lambda v: v.reshape(1, -1).astype(F32)
    cp = cmp_pos[li]
    half = lambda w, m: w[m * CMP_STRIDE:(m + 1) * CMP_STRIDE]
    pos = lambda m: jnp.concatenate([half(cp[0], m)] * NSA_KV_HEADS + [half(cp[1], m)] * NSA_KV_HEADS, axis=1)
    w = dict(stacked)
    w.update(
        li=li,
        norm=[row(norms[li, i]) for i in range(4)],
        q_gain=row(jnp.tile(qk_gain[li, 0], NSA_HEADS)),
        q_flag=jnp.ones((1, NSA_WIDTH), F32),
        kv_gain=row(jnp.concatenate([ones, ones, gain(1), ones])),
        kv_flag=row(jnp.concatenate([zeros, zeros, ones, zeros])),
        win_gain=row(jnp.concatenate([gain(2), ones])),
        win_flag=row(jnp.concatenate([ones, zeros])),
        cmp_p0=pos(0).astype(F32),
        cmp_p1=pos(1).astype(F32),
        w_phi_k=stacked["w_phi"][li, 0],
        w_phi_v=stacked["w_phi"][li, 1],
        kc_gain=row(qk_gain[li, 3]),
        ret_gain=row(out_gain[li, 0]),
        hgrn_gain=row(out_gain[li, 1]),
        lb=lb_all[li:li + 1],
    )
    return w


def _tiles(m):
    big = 512 if m % 512 == 0 else m
    small = 256 if m % 256 == 0 else m
    return big, small


def _projections(u, w, tm, row_bufs, cache=None):
    li, depth = w["li"], w["depth"]
    head = functools.partial(_proj, u, w["w_head"], li, tm=tm)
    q = head(col0=0, n=NSA_WIDTH, tn=512, out_dtype=BF16, gain=w["q_gain"], flag=w["q_flag"], name="proj_q")
    kvrows, kv_buf = head(col0=KV_OFF, n=4 * SLOT_WIDTH, tn=4 * SLOT_WIDTH, out_dtype=F32, gain=w["kv_gain"],
                          flag=w["kv_flag"], name="proj_kv", rows_out=(depth, row_bufs[0]))
    winrows, win_buf = head(col0=KV_OFF + 4 * SLOT_WIDTH, n=2 * SLOT_WIDTH, tn=2 * SLOT_WIDTH, out_dtype=F32,
                            gain=w["win_gain"], flag=w["win_flag"], name="proj_win", rows_out=(depth, row_bufs[1]))
    tn = 1152
    pages = None if cache is None else _cache_sum_pages(u.shape[0], tm, PLAIN_WIDTH, tn, cache[1])
    if pages is None:
        zplain = _proj(u, w["w_plain"], li, col0=0, n=PLAIN_WIDTH, tm=tm, tn=tn, out_dtype=F32, name="proj_plain")
        return q, kvrows, winrows, zplain, (kv_buf, win_buf), None
    zplain, a0, a1 = _proj_with_cache_sums(u, w["w_plain"], li, cache[0], cache[1], w["cmp_p0"], w["cmp_p1"],
                                           n=PLAIN_WIDTH, tm=tm, tn=tn, pages=pages)
    return q, kvrows, winrows, zplain, (kv_buf, win_buf), (a0, a1)


def _layer_tail(h, o_nsa, o_ret, o_hgrn, zplain, p, w, tm, ts):
    li = w["li"]
    h = _merge(o_nsa, o_ret, o_hgrn, zplain, h, w["w_branch"], w["w_o"], li, tm=ts)
    h, u = _ffn(h, w["norm"][2], w["w_gu"], w["w_down"], w["norm"][3], li, 1, tm=tm, tf=512)
    return _ple(h, u, p, w["w_ple_gate"], w["w_ple_proj"], li, tm=ts)


def _prompt_layer(x, p, w, row_bufs, *, n, t, cache=None):
    tm, ts = _tiles(n * t)
    h, u = _ffn(x, w["norm"][0], w["w_gu"], w["w_down"], w["norm"][1], w["li"], 0, tm=tm, tf=512)
    q, kvrows, winrows, zplain, row_bufs, cache_sums = _projections(
        u, w, 1024 if (n * t) % 1024 == 0 else tm, row_bufs, cache)
    a0, a1 = _cmp_partial(kvrows, w["cmp_p0"], w["cmp_p1"], rb=min(1024, t))
    ng = t // CMP_STRIDE
    kc, vc = _cmp_final(a0.reshape(n, ng, -1), a1.reshape(n, ng, -1), w["w_phi_k"], w["w_phi_v"], w["kc_gain"])
    o_nsa = _nsa_prompt(q, zplain, kc, vc, kvrows, winrows, n=n, t=t, tq=2 * LANES)
    o_ret, s_ret = _retention(zplain, w["ret_gain"], None, 0, n=n, t=t, c=LANES)
    o_hgrn, s_hgrn = _hgrn(zplain, w["lb"], w["hgrn_gain"], None, 0, n=n, t=t, c=LANES)
    h = _layer_tail(h, o_nsa, o_ret, o_hgrn, zplain, p, w, tm, ts)
    return h, row_bufs, s_ret, s_hgrn, cache_sums


def _nsa_dec_cmp_body(q_ref, kc_ref, vc_ref, ov_ref, ocmp_ref, sel_ref, *, past, steps, n_slc):
    n, gb = q_ref.shape[0], q_ref.shape[1]
    nc = kc_ref.shape[1]
    hpg = HEADS_PER_GROUP
    rows = hpg * steps
    ri = lax.broadcasted_iota(jnp.int32, (rows, 1), 0)
    qpos_i = past + ri % steps
    ci = lax.broadcasted_iota(jnp.int32, (rows, nc), 1)
    dist_i = qpos_i - (ci * CMP_STRIDE + CMP_BLOCK - 1)
    allow = (dist_i >= 0) & (ci < nc - 1)
    dist = dist_i.astype(F32)
    imps = []
    for i in range(n):
        for g in range(gb):
            slope = jnp.exp2(-(ri // steps + 1 + g * hpg).astype(F32))
            sl = slice(g * HEAD_DIM, (g + 1) * HEAD_DIM)
            p = _masked_softmax(_dot_nt(q_ref[i, g], kc_ref[i, :, sl]) * ATT_SCALE - slope * dist, allow)
            ocmp_ref[i, g] = _dot(p, vc_ref[i, :, sl])
            imp = p[0:steps]
            for r in range(1, hpg):
                imp = imp + p[r * steps:(r + 1) * steps]
            imps.append(imp)
    imp_all = jnp.concatenate(imps, axis=0)

    width = ov_ref.shape[1]
    all_rows = n * gb * steps
    qpos_all = past + lax.broadcasted_iota(jnp.int32, (all_rows, 1), 0) % steps
    score = _block_scores(imp_all, qpos_all, ov_ref[...], n_slc)
    sj = lax.broadcasted_iota(jnp.int32, score.shape, 1)
    lane = lax.broadcasted_iota(jnp.int32, (all_rows, LANES), 1)
    picked = jnp.full((all_rows, LANES), -1, jnp.int32)
    for it in range(min(N_SEL, n_slc)):
        m = jnp.max(score, axis=-1, keepdims=True)
        idx = jnp.min(jnp.where(score == m, sj, width), axis=-1, keepdims=True)
        picked = jnp.where(lane == it, jnp.where(m >= 0.0, idx, -1), picked)
        score = jnp.where(sj == idx, -3.0, score)
    sel_ref[...] = picked


def _nsa_dec_cmp(q_rt, kc, vc, *, past, steps, n_slc):
    n, ng = kc.shape[0], kc.shape[1]
    gb = NSA_KV_HEADS
    ov = _overlap_matrix(ng, n_slc, -(-n_slc // LANES) * LANES)
    o_cmp, sel = pl.pallas_call(
        functools.partial(_nsa_dec_cmp_body, past=past, steps=steps, n_slc=n_slc),
        out_shape=[
            jax.ShapeDtypeStruct((n, gb, HEADS_PER_GROUP * steps, HEAD_DIM), F32),
            jax.ShapeDtypeStruct((n * gb * steps, LANES), jnp.int32),
        ],
        compiler_params=pltpu.CompilerParams(vmem_limit_bytes=VMEM_LIMIT),
        name="nsa_dec_cmp",
    )(q_rt, kc, vc, ov)
    return o_cmp, sel.reshape(n, gb, steps, LANES)


ROW_PAD = 8


def _nsa_dec_attend_body(pt_ref, sel_ref, q_ref, gate_ref, ocmp_ref, kvn_ref, wn_ref, wp_ref, *refs,
                         past, steps, n_sel, past_blocks):
    gb = NSA_KV_HEADS
    blk_refs = refs[:gb * n_sel]
    o_ref = refs[gb * n_sel]
    i = pl.program_id(0)
    t = pl.program_id(1)
    qpos = past + t
    head = jnp.minimum(lax.broadcasted_iota(jnp.int32, (ROW_PAD, 1), 0), HEADS_PER_GROUP - 1)
    new_i = lax.broadcasted_iota(jnp.int32, (1, steps), 1)
    new_allow = new_i <= t
    new_dist = (t - new_i).astype(F32)
    cols = n_sel * SEL_BLOCK
    col = lax.broadcasted_iota(jnp.int32, (1, cols), 1)
    col_pick = col // SEL_BLOCK
    head_cols = lambda ref, blk: ref[:, blk * HEAD_DIM:(blk + 1) * HEAD_DIM]

    def two_part_attention(s_a, allow_a, v_a, s_b, allow_b, v_b):
        s_a = jnp.where(allow_a, s_a, NEG_INF)
        s_b = jnp.where(allow_b, s_b, NEG_INF)
        m = jnp.maximum(jnp.max(s_a, axis=-1, keepdims=True), jnp.max(s_b, axis=-1, keepdims=True))
        e_a = jnp.where(allow_a, jnp.exp(s_a - m), 0.0)
        e_b = jnp.where(allow_b, jnp.exp(s_b - m), 0.0)
        l = jnp.sum(e_a, axis=-1, keepdims=True) + jnp.sum(e_b, axis=-1, keepdims=True)
        return (_dot(e_a, v_a) + _dot(e_b, v_b)) / l

    for g in range(gb):
        q = q_ref[g]
        slope = jnp.exp2(-(head + 1 + g * HEADS_PER_GROUP).astype(F32))
        picked = [blk_refs[g * n_sel + k] for k in range(n_sel)]
        kall = jnp.concatenate([r[pl.ds(2 * gb + g, SEL_BLOCK, stride=CACHE_SUB), :] for r in picked], axis=0)
        vall = jnp.concatenate([r[pl.ds(3 * gb + g, SEL_BLOCK, stride=CACHE_SUB), :] for r in picked], axis=0)
        base = ((i * gb + g) * steps + t) * n_sel
        blk = jnp.zeros((1, cols), jnp.int32)
        for k in range(n_sel):
            blk = jnp.where(col_pick == k, sel_ref[base + k], blk)
        dist_i = qpos - (blk * SEL_BLOCK + col % SEL_BLOCK)
        allow = (blk >= 0) & (blk < past_blocks) & (dist_i >= 0)
        s_sel = _dot_nt(q, kall) * ATT_SCALE - slope * dist_i.astype(F32)
        s_new = _dot_nt(q, head_cols(kvn_ref, 2 * gb + g)) * ATT_SCALE - slope * new_dist
        o_slc = two_part_attention(s_sel, allow, vall, s_new, new_allow, head_cols(kvn_ref, 3 * gb + g))

        wb = wp_ref.shape[0] // (2 * gb)
        kwp = wp_ref[pl.ds(g, wb, stride=2 * gb), :]
        vwp = wp_ref[pl.ds(gb + g, wb, stride=2 * gb), :]
        wdist_i = qpos - (past - wb + lax.broadcasted_iota(jnp.int32, (1, wb), 1))
        wallow = (wdist_i >= 0) & (wdist_i <= WINDOW)
        s_wp = _dot_nt(q, kwp) * ATT_SCALE - slope * wdist_i.astype(F32)
        s_wn = _dot_nt(q, head_cols(wn_ref, g)) * ATT_SCALE - slope * new_dist
        o_win = two_part_attention(s_wp, wallow, vwp, s_wn, new_allow, head_cols(wn_ref, gb + g))

        gates = _sigmoid(gate_ref[g])
        o = gates[:, 0:1] * ocmp_ref[g] + gates[:, 1:2] * o_slc + gates[:, 2:3] * o_win
        o_ref[g] = o.astype(o_ref.dtype)


def _nsa_dec_attend(page_table, sel_flat, q_tr, gates_tr, ocmp_tr, kvrows, winrows, cache_half, win_rows_view, li, *,
                    past, steps, n_sel):
    n = q_tr.shape[0]
    rows = steps * ROW_PAD
    gb = NSA_KV_HEADS
    halves = cache_half.shape[2]
    past_blocks = past // SEL_BLOCK
    step = lambda i, t, pt, sel: (i, 0, t, 0)
    new = lambda i, t, pt, sel: (i, 0)

    def gather(g, k):
        def index(i, t, pt, sel):
            j = jnp.clip(sel[((i * gb + g) * steps + t) * n_sel + k], 0, past_blocks - 1)
            return (li, pt[i, j // halves], j % halves, 0, 0)
        return pl.BlockSpec((None, None, None, SEL_BLOCK * CACHE_SUB, HEAD_DIM), index)

    in_specs = [
        pl.BlockSpec((None, gb, ROW_PAD, HEAD_DIM), step),
        pl.BlockSpec((None, gb, ROW_PAD, LANES), step),
        pl.BlockSpec((None, gb, ROW_PAD, HEAD_DIM), step),
        pl.BlockSpec((steps, kvrows.shape[1]), new),
        pl.BlockSpec((steps, winrows.shape[1]), new),
        pl.BlockSpec((None, None, win_rows_view.shape[2], HEAD_DIM), lambda i, t, pt, sel: (li, i, 0, 0)),
    ]
    in_specs += [gather(g, k) for g in range(gb) for k in range(n_sel)]
    return pl.pallas_call(
        functools.partial(_nsa_dec_attend_body, past=past, steps=steps, n_sel=n_sel, past_blocks=past_blocks),
        grid_spec=pltpu.PrefetchScalarGridSpec(
            num_scalar_prefetch=2,
            grid=(n, steps),
            in_specs=in_specs,
            out_specs=pl.BlockSpec((None, gb, ROW_PAD, HEAD_DIM), step),
        ),
        out_shape=jax.ShapeDtypeStruct((n, gb, rows, HEAD_DIM), BF16),
        compiler_params=_cp(("parallel", "arbitrary")),
        name="nsa_dec_attend",
    )(page_table, sel_flat, q_tr, gates_tr, ocmp_tr, kvrows, winrows, win_rows_view, *([cache_half] * (gb * n_sel)))


def _decode_layer(x, p, w, row_bufs, cache_rows, cache_half, win_rows_view, state_ret, state_hgrn, page_table,
                  cache_sums, *, n, steps):
    m = n * steps
    li = w["li"]
    hpg, gb = HEADS_PER_GROUP, NSA_KV_HEADS
    past = page_table.shape[1] * (cache_rows.shape[2] // CACHE_SUB)
    n_slc = -(-(past + steps) // SEL_BLOCK)
    n_sel = min(N_SEL, n_slc)
    h, u = _ffn(x, w["norm"][0], w["w_gu"], w["w_down"], w["norm"][1], li, 0, tm=m, tf=512)
    q, kvrows, winrows, zplain, row_bufs, _ = _projections(u, w, m, row_bufs)
    if cache_sums is None:
        cache_sums = _cmp_partial_paged(cache_rows, page_table, w["cmp_p0"], w["cmp_p1"], li, pages=8)
    kc, vc = _cmp_final(*cache_sums, w["w_phi_k"], w["w_phi_v"], w["kc_gain"])
    q5 = q.reshape(n, steps, gb, hpg, HEAD_DIM)
    q_rt = q5.transpose(0, 2, 3, 1, 4).reshape(n, gb, hpg * steps, HEAD_DIM)
    o_cmp, sel = _nsa_dec_cmp(q_rt, kc, vc, past=past, steps=steps, n_slc=n_slc)
    pad_heads = lambda a: jnp.pad(a, ((0, 0), (0, 0), (0, 0), (0, ROW_PAD - hpg), (0, 0)))
    rows = steps * ROW_PAD
    q_tr = pad_heads(q5.transpose(0, 2, 1, 3, 4)).reshape(n, gb, rows, HEAD_DIM)
    ocmp_tr = pad_heads(o_cmp.reshape(n, gb, hpg, steps, HEAD_DIM).transpose(0, 1, 3, 2, 4)).reshape(
        n, gb, rows, HEAD_DIM)
    zg = zplain[:, PLAIN_GATE_BLK * LANES:PLAIN_GATE_BLK * LANES + GATE_WIDTH].reshape(n, steps, gb, hpg, 3)
    gates_tr = pad_heads(zg.transpose(0, 2, 1, 3, 4)).reshape(n, gb, rows, 3)
    gates_tr = jnp.pad(gates_tr, ((0, 0), (0, 0), (0, 0), (0, LANES - 3)))
    sel_flat = sel[..., :n_sel].reshape(-1)
    o_tr = _nsa_dec_attend(page_table, sel_flat, q_tr, gates_tr, ocmp_tr, kvrows, winrows, cache_half, win_rows_view,
                           li, past=past, steps=steps, n_sel=n_sel)
    o_nsa = o_tr.reshape(n, gb, steps, ROW_PAD, HEAD_DIM)[:, :, :, :hpg].transpose(0, 2, 1, 3, 4).reshape(
        m, NSA_WIDTH)
    o_ret, s_ret = _retention(zplain, w["ret_gain"], state_ret, li, n=n, t=steps, c=steps)
    o_hgrn, s_hgrn = _hgrn(zplain, w["lb"], w["hgrn_gain"], state_hgrn, li, n=n, t=steps, c=steps)
    h = _layer_tail(h, o_nsa, o_ret, o_hgrn, zplain, p, w, m, m)
    return h, row_bufs, s_ret, s_hgrn


def kernel(x_prompt, x_sample, cache_kv, cache_win, state_ret, state_hgrn, page_table, p_prompt, p_sample, norms,
           w_in, qk_gain, cmp_pos, cmp_w, out_gain, hgrn_lb, w_branch, w_o, w_gu, w_down, w_ple_proj, w_ple_gate):
    n_p, t_p, d = x_prompt.shape
    n_d, t_d, _ = x_sample.shape
    depth, n_pool, page = cache_kv.shape[:3]
    cache_rows = cache_kv.reshape(depth, n_pool, page * CACHE_SUB, HEAD_DIM)
    cache_half = cache_kv.reshape(depth, n_pool, page // SEL_BLOCK, SEL_BLOCK * CACHE_SUB, HEAD_DIM)
    wb = cache_win.shape[2]
    win_rows_view = cache_win.reshape(depth, n_d, wb * 2 * NSA_KV_HEADS, HEAD_DIM)
    lb_all = _hgrn_lower_bounds(hgrn_lb)
    stacked = _stacked_weights(w_in, cmp_w, w_branch, w_o, w_gu, w_down, w_ple_proj, w_ple_gate)
    hp = x_prompt.reshape(n_p * t_p, d)
    hs = x_sample.reshape(n_d * t_d, d)
    bufs_p = bufs_s = (None, None)
    states = [[] for _ in range(4)]
    for li in range(depth):
        w = _layer_weights(li, stacked, norms, qk_gain, cmp_pos, out_gain, lb_all)
        w["depth"] = depth
        hp, bufs_p, s_r, s_h, cache_sums = _prompt_layer(hp, p_prompt[li].reshape(n_p * t_p, -1), w, bufs_p,
                                                         n=n_p, t=t_p, cache=(cache_rows, page_table))
        states[0].append(s_r)
        states[2].append(s_h)
        hs, bufs_s, s_r, s_h = _decode_layer(hs, p_sample[li].reshape(n_d * t_d, -1), w, bufs_s, cache_rows,
                                             cache_half, win_rows_view, state_ret, state_hgrn, page_table,
                                             cache_sums, n=n_d, steps=t_d)
        states[1].append(s_r)
        states[3].append(s_h)
    gb = NSA_KV_HEADS
    kv_p = bufs_p[0].reshape(depth, n_p, t_p, 4, gb, HEAD_DIM)
    kv_s = bufs_s[0].reshape(depth, n_d, t_d, 4, gb, HEAD_DIM)
    win_p = bufs_p[1].reshape(depth, n_p, t_p, 2, gb, HEAD_DIM)[:, :, t_p - min(WINDOW, t_p):]
    win_all = jnp.concatenate([cache_win, bufs_s[1].reshape(depth, n_d, t_d, 2, gb, HEAD_DIM)], axis=2)
    win_s = win_all[:, :, wb + t_d - min(WINDOW, wb + t_d):]
    ret_p, ret_s, hg_p, hg_s = (jnp.stack(s) for s in states)
    return (hp.reshape(n_p, t_p, d), hs.reshape(n_d, t_d, d), kv_p, kv_s, win_p, win_s, ret_p, ret_s, hg_p, hg_s)
```

```python
import functools
import math

import jax
import jax.numpy as jnp
import numpy as np
from jax import lax
from jax.experimental import pallas as pl
from jax.experimental.pallas import tpu as pltpu

F32 = jnp.float32
BF16 = jnp.bfloat16

D_MODEL = 2048
HEAD_DIM = 128
NSA_HEADS = 8
NSA_KV_HEADS = 2
HEADS_PER_GROUP = NSA_HEADS // NSA_KV_HEADS
CMP_BLOCK = 32
CMP_STRIDE = 16
SEL_BLOCK = 64
N_SEL = 16
WINDOW = 512
RET_HEADS = 4
HGRN_HEADS = 4
D_FF = 5632
PLE_DIM = 256
NSA_WIDTH = NSA_HEADS * HEAD_DIM
RET_WIDTH = RET_HEADS * HEAD_DIM
HGRN_WIDTH = HGRN_HEADS * HEAD_DIM
KV_WIDTH = 6 * NSA_KV_HEADS * HEAD_DIM
GATE_WIDTH = 3 * NSA_HEADS
NEG_INF = -1e30
FORCE_SCORE = 1e6
EPS = 1e-6
ATT_SCALE = HEAD_DIM ** -0.5
LOG2E = math.log2(math.e)

LANES = 128
PLAIN_RET_BLK = 0
PLAIN_HGRN_BLK = (4 * RET_WIDTH) // LANES
PLAIN_MERGE_OFF = 4 * RET_WIDTH + 4 * HGRN_WIDTH
PLAIN_GATE_BLK = (PLAIN_MERGE_OFF + 3 * D_MODEL) // LANES
PLAIN_WIDTH = PLAIN_MERGE_OFF + 3 * D_MODEL + LANES

VMEM_LIMIT = 56 * 1024 * 1024


def _cp(sem, vmem=VMEM_LIMIT):
    return pltpu.CompilerParams(dimension_semantics=sem, vmem_limit_bytes=vmem)


def _rms(x):
    return x * lax.rsqrt(jnp.mean(x * x, axis=-1, keepdims=True) + EPS)


def _dot(a, b):
    return jnp.dot(a.astype(BF16), b.astype(BF16), preferred_element_type=F32)


def _dot_nt(a, b):
    return lax.dot_general(a.astype(BF16), b.astype(BF16), (((1,), (1,)), ((), ())),
                           preferred_element_type=F32)


def _dot_tn(a, b):
    rows = a.shape[0]
    if rows % LANES:
        pad = LANES - rows % LANES
        a = jnp.concatenate([a, jnp.zeros((pad, a.shape[1]), a.dtype)], axis=0)
        b = jnp.concatenate([b, jnp.zeros((pad, b.shape[1]), b.dtype)], axis=0)
    return _dot(a.T, b)


def _split_bf16(x):
    hi = x.astype(BF16)
    lo = (x - hi.astype(F32)).astype(BF16)
    return hi, lo


def _sigmoid(x):
    return 1.0 / (1.0 + jnp.exp(-x))


def _ffn_body(x_ref, g1_ref, wg_ref, wv_ref, wd_ref, g2_ref, h_ref, u_ref, xn_sc, acc_sc, *, nf):
    j = pl.program_id(1)

    @pl.when(j == 0)
    def _():
        xn_sc[...] = (_rms(x_ref[...]) * g1_ref[...]).astype(BF16)
        acc_sc[...] = jnp.zeros_like(acc_sc)

    xn = xn_sc[...]
    g = jnp.dot(xn, wg_ref[...], preferred_element_type=F32)
    v = jnp.dot(xn, wv_ref[...], preferred_element_type=F32)
    a = (g * _sigmoid(g) * v).astype(BF16)
    acc_sc[...] += jnp.dot(a, wd_ref[...], preferred_element_type=F32)

    @pl.when(j == nf - 1)
    def _():
        h = x_ref[...] + 0.5 * acc_sc[...]
        h_ref[...] = h
        u_ref[...] = (_rms(h) * g2_ref[...]).astype(BF16)


def _ffn(x, g1, w_gu, w_down, g2, li, which, *, tm, tf):
    m, d = x.shape
    f = w_down.shape[2]
    nf = f // tf
    return pl.pallas_call(
        functools.partial(_ffn_body, nf=nf),
        grid=(m // tm, nf),
        in_specs=[
            pl.BlockSpec((tm, d), lambda i, j: (i, 0)),
            pl.BlockSpec((1, d), lambda i, j: (0, 0)),
            pl.BlockSpec((None, None, d, tf), lambda i, j: (li, which, 0, j)),
            pl.BlockSpec((None, None, d, tf), lambda i, j: (li, which, 0, j + nf)),
            pl.BlockSpec((None, None, tf, d), lambda i, j: (li, which, j, 0)),
            pl.BlockSpec((1, d), lambda i, j: (0, 0)),
        ],
        out_specs=[
            pl.BlockSpec((tm, d), lambda i, j: (i, 0)),
            pl.BlockSpec((tm, d), lambda i, j: (i, 0)),
        ],
        out_shape=[jax.ShapeDtypeStruct((m, d), F32), jax.ShapeDtypeStruct((m, d), BF16)],
        scratch_shapes=[pltpu.VMEM((tm, d), BF16), pltpu.VMEM((tm, d), F32)],
        compiler_params=_cp(("parallel", "arbitrary")),
        name="ffn",
    )(x, g1, w_gu, w_gu, w_down, g2)


def _proj_plain_body(u_ref, w_ref, o_ref):
    o_ref[...] = _dot_nt(u_ref[...], w_ref[...]).astype(o_ref.dtype)


def _proj_norm_body(u_ref, w_ref, gain_ref, flag_ref, *refs, tn, rows_out):
    o_ref = refs[-2] if rows_out else refs[-1]
    z = _dot_nt(u_ref[...], w_ref[...])
    chunks = tn // LANES
    for c in range(chunks):
        sl = slice(c * LANES, (c + 1) * LANES)
        zc = z[:, sl]
        normed = _rms(zc) * gain_ref[:, sl]
        val = jnp.where(flag_ref[:, sl] > 0.5, normed, zc)
        o_ref[:, sl] = val.astype(o_ref.dtype)
        if rows_out:
            refs[-1][pl.ds(c, z.shape[0], stride=chunks), :] = val


def _proj(u, w, li, *, col0, n, tm, tn, out_dtype, gain=None, flag=None, name, rows_out=None):
    m, k = u.shape
    cb = col0 // tn
    in_specs = [pl.BlockSpec((tm, k), lambda i, j: (i, 0)), pl.BlockSpec((None, tn, k), lambda i, j: (li, cb + j, 0))]
    args = [u, w]
    out_specs = pl.BlockSpec((tm, tn), lambda i, j: (i, j))
    out_shape = jax.ShapeDtypeStruct((m, n), out_dtype)
    aliases = {}
    if gain is None:
        body = _proj_plain_body
    else:
        body = functools.partial(_proj_norm_body, tn=tn, rows_out=rows_out is not None)
        in_specs += [pl.BlockSpec((1, tn), lambda i, j: (0, j)), pl.BlockSpec((1, tn), lambda i, j: (0, j))]
        args += [gain, flag]
    if rows_out is not None:
        assert n == tn, "the row-major store needs the whole width in one column tile"
        depth, prev = rows_out
        chunks = n // LANES
        out_specs = [out_specs, pl.BlockSpec((None, tm * chunks, LANES), lambda i, j: (li, i, 0))]
        out_shape = [out_shape, jax.ShapeDtypeStruct((depth, m * chunks, LANES), F32)]
        if prev is not None:
            in_specs.append(pl.BlockSpec(memory_space=pl.ANY))
            args.append(prev)
            aliases = {len(args) - 1: 1}
    return pl.pallas_call(
        body,
        grid=(m // tm, n // tn),
        in_specs=in_specs,
        out_specs=out_specs,
        out_shape=out_shape,
        input_output_aliases=aliases,
        compiler_params=_cp(("parallel", "arbitrary")),
        name=name,
    )(*args)


def _merge_body(oa_ref, ob_ref, oc_ref, m0_ref, m1_ref, m2_ref, h_ref, wb_ref, wo_ref, out_ref):
    ya = jnp.dot(oa_ref[...], wb_ref[0:NSA_WIDTH, :], preferred_element_type=F32)
    yb = jnp.dot(ob_ref[...], wb_ref[NSA_WIDTH:NSA_WIDTH + RET_WIDTH, :], preferred_element_type=F32)
    yc = jnp.dot(oc_ref[...], wb_ref[NSA_WIDTH + RET_WIDTH:, :], preferred_element_type=F32)
    mixed = _sigmoid(m0_ref[...]) * ya + _sigmoid(m1_ref[...]) * yb + _sigmoid(m2_ref[...]) * yc
    out_ref[...] = h_ref[...] + jnp.dot(mixed.astype(BF16), wo_ref[...], preferred_element_type=F32)


def _merge(o_nsa, o_ret, o_hgrn, zplain, h, w_branch, w_o, li, *, tm):
    m, d = h.shape
    mb = PLAIN_MERGE_OFF // d
    row = lambda i: (i, 0)
    layer = lambda i: (li, 0, 0)
    return pl.pallas_call(
        _merge_body,
        grid=(m // tm,),
        in_specs=[
            pl.BlockSpec((tm, NSA_WIDTH), row),
            pl.BlockSpec((tm, RET_WIDTH), row),
            pl.BlockSpec((tm, HGRN_WIDTH), row),
            pl.BlockSpec((tm, d), lambda i: (i, mb)),
            pl.BlockSpec((tm, d), lambda i: (i, mb + 1)),
            pl.BlockSpec((tm, d), lambda i: (i, mb + 2)),
            pl.BlockSpec((tm, d), row),
            pl.BlockSpec((None,) + w_branch.shape[1:], layer),
            pl.BlockSpec((None,) + w_o.shape[1:], layer),
        ],
        out_specs=pl.BlockSpec((tm, d), row),
        out_shape=jax.ShapeDtypeStruct((m, d), F32),
        compiler_params=_cp(("parallel",)),
        name="merge",
    )(o_nsa, o_ret, o_hgrn, zplain, zplain, zplain, h, w_branch, w_o)


def _ple_body(h_ref, u_ref, p_ref, wg_ref, wp_ref, out_ref):
    gate = _sigmoid(jnp.dot(u_ref[...], wg_ref[...], preferred_element_type=F32))
    proj = jnp.dot(p_ref[...].astype(BF16), wp_ref[...], preferred_element_type=F32)
    out_ref[...] = h_ref[...] + gate * proj


def _ple(h, u, p, w_gate, w_proj, li, *, tm):
    m, d = h.shape
    row = lambda i: (i, 0)
    layer = lambda i: (li, 0, 0)
    return pl.pallas_call(
        _ple_body,
        grid=(m // tm,),
        in_specs=[
            pl.BlockSpec((tm, d), row),
            pl.BlockSpec((tm, d), row),
            pl.BlockSpec((tm, p.shape[1]), row),
            pl.BlockSpec((None,) + w_gate.shape[1:], layer),
            pl.BlockSpec((None,) + w_proj.shape[1:], layer),
        ],
        out_specs=pl.BlockSpec((tm, d), row),
        out_shape=jax.ShapeDtypeStruct((m, d), F32),
        compiler_params=_cp(("parallel",)),
        name="ple",
    )(h, u, p, w_gate, w_proj)


def _retention_tables(c):
    lg = np.log1p(-np.exp2(-5.0 - np.arange(RET_HEADS, dtype=np.float64)))
    i = np.arange(c, dtype=np.float64)
    diff = i[:, None] - i[None, :]
    dmat = np.where(diff >= 0, np.exp(lg[:, None, None] * np.maximum(diff, 0.0)), 0.0)
    q_dec = np.exp(lg[:, None] * (i + 1.0))[..., None] * np.ones((1, 1, HEAD_DIM))
    k_dec = np.exp(lg[:, None] * (c - 1.0 - i))[..., None] * np.ones((1, 1, HEAD_DIM))
    c_dec = np.exp(lg * c)[:, None, None] * np.ones((1, 8, HEAD_DIM))
    return tuple(jnp.asarray(a, F32) for a in (dmat, q_dec, k_dec, c_dec))


def _retention_body(*refs, has_state, nch, c):
    if has_state:
        q_ref, k_ref, v_ref, g_ref, dm_ref, qd_ref, kd_ref, cd_ref, gain_ref, s0_ref, o_ref, s_ref = refs
    else:
        q_ref, k_ref, v_ref, g_ref, dm_ref, qd_ref, kd_ref, cd_ref, gain_ref, o_ref, s_ref = refs
    heads = range(RET_HEADS)
    cols = lambda hh: slice(hh * HEAD_DIM, (hh + 1) * HEAD_DIM)

    def chunk(ci, states):
        rows = pl.ds(pl.multiple_of(ci * c, c), c)
        out = []
        for hh in heads:
            q = q_ref[rows, cols(hh)]
            k = k_ref[rows, cols(hh)] * ATT_SCALE
            v = v_ref[rows, cols(hh)]
            a = _dot_nt(q, k) * dm_ref[hh]
            o = _dot(a, v) + _dot(q * qd_ref[hh], states[hh])
            g = g_ref[rows, cols(hh)]
            o_ref[rows, cols(hh)] = (_rms(o) * gain_ref[:, cols(hh)] * (g * _sigmoid(g))).astype(o_ref.dtype)
            out.append(states[hh] * cd_ref[hh, 0:1, :] + _dot_tn(k * kd_ref[hh], v))
        return tuple(out)

    init = tuple(s0_ref[hh] if has_state else jnp.zeros((LANES, LANES), F32) for hh in heads)
    final = lax.fori_loop(0, nch, chunk, init)
    for hh in heads:
        s_ref[hh] = final[hh]


def _retention(zplain, gain, s0, li, *, n, t, c):
    nch = t // c
    tables = _retention_tables(c)
    hb = RET_HEADS
    width = hb * HEAD_DIM
    first = PLAIN_RET_BLK * LANES // width
    in_specs = [pl.BlockSpec((t, width), (lambda w: (lambda i: (i, first + w)))(w)) for w in range(4)]
    in_specs += [pl.BlockSpec(tab.shape, lambda i: (0, 0, 0)) for tab in tables]
    in_specs.append(pl.BlockSpec((1, width), lambda i: (0, 0)))
    args = [zplain] * 4 + list(tables) + [gain]
    if s0 is not None:
        in_specs.append(pl.BlockSpec((None, None, hb, LANES, LANES), lambda i: (li, i, 0, 0, 0)))
        args.append(s0)
    return pl.pallas_call(
        functools.partial(_retention_body, has_state=s0 is not None, nch=nch, c=c),
        grid=(n,),
        in_specs=in_specs,
        out_specs=[
            pl.BlockSpec((t, width), lambda i: (i, 0)),
            pl.BlockSpec((None, hb, LANES, LANES), lambda i: (i, 0, 0, 0)),
        ],
        out_shape=[
            jax.ShapeDtypeStruct((n * t, RET_WIDTH), BF16),
            jax.ShapeDtypeStruct((n, hb, LANES, LANES), F32),
        ],
        compiler_params=_cp(("parallel",)),
        name="retention",
    )(*args)


HGRN_SUB = 16


def _hgrn_body(*refs, has_state, nch, c):
    if has_state:
        q_ref, f_ref, v_ref, g_ref, lb_ref, gain_ref, s0_ref, o_ref, s_ref, st_sc, b_sc, k_sc, v_sc, o_sc = refs
    else:
        q_ref, f_ref, v_ref, g_ref, lb_ref, gain_ref, o_ref, s_ref, st_sc, b_sc, k_sc, v_sc, o_sc = refs
    ci = pl.program_id(1)
    heads = range(HGRN_HEADS)
    cols = lambda hh: slice(hh * HEAD_DIM, (hh + 1) * HEAD_DIM)

    @pl.when(ci == 0)
    def _():
        for hh in heads:
            st_sc[hh] = s0_ref[hh].T if has_state else jnp.zeros((LANES, LANES), F32)

    ri = lax.broadcasted_iota(jnp.int32, (c, c), 0)
    si = lax.broadcasted_iota(jnp.int32, (c, c), 1)
    tri = jnp.where(ri >= si, 1.0, 0.0).astype(BF16)
    q, kk, v, b, st = [], [], [], [], []
    for hh in heads:
        lb = lb_ref[:, cols(hh)]
        f = lb + (1.0 - lb) * _sigmoid(f_ref[:, cols(hh)])
        logf = jnp.log(f)
        hi = logf.astype(BF16)
        r1 = logf - hi.astype(F32)
        mid = r1.astype(BF16)
        lo = (r1 - mid.astype(F32)).astype(BF16)
        b.append(jnp.dot(tri, hi, preferred_element_type=F32) + jnp.dot(tri, mid, preferred_element_type=F32)
                 + jnp.dot(tri, lo, preferred_element_type=F32))
        q.append(q_ref[:, cols(hh)])
        kk.append(1.0 - f)
        v.append(v_ref[:, cols(hh)])
        b_sc[hh] = b[hh]
        k_sc[hh] = kk[hh]
        v_sc[hh] = v[hh]
        st.append(st_sc[hh])
        o_sc[hh] = _dot_nt(q[hh] * jnp.exp(b[hh]), st[hh])

    sub = min(HGRN_SUB, c)
    row_id = lax.broadcasted_iota(jnp.int32, (sub, LANES), 0)
    for blk in range(c // sub):
        r0 = blk * sub
        b_i = [b[hh][r0:r0 + sub] for hh in heads]
        q_i = [q[hh][r0:r0 + sub] for hh in heads]
        acc = [jnp.zeros((sub, LANES), F32) for _ in heads]
        if blk > 0:
            for hh in heads:
                ref = b_sc[hh, r0 - 1:r0, :]
                qt = q_i[hh] * jnp.exp(b_i[hh] - ref)
                kt = kk[hh][0:r0] * jnp.exp(ref - b[hh][0:r0])
                acc[hh] = _dot(_dot_nt(qt, kt), v[hh][0:r0])
        for s in range(sub):
            for hh in heads:
                b_s = b_sc[hh, r0 + s:r0 + s + 1, :]
                k_s = k_sc[hh, r0 + s:r0 + s + 1, :]
                v_s = v_sc[hh, r0 + s:r0 + s + 1, :]
                w = q_i[hh] * jnp.exp(jnp.minimum(b_i[hh] - b_s, 0.0)) * k_s
                w = jnp.where(row_id >= s, w, 0.0)
                acc[hh] = acc[hh] + jnp.sum(w, axis=-1, keepdims=True) * v_s
        for hh in heads:
            o_sc[hh, r0:r0 + sub, :] += acc[hh]

    for hh in heads:
        b_last = b_sc[hh, c - 1:c, :]
        st_sc[hh] = st[hh] * jnp.exp(b_last) + _dot_tn(v[hh], kk[hh] * jnp.exp(b_last - b[hh]))
        g = g_ref[:, cols(hh)]
        o_ref[:, cols(hh)] = (_rms(o_sc[hh]) * gain_ref[:, cols(hh)] * (g * _sigmoid(g))).astype(o_ref.dtype)

    @pl.when(ci == nch - 1)
    def _():
        for hh in heads:
            s_ref[hh] = st_sc[hh].T


def _hgrn(zplain, lb, gain, s0, li, *, n, t, c):
    nch = t // c
    hb = HGRN_HEADS
    width = hb * HEAD_DIM
    first = PLAIN_HGRN_BLK * LANES // width
    row = lambda w: (lambda i, j: (i * nch + j, first + w))
    const = lambda i, j: (0, 0)
    in_specs = [pl.BlockSpec((c, width), row(w)) for w in range(4)]
    in_specs += [pl.BlockSpec((1, width), const), pl.BlockSpec((1, width), const)]
    args = [zplain] * 4 + [lb, gain]
    if s0 is not None:
        in_specs.append(pl.BlockSpec((None, None, hb, LANES, LANES), lambda i, j: (li, i, 0, 0, 0)))
        args.append(s0)
    return pl.pallas_call(
        functools.partial(_hgrn_body, has_state=s0 is not None, nch=nch, c=c),
        grid=(n, nch),
        in_specs=in_specs,
        out_specs=[
            pl.BlockSpec((c, width), lambda i, j: (i * nch + j, 0)),
            pl.BlockSpec((None, hb, LANES, LANES), lambda i, j: (i, 0, 0, 0)),
        ],
        out_shape=[
            jax.ShapeDtypeStruct((n * t, HGRN_WIDTH), BF16),
            jax.ShapeDtypeStruct((n, hb, LANES, LANES), F32),
        ],
        scratch_shapes=[pltpu.VMEM((hb, LANES, LANES), F32)] + [pltpu.VMEM((hb, c, LANES), F32)] * 4,
        compiler_params=_cp(("parallel", "arbitrary")),
        name="hgrn",
    )(*args)


def _lb_body(x_ref, o_ref):
    x = x_ref[...]
    e = jnp.exp(x - jnp.max(x, axis=0, keepdims=True))
    sm = e / jnp.sum(e, axis=0, keepdims=True)
    acc = jnp.zeros_like(sm[0:1])
    o_ref[0:1, :] = acc
    for layer in range(1, x.shape[0]):
        acc = acc + sm[layer:layer + 1]
        o_ref[layer:layer + 1, :] = acc


def _hgrn_lower_bounds(hgrn_lb):
    return pl.pallas_call(
        _lb_body,
        out_shape=jax.ShapeDtypeStruct(hgrn_lb.shape, F32),
        name="hgrn_lb",
    )(hgrn_lb.astype(F32))


def _group_sums(x, p0, p1):
    rows, width = x.shape
    xg = x.reshape(rows // CMP_STRIDE, CMP_STRIDE, width)
    return jnp.sum(xg * p0[None], axis=1), jnp.sum(xg * p1[None], axis=1)


def _cmp_partial_body(x_ref, p0_ref, p1_ref, a0_ref, a1_ref):
    a0, a1 = _group_sums(x_ref[...], p0_ref[...], p1_ref[...])
    a0_ref[...] = a0
    a1_ref[...] = a1


def _cmp_partial(kvrows, p0, p1, *, rb):
    m = kvrows.shape[0]
    w = p0.shape[1]
    const = lambda i: (0, 0)
    out = jax.ShapeDtypeStruct((m // CMP_STRIDE, w), F32)
    return pl.pallas_call(
        _cmp_partial_body,
        grid=(m // rb,),
        in_specs=[pl.BlockSpec((rb, w), lambda i: (i, 0)), pl.BlockSpec(p0.shape, const), pl.BlockSpec(p1.shape, const)],
        out_specs=[pl.BlockSpec((rb // CMP_STRIDE, w), lambda i: (i, 0))] * 2,
        out_shape=[out, out],
        compiler_params=_cp(("parallel",)),
        name="cmp_partial",
    )(kvrows, p0, p1)


CACHE_SUB = 4 * NSA_KV_HEADS


def _page_group_sums(x_refs, p0_ref, p1_ref, a0_ref, a1_ref, page):
    gp = page // CMP_STRIDE
    for k, x_ref in enumerate(x_refs):
        for s in range(2 * NSA_KV_HEADS):
            sl = slice(s * HEAD_DIM, (s + 1) * HEAD_DIM)
            x = x_ref[pl.ds(s, page, stride=CACHE_SUB), :]
            a0, a1 = _group_sums(x, p0_ref[:, sl], p1_ref[:, sl])
            a0_ref[k * gp:(k + 1) * gp, sl] = a0
            a1_ref[k * gp:(k + 1) * gp, sl] = a1


def _cmp_partial_paged_body(pt_ref, *refs, pages, page):
    p0_ref, p1_ref, a0_ref, a1_ref = refs[pages:]
    _page_group_sums(refs[:pages], p0_ref, p1_ref, a0_ref, a1_ref, page)


def _proj_cache_sums_body(pt_ref, u_ref, w_ref, *refs, pages, page):
    p0_ref, p1_ref, o_ref, a0_ref, a1_ref = refs[pages:]
    o_ref[...] = _dot_nt(u_ref[...], w_ref[...])
    _page_group_sums(refs[:pages], p0_ref, p1_ref, a0_ref, a1_ref, page)


def _proj_with_cache_sums(u, w, li, cache_rows, page_table, p0, p1, *, n, tm, tn, pages):
    m, k = u.shape
    nd, n_pages = page_table.shape
    page = cache_rows.shape[2] // CACHE_SUB
    width = p0.shape[1]
    gp = page // CMP_STRIDE
    chunks = n_pages // pages
    units = nd * chunks
    ni, nj = m // tm, n // tn
    unit = lambda i, j: jnp.minimum(i * nj + j, units - 1)
    const = lambda i, j, pt: (0, 0)
    page_spec = lambda kk: pl.BlockSpec(
        (None, None, page * CACHE_SUB, HEAD_DIM),
        lambda i, j, pt: (li, pt[unit(i, j) // chunks, (unit(i, j) % chunks) * pages + kk], 0, 0))
    sums_spec = pl.BlockSpec((None, pages * gp, width), lambda i, j, pt: (unit(i, j) // chunks, unit(i, j) % chunks, 0))
    sums = jax.ShapeDtypeStruct((nd, n_pages * gp, width), F32)
    return pl.pallas_call(
        functools.partial(_proj_cache_sums_body, pages=pages, page=page),
        grid_spec=pltpu.PrefetchScalarGridSpec(
            num_scalar_prefetch=1,
            grid=(ni, nj),
            in_specs=[pl.BlockSpec((tm, k), lambda i, j, pt: (i, 0)),
                      pl.BlockSpec((None, tn, k), lambda i, j, pt: (li, j, 0))]
            + [page_spec(kk) for kk in range(pages)]
            + [pl.BlockSpec(p0.shape, const), pl.BlockSpec(p1.shape, const)],
            out_specs=[pl.BlockSpec((tm, tn), lambda i, j, pt: (i, j)), sums_spec, sums_spec],
        ),
        out_shape=[jax.ShapeDtypeStruct((m, n), F32), sums, sums],
        compiler_params=_cp(("arbitrary", "arbitrary")),
        name="proj_plain_cache_sums",
    )(page_table, u, w, *([cache_rows] * pages), p0, p1)


def _cache_sum_pages(m, tm, n, tn, page_table):
    nd, n_pages = page_table.shape
    steps = (m // tm) * (n // tn)
    for pages in (4, 8, 16, 32):
        if n_pages % pages == 0 and nd * (n_pages // pages) <= steps:
            return pages
    return None


def _cmp_partial_paged(cache_rows, page_table, p0, p1, li, *, pages):
    n, n_pages = page_table.shape
    page = cache_rows.shape[2] // CACHE_SUB
    w = p0.shape[1]
    gp = page // CMP_STRIDE
    const = lambda i, c, pt: (0, 0)
    page_spec = lambda k: pl.BlockSpec((None, None, page * CACHE_SUB, HEAD_DIM),
                                       lambda i, c, pt: (li, pt[i, c * pages + k], 0, 0))
    out = jax.ShapeDtypeStruct((n, n_pages * gp, w), F32)
    return pl.pallas_call(
        functools.partial(_cmp_partial_paged_body, pages=pages, page=page),
        grid_spec=pltpu.PrefetchScalarGridSpec(
            num_scalar_prefetch=1,
            grid=(n, n_pages // pages),
            in_specs=[page_spec(k) for k in range(pages)] + [pl.BlockSpec(p0.shape, const), pl.BlockSpec(p1.shape, const)],
            out_specs=[pl.BlockSpec((None, pages * gp, w), lambda i, c, pt: (i, c, 0))] * 2,
        ),
        out_shape=[out, out],
        compiler_params=_cp(("parallel", "arbitrary")),
        name="cmp_partial_paged",
    )(page_table, *([cache_rows] * pages), p0, p1)


def _cmp_final_body(a0_ref, a1_ref, wk_ref, wv_ref, gain_ref, kc_ref, vc_ref):
    ng = a0_ref.shape[0]
    agg = a0_ref[...] + pltpu.roll(a1_ref[...], ng - 1, 0)
    for g in range(NSA_KV_HEADS):
        sl = slice(g * HEAD_DIM, (g + 1) * HEAD_DIM)
        ak = agg[:, sl]
        av = agg[:, NSA_KV_HEADS * HEAD_DIM + g * HEAD_DIM:NSA_KV_HEADS * HEAD_DIM + (g + 1) * HEAD_DIM]
        kc_ref[:, sl] = (_rms(_dot(ak, wk_ref[...])) * gain_ref[...]).astype(kc_ref.dtype)
        vc_ref[:, sl] = _dot(av, wv_ref[...]).astype(vc_ref.dtype)


def _cmp_final(a0, a1, w_k, w_v, gain):
    n, ng, w = a0.shape
    const = lambda i: (0, 0)
    out = jax.ShapeDtypeStruct((n, ng, NSA_KV_HEADS * HEAD_DIM), BF16)
    return pl.pallas_call(
        _cmp_final_body,
        grid=(n,),
        in_specs=[
            pl.BlockSpec((None, ng, w), lambda i: (i, 0, 0)),
            pl.BlockSpec((None, ng, w), lambda i: (i, 0, 0)),
            pl.BlockSpec(w_k.shape, const),
            pl.BlockSpec(w_v.shape, const),
            pl.BlockSpec(gain.shape, const),
        ],
        out_specs=[pl.BlockSpec((None, ng, NSA_KV_HEADS * HEAD_DIM), lambda i: (i, 0, 0))] * 2,
        out_shape=[out, out],
        compiler_params=_cp(("parallel",)),
        name="cmp_final",
    )(a0, a1, w_k, w_v, gain)


def _overlap_matrix(n_cmp_pad, n_slc, width):
    ci = np.arange(n_cmp_pad)[:, None]
    sj = np.arange(width)[None, :]
    c_start = ci * CMP_STRIDE
    s_start = sj * SEL_BLOCK
    hit = ((c_start < s_start + SEL_BLOCK) & (c_start + CMP_BLOCK > s_start)
           & (ci < n_cmp_pad - 1) & (sj < n_slc))
    return jnp.asarray(hit, BF16)


def _block_scores(imp, qpos_i, ov, n_slc):
    hi, lo = _split_bf16(imp)
    score = jnp.dot(hi, ov, preferred_element_type=F32) + jnp.dot(lo, ov, preferred_element_type=F32)
    sj = lax.broadcasted_iota(jnp.int32, score.shape, 1)
    cur = qpos_i // SEL_BLOCK
    forced = (sj == 0) | (sj == cur) | (sj == cur - 1)
    score = jnp.where(forced, FORCE_SCORE, score)
    score = jnp.where(sj * SEL_BLOCK <= qpos_i, score, -1.0)
    return jnp.where(sj < n_slc, score, -2.0)


def _masked_softmax(s, allow):
    s = jnp.where(allow, s, NEG_INF)
    m = jnp.max(s, axis=-1, keepdims=True)
    e = jnp.where(allow, jnp.exp(s - m), 0.0)
    return e / jnp.maximum(jnp.sum(e, axis=-1, keepdims=True), 1e-30)


SEL_KEY_BLOCK = 256


def _nsa_prompt_body(q_ref, zg_ref, kc_ref, vc_ref, ov_ref, ks_ref, vs_ref, kw_ref, vw_ref, o_ref, *, t_len, tq):
    g = pl.program_id(1)
    t0 = pl.program_id(2) * tq
    n_cmp_pad = kc_ref.shape[0]
    n_slc = -(-t_len // SEL_BLOCK)
    hpg = HEADS_PER_GROUP
    qpos_i = t0 + lax.broadcasted_iota(jnp.int32, (tq, 1), 0)
    slopes = [jnp.where(g == 0, 2.0 ** -(r + 1), 2.0 ** -(r + 1 + hpg)) for r in range(hpg)]
    slopes2 = [sl * LOG2E for sl in slopes]
    qs = [q_ref[:, r * HEAD_DIM:(r + 1) * HEAD_DIM] for r in range(hpg)]

    ci = lax.broadcasted_iota(jnp.int32, (tq, n_cmp_pad), 1)
    cdist_i = qpos_i - (ci * CMP_STRIDE + CMP_BLOCK - 1)
    callow = (cdist_i >= 0) & (ci < n_cmp_pad - 1)
    cdist = cdist_i.astype(F32)
    kc = kc_ref[...]
    vc = vc_ref[...]
    imp = jnp.zeros((tq, n_cmp_pad), F32)
    o_cmp = []
    for r in range(hpg):
        p = _masked_softmax(_dot_nt(qs[r], kc) * ATT_SCALE - slopes[r] * cdist, callow)
        o_cmp.append(_dot(p, vc))
        imp = imp + p

    score = _block_scores(imp, qpos_i, ov_ref[...], n_slc)
    s_t = score.T[0:n_slc]
    jrow = lax.broadcasted_iota(jnp.int32, (n_slc, tq), 0)
    rank = jnp.zeros((n_slc, tq), jnp.int32)
    for jp in range(n_slc):
        row = s_t[jp:jp + 1, :]
        beats = (row > s_t) | ((row == s_t) & (jp < jrow))
        rank = rank + jnp.where(beats, 1, 0)
    sel_t = jnp.where((rank < N_SEL) & (s_t >= 0.0), 1.0, 0.0)
    sel_t = jnp.concatenate([sel_t, jnp.zeros((LANES - n_slc, tq), F32)], axis=0)
    sel = sel_t.T.astype(BF16)

    kb = SEL_KEY_BLOCK
    nkb = (t0 + tq + kb - 1) // kb

    def sel_step(i, carry):
        k0 = pl.multiple_of(i * kb, kb)
        kblk = ks_ref[pl.ds(k0, kb), :].astype(BF16)
        vblk = vs_ref[pl.ds(k0, kb), :].astype(BF16)
        kpos_i = k0 + lax.broadcasted_iota(jnp.int32, (1, kb), 1)
        ej = lax.broadcasted_iota(jnp.int32, (LANES, kb), 0)
        ec = lax.broadcasted_iota(jnp.int32, (LANES, kb), 1)
        expand = jnp.where(ej == (k0 + ec) // SEL_BLOCK, 1.0, 0.0).astype(BF16)
        allow = (jnp.dot(sel, expand, preferred_element_type=F32) > 0.5) & (kpos_i <= qpos_i)
        kpos = kpos_i.astype(F32)
        out = []
        for r in range(hpg):
            m_old, l_old, acc_old = carry[3 * r:3 * r + 3]
            s = jnp.where(allow, _dot_nt(qs[r], kblk) * (ATT_SCALE * LOG2E) + slopes2[r] * kpos, NEG_INF)
            m_new = jnp.maximum(m_old, jnp.max(s, axis=-1, keepdims=True))
            alpha = jnp.exp2(m_old - m_new)
            e = jnp.exp2(s - m_new)
            out += [m_new, alpha * l_old + jnp.sum(e, axis=-1, keepdims=True),
                    alpha * acc_old + _dot(e, vblk)]
        return tuple(out)

    init = (jnp.full((tq, 1), NEG_INF, F32), jnp.zeros((tq, 1), F32), jnp.zeros((tq, HEAD_DIM), F32)) * hpg
    fin = lax.fori_loop(0, nkb, sel_step, init)
    o_slc = [fin[3 * r + 2] / fin[3 * r + 1] for r in range(hpg)]

    wk = WINDOW + tq
    ws = pl.multiple_of(jnp.clip(t0 - WINDOW, 0, t_len - wk), LANES)
    kw = kw_ref[pl.ds(ws, wk), :].astype(BF16)
    vw = vw_ref[pl.ds(ws, wk), :].astype(BF16)
    wpos_i = ws + lax.broadcasted_iota(jnp.int32, (1, wk), 1)
    wdist_i = qpos_i - wpos_i
    wallow = (wdist_i >= 0) & (wdist_i <= WINDOW)
    wpos = wpos_i.astype(F32)
    gates = _sigmoid(zg_ref[...])
    for r in range(hpg):
        s = jnp.where(wallow, _dot_nt(qs[r], kw) * (ATT_SCALE * LOG2E) + slopes2[r] * wpos, NEG_INF)
        e = jnp.exp2(s - jnp.max(s, axis=-1, keepdims=True))
        o_win = _dot(e, vw) / jnp.sum(e, axis=-1, keepdims=True)
        gate = [jnp.where(g == 0, gates[:, 3 * r + b:3 * r + b + 1],
                          gates[:, 3 * (r + hpg) + b:3 * (r + hpg) + b + 1]) for b in range(3)]
        o = gate[0] * o_cmp[r] + gate[1] * o_slc[r] + gate[2] * o_win
        o_ref[:, r * HEAD_DIM:(r + 1) * HEAD_DIM] = o.astype(o_ref.dtype)


def _nsa_prompt(q, zplain, kc, vc, kvrows, winrows, *, n, t, tq):
    nt = t // tq
    gw = HEADS_PER_GROUP * HEAD_DIM
    n_cmp_pad = kc.shape[1]
    g_blocks = NSA_KV_HEADS
    qrow = lambda i, g, j: (i * nt + j, g)
    seq = lambda blk: (lambda i, g, j: (i, blk + g))
    ov = _overlap_matrix(n_cmp_pad, -(-t // SEL_BLOCK), LANES)
    return pl.pallas_call(
        functools.partial(_nsa_prompt_body, t_len=t, tq=tq),
        grid=(n, NSA_KV_HEADS, nt),
        in_specs=[
            pl.BlockSpec((tq, gw), qrow),
            pl.BlockSpec((tq, LANES), lambda i, g, j: (i * nt + j, PLAIN_GATE_BLK)),
            pl.BlockSpec((None, n_cmp_pad, HEAD_DIM), lambda i, g, j: (i, 0, g)),
            pl.BlockSpec((None, n_cmp_pad, HEAD_DIM), lambda i, g, j: (i, 0, g)),
            pl.BlockSpec(ov.shape, lambda i, g, j: (0, 0)),
            pl.BlockSpec((t, HEAD_DIM), seq(2 * g_blocks)),
            pl.BlockSpec((t, HEAD_DIM), seq(3 * g_blocks)),
            pl.BlockSpec((t, HEAD_DIM), seq(0)),
            pl.BlockSpec((t, HEAD_DIM), seq(g_blocks)),
        ],
        out_specs=pl.BlockSpec((tq, gw), qrow),
        out_shape=jax.ShapeDtypeStruct((n * t, NSA_WIDTH), BF16),
        compiler_params=_cp(("parallel", "parallel", "arbitrary")),
        name="nsa_prompt",
    )(q, zplain, kc, vc, ov, kvrows, kvrows, winrows, winrows)


KV_OFF = NSA_WIDTH
GATE_OFF = KV_OFF + KV_WIDTH
PLAIN_OFF = GATE_OFF + GATE_WIDTH
SLOT_WIDTH = NSA_KV_HEADS * HEAD_DIM


def _stacked_weights(w_in, cmp_w, w_branch, w_o, w_gu, w_down, w_ple_proj, w_ple_gate):
    w_t = jnp.swapaxes(w_in, 1, 2).astype(BF16)
    w_gate = jnp.pad(w_t[:, GATE_OFF:PLAIN_OFF], ((0, 0), (0, LANES - GATE_WIDTH), (0, 0)))
    return dict(
        w_head=w_t,
        w_plain=jnp.concatenate([w_t[:, PLAIN_OFF:], w_gate], axis=1),
        w_phi=cmp_w.astype(BF16),
        w_branch=w_branch.astype(BF16),
        w_o=w_o.astype(BF16),
        w_gu=w_gu.astype(BF16),
        w_down=w_down.astype(BF16),
        w_ple_proj=w_ple_proj.astype(BF16),
        w_ple_gate=w_ple_gate.astype(BF16),
    )


def _layer_weights(li, stacked, norms, qk_gain, cmp_pos, out_gain, lb_all):
    ones = jnp.ones((SLOT_WIDTH,), F32)
    zeros = jnp.zeros((SLOT_WIDTH,), F32)
    gain = lambda i: jnp.tile(qk_gain[li, i], NSA_KV_HEADS)
    row = lambda v: v.reshape(1, -1).astype(F32)
    cp = cmp_pos[li]
    half = lambda w, m: w[m * CMP_STRIDE:(m + 1) * CMP_STRIDE]
    pos = lambda m: jnp.concatenate([half(cp[0], m)] * NSA_KV_HEADS + [half(cp[1], m)] * NSA_KV_HEADS, axis=1)
    w = dict(stacked)
    w.update(
        li=li,
        norm=[row(norms[li, i]) for i in range(4)],
        q_gain=row(jnp.tile(qk_gain[li, 0], NSA_HEADS)),
        q_flag=jnp.ones((1, NSA_WIDTH), F32),
        kv_gain=row(jnp.concatenate([ones, ones, gain(1), ones])),
        kv_flag=row(jnp.concatenate([zeros, zeros, ones, zeros])),
        win_gain=row(jnp.concatenate([gain(2), ones])),
        win_flag=row(jnp.concatenate([ones, zeros])),
        cmp_p0=pos(0).astype(F32),
        cmp_p1=pos(1).astype(F32),
        w_phi_k=stacked["w_phi"][li, 0],
        w_phi_v=stacked["w_phi"][li, 1],
        kc_gain=row(qk_gain[li, 3]),
        ret_gain=row(out_gain[li, 0]),
        hgrn_gain=row(out_gain[li, 1]),
        lb=lb_all[li:li + 1],
    )
    return w


def _tiles(m):
    big = 512 if m % 512 == 0 else m
    small = 256 if m % 256 == 0 else m
    return big, small


def _projections(u, w, tm, row_bufs, cache=None):
    li, depth = w["li"], w["depth"]
    head = functools.partial(_proj, u, w["w_head"], li, tm=tm)
    q = head(col0=0, n=NSA_WIDTH, tn=512, out_dtype=BF16, gain=w["q_gain"], flag=w["q_flag"], name="proj_q")
    kvrows, kv_buf = head(col0=KV_OFF, n=4 * SLOT_WIDTH, tn=4 * SLOT_WIDTH, out_dtype=F32, gain=w["kv_gain"],
                          flag=w["kv_flag"], name="proj_kv", rows_out=(depth, row_bufs[0]))
    winrows, win_buf = head(col0=KV_OFF + 4 * SLOT_WIDTH, n=2 * SLOT_WIDTH, tn=2 * SLOT_WIDTH, out_dtype=F32,
                            gain=w["win_gain"], flag=w["win_flag"], name="proj_win", rows_out=(depth, row_bufs[1]))
    tn = 1152
    pages = None if cache is None else _cache_sum_pages(u.shape[0], tm, PLAIN_WIDTH, tn, cache[1])
    if pages is None:
        zplain = _proj(u, w["w_plain"], li, col0=0, n=PLAIN_WIDTH, tm=tm, tn=tn, out_dtype=F32, name="proj_plain")
        return q, kvrows, winrows, zplain, (kv_buf, win_buf), None
    zplain, a0, a1 = _proj_with_cache_sums(u, w["w_plain"], li, cache[0], cache[1], w["cmp_p0"], w["cmp_p1"],
                                           n=PLAIN_WIDTH, tm=tm, tn=tn, pages=pages)
    return q, kvrows, winrows, zplain, (kv_buf, win_buf), (a0, a1)


def _layer_tail(h, o_nsa, o_ret, o_hgrn, zplain, p, w, tm, ts):
    li = w["li"]
    h = _merge(o_nsa, o_ret, o_hgrn, zplain, h, w["w_branch"], w["w_o"], li, tm=ts)
    h, u = _ffn(h, w["norm"][2], w["w_gu"], w["w_down"], w["norm"][3], li, 1, tm=tm, tf=512)
    return _ple(h, u, p, w["w_ple_gate"], w["w_ple_proj"], li, tm=ts)


def _prompt_layer(x, p, w, row_bufs, *, n, t, cache=None):
    tm, ts = _tiles(n * t)
    h, u = _ffn(x, w["norm"][0], w["w_gu"], w["w_down"], w["norm"][1], w["li"], 0, tm=tm, tf=512)
    q, kvrows, winrows, zplain, row_bufs, cache_sums = _projections(
        u, w, 1024 if (n * t) % 1024 == 0 else tm, row_bufs, cache)
    a0, a1 = _cmp_partial(kvrows, w["cmp_p0"], w["cmp_p1"], rb=min(1024, t))
    ng = t // CMP_STRIDE
    kc, vc = _cmp_final(a0.reshape(n, ng, -1), a1.reshape(n, ng, -1), w["w_phi_k"], w["w_phi_v"], w["kc_gain"])
    o_nsa = _nsa_prompt(q, zplain, kc, vc, kvrows, winrows, n=n, t=t, tq=2 * LANES)
    o_ret, s_ret = _retention(zplain, w["ret_gain"], None, 0, n=n, t=t, c=LANES)
    o_hgrn, s_hgrn = _hgrn(zplain, w["lb"], w["hgrn_gain"], None, 0, n=n, t=t, c=LANES)
    h = _layer_tail(h, o_nsa, o_ret, o_hgrn, zplain, p, w, tm, ts)
    return h, row_bufs, s_ret, s_hgrn, cache_sums


def _nsa_dec_cmp_body(q_ref, kc_ref, vc_ref, ov_ref, ocmp_ref, sel_ref, *, past, steps, n_slc):
    n, gb = q_ref.shape[0], q_ref.shape[1]
    nc = kc_ref.shape[1]
    hpg = HEADS_PER_GROUP
    rows = hpg * steps
    ri = lax.broadcasted_iota(jnp.int32, (rows, 1), 0)
    qpos_i = past + ri % steps
    ci = lax.broadcasted_iota(jnp.int32, (rows, nc), 1)
    dist_i = qpos_i - (ci * CMP_STRIDE + CMP_BLOCK - 1)
    allow = (dist_i >= 0) & (ci < nc - 1)
    dist = dist_i.astype(F32)
    imps = []
    for i in range(n):
        for g in range(gb):
            slope = jnp.exp2(-(ri // steps + 1 + g * hpg).astype(F32))
            sl = slice(g * HEAD_DIM, (g + 1) * HEAD_DIM)
            p = _masked_softmax(_dot_nt(q_ref[i, g], kc_ref[i, :, sl]) * ATT_SCALE - slope * dist, allow)
            ocmp_ref[i, g] = _dot(p, vc_ref[i, :, sl])
            imp = p[0:steps]
            for r in range(1, hpg):
                imp = imp + p[r * steps:(r + 1) * steps]
            imps.append(imp)
    imp_all = jnp.concatenate(imps, axis=0)

    width = ov_ref.shape[1]
    all_rows = n * gb * steps
    qpos_all = past + lax.broadcasted_iota(jnp.int32, (all_rows, 1), 0) % steps
    score = _block_scores(imp_all, qpos_all, ov_ref[...], n_slc)
    sj = lax.broadcasted_iota(jnp.int32, score.shape, 1)
    lane = lax.broadcasted_iota(jnp.int32, (all_rows, LANES), 1)
    picked = jnp.full((all_rows, LANES), -1, jnp.int32)
    for it in range(min(N_SEL, n_slc)):
        m = jnp.max(score, axis=-1, keepdims=True)
        idx = jnp.min(jnp.where(score == m, sj, width), axis=-1, keepdims=True)
        picked = jnp.where(lane == it, jnp.where(m >= 0.0, idx, -1), picked)
        score = jnp.where(sj == idx, -3.0, score)
    sel_ref[...] = picked


def _nsa_dec_cmp(q_rt, kc, vc, *, past, steps, n_slc):
    n, ng = kc.shape[0], kc.shape[1]
    gb = NSA_KV_HEADS
    ov = _overlap_matrix(ng, n_slc, -(-n_slc // LANES) * LANES)
    o_cmp, sel = pl.pallas_call(
        functools.partial(_nsa_dec_cmp_body, past=past, steps=steps, n_slc=n_slc),
        out_shape=[
            jax.ShapeDtypeStruct((n, gb, HEADS_PER_GROUP * steps, HEAD_DIM), F32),
            jax.ShapeDtypeStruct((n * gb * steps, LANES), jnp.int32),
        ],
        compiler_params=pltpu.CompilerParams(vmem_limit_bytes=VMEM_LIMIT),
        name="nsa_dec_cmp",
    )(q_rt, kc, vc, ov)
    return o_cmp, sel.reshape(n, gb, steps, LANES)


ROW_PAD = 8


def _nsa_dec_attend_body(pt_ref, sel_ref, q_ref, gate_ref, ocmp_ref, kvn_ref, wn_ref, wp_ref, *refs,
                         past, steps, n_sel, past_blocks):
    gb = NSA_KV_HEADS
    blk_refs = refs[:gb * n_sel]
    o_ref = refs[gb * n_sel]
    i = pl.program_id(0)
    t = pl.program_id(1)
    qpos = past + t
    head = jnp.minimum(lax.broadcasted_iota(jnp.int32, (ROW_PAD, 1), 0), HEADS_PER_GROUP - 1)
    new_i = lax.broadcasted_iota(jnp.int32, (1, steps), 1)
    new_allow = new_i <= t
    new_dist = (t - new_i).astype(F32)
    cols = n_sel * SEL_BLOCK
    col = lax.broadcasted_iota(jnp.int32, (1, cols), 1)
    col_pick = col // SEL_BLOCK
    head_cols = lambda ref, blk: ref[:, blk * HEAD_DIM:(blk + 1) * HEAD_DIM]

    def two_part_attention(s_a, allow_a, v_a, s_b, allow_b, v_b):
        s_a = jnp.where(allow_a, s_a, NEG_INF)
        s_b = jnp.where(allow_b, s_b, NEG_INF)
        m = jnp.maximum(jnp.max(s_a, axis=-1, keepdims=True), jnp.max(s_b, axis=-1, keepdims=True))
        e_a = jnp.where(allow_a, jnp.exp(s_a - m), 0.0)
        e_b = jnp.where(allow_b, jnp.exp(s_b - m), 0.0)
        l = jnp.sum(e_a, axis=-1, keepdims=True) + jnp.sum(e_b, axis=-1, keepdims=True)
        return (_dot(e_a, v_a) + _dot(e_b, v_b)) / l

    for g in range(gb):
        q = q_ref[g]
        slope = jnp.exp2(-(head + 1 + g * HEADS_PER_GROUP).astype(F32))
        picked = [blk_refs[g * n_sel + k] for k in range(n_sel)]
        kall = jnp.concatenate([r[pl.ds(2 * gb + g, SEL_BLOCK, stride=CACHE_SUB), :] for r in picked], axis=0)
        vall = jnp.concatenate([r[pl.ds(3 * gb + g, SEL_BLOCK, stride=CACHE_SUB), :] for r in picked], axis=0)
        base = ((i * gb + g) * steps + t) * n_sel
        blk = jnp.zeros((1, cols), jnp.int32)
        for k in range(n_sel):
            blk = jnp.where(col_pick == k, sel_ref[base + k], blk)
        dist_i = qpos - (blk * SEL_BLOCK + col % SEL_BLOCK)
        allow = (blk >= 0) & (blk < past_blocks) & (dist_i >= 0)
        s_sel = _dot_nt(q, kall) * ATT_SCALE - slope * dist_i.astype(F32)
        s_new = _dot_nt(q, head_cols(kvn_ref, 2 * gb + g)) * ATT_SCALE - slope * new_dist
        o_slc = two_part_attention(s_sel, allow, vall, s_new, new_allow, head_cols(kvn_ref, 3 * gb + g))

        wb = wp_ref.shape[0] // (2 * gb)
        kwp = wp_ref[pl.ds(g, wb, stride=2 * gb), :]
        vwp = wp_ref[pl.ds(gb + g, wb, stride=2 * gb), :]
        wdist_i = qpos - (past - wb + lax.broadcasted_iota(jnp.int32, (1, wb), 1))
        wallow = (wdist_i >= 0) & (wdist_i <= WINDOW)
        s_wp = _dot_nt(q, kwp) * ATT_SCALE - slope * wdist_i.astype(F32)
        s_wn = _dot_nt(q, head_cols(wn_ref, g)) * ATT_SCALE - slope * new_dist
        o_win = two_part_attention(s_wp, wallow, vwp, s_wn, new_allow, head_cols(wn_ref, gb + g))

        gates = _sigmoid(gate_ref[g])
        o = gates[:, 0:1] * ocmp_ref[g] + gates[:, 1:2] * o_slc + gates[:, 2:3] * o_win
        o_ref[g] = o.astype(o_ref.dtype)


def _nsa_dec_attend(page_table, sel_flat, q_tr, gates_tr, ocmp_tr, kvrows, winrows, cache_half, win_rows_view, li, *,
                    past, steps, n_sel):
    n = q_tr.shape[0]
    rows = steps * ROW_PAD
    gb = NSA_KV_HEADS
    halves = cache_half.shape[2]
    past_blocks = past // SEL_BLOCK
    step = lambda i, t, pt, sel: (i, 0, t, 0)
    new = lambda i, t, pt, sel: (i, 0)

    def gather(g, k):
        def index(i, t, pt, sel):
            j = jnp.clip(sel[((i * gb + g) * steps + t) * n_sel + k], 0, past_blocks - 1)
            return (li, pt[i, j // halves], j % halves, 0, 0)
        return pl.BlockSpec((None, None, None, SEL_BLOCK * CACHE_SUB, HEAD_DIM), index)

    in_specs = [
        pl.BlockSpec((None, gb, ROW_PAD, HEAD_DIM), step),
        pl.BlockSpec((None, gb, ROW_PAD, LANES), step),
        pl.BlockSpec((None, gb, ROW_PAD, HEAD_DIM), step),
        pl.BlockSpec((steps, kvrows.shape[1]), new),
        pl.BlockSpec((steps, winrows.shape[1]), new),
        pl.BlockSpec((None, None, win_rows_view.shape[2], HEAD_DIM), lambda i, t, pt, sel: (li, i, 0, 0)),
    ]
    in_specs += [gather(g, k) for g in range(gb) for k in range(n_sel)]
    return pl.pallas_call(
        functools.partial(_nsa_dec_attend_body, past=past, steps=steps, n_sel=n_sel, past_blocks=past_blocks),
        grid_spec=pltpu.PrefetchScalarGridSpec(
            num_scalar_prefetch=2,
            grid=(n, steps),
            in_specs=in_specs,
            out_specs=pl.BlockSpec((None, gb, ROW_PAD, HEAD_DIM), step),
        ),
        out_shape=jax.ShapeDtypeStruct((n, gb, rows, HEAD_DIM), BF16),
        compiler_params=_cp(("parallel", "arbitrary")),
        name="nsa_dec_attend",
    )(page_table, sel_flat, q_tr, gates_tr, ocmp_tr, kvrows, winrows, win_rows_view, *([cache_half] * (gb * n_sel)))


def _decode_layer(x, p, w, row_bufs, cache_rows, cache_half, win_rows_view, state_ret, state_hgrn, page_table,
                  cache_sums, *, n, steps):
    m = n * steps
    li = w["li"]
    hpg, gb = HEADS_PER_GROUP, NSA_KV_HEADS
    past = page_table.shape[1] * (cache_rows.shape[2] // CACHE_SUB)
    n_slc = -(-(past + steps) // SEL_BLOCK)
    n_sel = min(N_SEL, n_slc)
    h, u = _ffn(x, w["norm"][0], w["w_gu"], w["w_down"], w["norm"][1], li, 0, tm=m, tf=512)
    q, kvrows, winrows, zplain, row_bufs, _ = _projections(u, w, m, row_bufs)
    if cache_sums is None:
        cache_sums = _cmp_partial_paged(cache_rows, page_table, w["cmp_p0"], w["cmp_p1"], li, pages=8)
    kc, vc = _cmp_final(*cache_sums, w["w_phi_k"], w["w_phi_v"], w["kc_gain"])
    q5 = q.reshape(n, steps, gb, hpg, HEAD_DIM)
    q_rt = q5.transpose(0, 2, 3, 1, 4).reshape(n, gb, hpg * steps, HEAD_DIM)
    o_cmp, sel = _nsa_dec_cmp(q_rt, kc, vc, past=past, steps=steps, n_slc=n_slc)
    pad_heads = lambda a: jnp.pad(a, ((0, 0), (0, 0), (0, 0), (0, ROW_PAD - hpg), (0, 0)))
    rows = steps * ROW_PAD
    q_tr = pad_heads(q5.transpose(0, 2, 1, 3, 4)).reshape(n, gb, rows, HEAD_DIM)
    ocmp_tr = pad_heads(o_cmp.reshape(n, gb, hpg, steps, HEAD_DIM).transpose(0, 1, 3, 2, 4)).reshape(
        n, gb, rows, HEAD_DIM)
    zg = zplain[:, PLAIN_GATE_BLK * LANES:PLAIN_GATE_BLK * LANES + GATE_WIDTH].reshape(n, steps, gb, hpg, 3)
    gates_tr = pad_heads(zg.transpose(0, 2, 1, 3, 4)).reshape(n, gb, rows, 3)
    gates_tr = jnp.pad(gates_tr, ((0, 0), (0, 0), (0, 0), (0, LANES - 3)))
    sel_flat = sel[..., :n_sel].reshape(-1)
    o_tr = _nsa_dec_attend(page_table, sel_flat, q_tr, gates_tr, ocmp_tr, kvrows, winrows, cache_half, win_rows_view,
                           li, past=past, steps=steps, n_sel=n_sel)
    o_nsa = o_tr.reshape(n, gb, steps, ROW_PAD, HEAD_DIM)[:, :, :, :hpg].transpose(0, 2, 1, 3, 4).reshape(
        m, NSA_WIDTH)
    o_ret, s_ret = _retention(zplain, w["ret_gain"], state_ret, li, n=n, t=steps, c=steps)
    o_hgrn, s_hgrn = _hgrn(zplain, w["lb"], w["hgrn_gain"], state_hgrn, li, n=n, t=steps, c=steps)
    h = _layer_tail(h, o_nsa, o_ret, o_hgrn, zplain, p, w, m, m)
    return h, row_bufs, s_ret, s_hgrn


def kernel(x_prompt, x_sample, cache_kv, cache_win, state_ret, state_hgrn, page_table, p_prompt, p_sample, norms,
           w_in, qk_gain, cmp_pos, cmp_w, out_gain, hgrn_lb, w_branch, w_o, w_gu, w_down, w_ple_proj, w_ple_gate):
    n_p, t_p, d = x_prompt.shape
    n_d, t_d, _ = x_sample.shape
    depth, n_pool, page = cache_kv.shape[:3]
    cache_rows = cache_kv.reshape(depth, n_pool, page * CACHE_SUB, HEAD_DIM)
    cache_half = cache_kv.reshape(depth, n_pool, page // SEL_BLOCK, SEL_BLOCK * CACHE_SUB, HEAD_DIM)
    wb = cache_win.shape[2]
    win_rows_view = cache_win.reshape(depth, n_d, wb * 2 * NSA_KV_HEADS, HEAD_DIM)
    lb_all = _hgrn_lower_bounds(hgrn_lb)
    stacked = _stacked_weights(w_in, cmp_w, w_branch, w_o, w_gu, w_down, w_ple_proj, w_ple_gate)
    hp = x_prompt.reshape(n_p * t_p, d)
    hs = x_sample.reshape(n_d * t_d, d)
    bufs_p = bufs_s = (None, None)
    states = [[] for _ in range(4)]
    for li in range(depth):
        w = _layer_weights(li, stacked, norms, qk_gain, cmp_pos, out_gain, lb_all)
        w["depth"] = depth
        hp, bufs_p, s_r, s_h, cache_sums = _prompt_layer(hp, p_prompt[li].reshape(n_p * t_p, -1), w, bufs_p,
                                                         n=n_p, t=t_p, cache=(cache_rows, page_table))
        states[0].append(s_r)
        states[2].append(s_h)
        hs, bufs_s, s_r, s_h = _decode_layer(hs, p_sample[li].reshape(n_d * t_d, -1), w, bufs_s, cache_rows,
                                             cache_half, win_rows_view, state_ret, state_hgrn, page_table,
                                             cache_sums, n=n_d, steps=t_d)
        states[1].append(s_r)
        states[3].append(s_h)
    gb = NSA_KV_HEADS
    kv_p = bufs_p[0].reshape(depth, n_p, t_p, 4, gb, HEAD_DIM)
    kv_s = bufs_s[0].reshape(depth, n_d, t_d, 4, gb, HEAD_DIM)
    win_p = bufs_p[1].reshape(depth, n_p, t_p, 2, gb, HEAD_DIM)[:, :, t_p - min(WINDOW, t_p):]
    win_all = jnp.concatenate([cache_win, bufs_s[1].reshape(depth, n_d, t_d, 2, gb, HEAD_DIM)], axis=2)
    win_s = win_all[:, :, wb + t_d - min(WINDOW, wb + t_d):]
    ret_p, ret_s, hg_p, hg_s = (jnp.stack(s) for s in states)
    return (hp.reshape(n_p, t_p, d), hs.reshape(n_d, t_d, d), kv_p, kv_s, win_p, win_s, ret_p, ret_s, hg_p, hg_s)
```

```python
import functools
import math

import jax
import jax.numpy as jnp
import numpy as np
from jax import lax
from jax.experimental import pallas as pl
from jax.experimental.pallas import tpu as pltpu

F32 = jnp.float32
BF16 = jnp.bfloat16

D_MODEL = 2048
HEAD_DIM = 128
NSA_HEADS = 8
NSA_KV_HEADS = 2
HEADS_PER_GROUP = NSA_HEADS // NSA_KV_HEADS
CMP_BLOCK = 32
CMP_STRIDE = 16
SEL_BLOCK = 64
N_SEL = 16
WINDOW = 512
RET_HEADS = 4
HGRN_HEADS = 4
D_FF = 5632
PLE_DIM = 256
NSA_WIDTH = NSA_HEADS * HEAD_DIM
RET_WIDTH = RET_HEADS * HEAD_DIM
HGRN_WIDTH = HGRN_HEADS * HEAD_DIM
KV_WIDTH = 6 * NSA_KV_HEADS * HEAD_DIM
GATE_WIDTH = 3 * NSA_HEADS
NEG_INF = -1e30
FORCE_SCORE = 1e6
EPS = 1e-6
ATT_SCALE = HEAD_DIM ** -0.5
LOG2E = math.log2(math.e)

LANES = 128
MXU_WIDTH = 256
PLAIN_RET_BLK = 0
PLAIN_HGRN_BLK = (4 * RET_WIDTH) // LANES
PLAIN_MERGE_OFF = 4 * RET_WIDTH + 4 * HGRN_WIDTH
PLAIN_GATE_BLK = (PLAIN_MERGE_OFF + 3 * D_MODEL) // LANES
PLAIN_TILE = 3 * MXU_WIDTH
PLAIN_WIDTH = -(-(PLAIN_MERGE_OFF + 3 * D_MODEL + GATE_WIDTH) // PLAIN_TILE) * PLAIN_TILE

VMEM_LIMIT = 56 * 1024 * 1024


def _cp(sem, vmem=VMEM_LIMIT):
    return pltpu.CompilerParams(dimension_semantics=sem, vmem_limit_bytes=vmem)


def _rms(x):
    return x * lax.rsqrt(jnp.mean(x * x, axis=-1, keepdims=True) + EPS)


def _dot(a, b):
    return jnp.dot(a.astype(BF16), b.astype(BF16), preferred_element_type=F32)


def _dot_nt(a, b):
    return lax.dot_general(a.astype(BF16), b.astype(BF16), (((1,), (1,)), ((), ())),
                           preferred_element_type=F32)


def _dot_tn(a, b):
    rows = a.shape[0]
    if rows % LANES:
        pad = LANES - rows % LANES
        a = jnp.concatenate([a, jnp.zeros((pad, a.shape[1]), a.dtype)], axis=0)
        b = jnp.concatenate([b, jnp.zeros((pad, b.shape[1]), b.dtype)], axis=0)
    return _dot(a.T, b)


def _split_bf16(x):
    hi = x.astype(BF16)
    lo = (x - hi.astype(F32)).astype(BF16)
    return hi, lo


def _sigmoid(x):
    return 1.0 / (1.0 + jnp.exp(-x))


def _ffn_body(x_ref, g1_ref, wg_ref, wv_ref, wd_ref, g2_ref, h_ref, u_ref, xn_sc, acc_sc, *, nf):
    j = pl.program_id(1)

    @pl.when(j == 0)
    def _():
        xn_sc[...] = (_rms(x_ref[...]) * g1_ref[...]).astype(BF16)
        acc_sc[...] = jnp.zeros_like(acc_sc)

    xn = xn_sc[...]
    g = jnp.dot(xn, wg_ref[...], preferred_element_type=F32)
    v = jnp.dot(xn, wv_ref[...], preferred_element_type=F32)
    a = (g * _sigmoid(g) * v).astype(BF16)
    acc_sc[...] += jnp.dot(a, wd_ref[...], preferred_element_type=F32)

    @pl.when(j == nf - 1)
    def _():
        h = x_ref[...] + 0.5 * acc_sc[...]
        h_ref[...] = h
        u_ref[...] = (_rms(h) * g2_ref[...]).astype(BF16)


def _ffn(x, g1, w_gu, w_down, g2, li, which, *, tm, tf):
    m, d = x.shape
    f = w_down.shape[2]
    nf = f // tf
    return pl.pallas_call(
        functools.partial(_ffn_body, nf=nf),
        grid=(m // tm, nf),
        in_specs=[
            pl.BlockSpec((tm, d), lambda i, j: (i, 0)),
            pl.BlockSpec((1, d), lambda i, j: (0, 0)),
            pl.BlockSpec((None, None, d, tf), lambda i, j: (li, which, 0, j)),
            pl.BlockSpec((None, None, d, tf), lambda i, j: (li, which, 0, j + nf)),
            pl.BlockSpec((None, None, tf, d), lambda i, j: (li, which, j, 0)),
            pl.BlockSpec((1, d), lambda i, j: (0, 0)),
        ],
        out_specs=[
            pl.BlockSpec((tm, d), lambda i, j: (i, 0)),
            pl.BlockSpec((tm, d), lambda i, j: (i, 0)),
        ],
        out_shape=[jax.ShapeDtypeStruct((m, d), F32), jax.ShapeDtypeStruct((m, d), BF16)],
        scratch_shapes=[pltpu.VMEM((tm, d), BF16), pltpu.VMEM((tm, d), F32)],
        compiler_params=_cp(("parallel", "arbitrary")),
        name="ffn",
    )(x, g1, w_gu, w_gu, w_down, g2)


def _proj_plain_body(u_ref, w_ref, o_ref):
    o_ref[...] = _dot_nt(u_ref[...], w_ref[...]).astype(o_ref.dtype)


def _proj_norm_body(u_ref, w_ref, gain_ref, flag_ref, *refs, tn, rows_out):
    o_ref = refs[-2] if rows_out else refs[-1]
    z = _dot_nt(u_ref[...], w_ref[...])
    chunks = tn // LANES
    for c in range(chunks):
        sl = slice(c * LANES, (c + 1) * LANES)
        zc = z[:, sl]
        normed = _rms(zc) * gain_ref[:, sl]
        val = jnp.where(flag_ref[:, sl] > 0.5, normed, zc)
        o_ref[:, sl] = val.astype(o_ref.dtype)
        if rows_out:
            refs[-1][pl.ds(c, z.shape[0], stride=chunks), :] = val


def _proj(u, w, li, *, col0, n, tm, tn, out_dtype, gain=None, flag=None, name, rows_out=None):
    m, k = u.shape
    cb = col0 // tn
    in_specs = [pl.BlockSpec((tm, k), lambda i, j: (i, 0)), pl.BlockSpec((None, tn, k), lambda i, j: (li, cb + j, 0))]
    args = [u, w]
    out_specs = pl.BlockSpec((tm, tn), lambda i, j: (i, j))
    out_shape = jax.ShapeDtypeStruct((m, n), out_dtype)
    aliases = {}
    if gain is None:
        body = _proj_plain_body
    else:
        body = functools.partial(_proj_norm_body, tn=tn, rows_out=rows_out is not None)
        in_specs += [pl.BlockSpec((1, tn), lambda i, j: (0, j)), pl.BlockSpec((1, tn), lambda i, j: (0, j))]
        args += [gain, flag]
    if rows_out is not None:
        assert n == tn, "the row-major store needs the whole width in one column tile"
        depth, prev = rows_out
        chunks = n // LANES
        out_specs = [out_specs, pl.BlockSpec((None, tm * chunks, LANES), lambda i, j: (li, i, 0))]
        out_shape = [out_shape, jax.ShapeDtypeStruct((depth, m * chunks, LANES), F32)]
        if prev is not None:
            in_specs.append(pl.BlockSpec(memory_space=pl.ANY))
            args.append(prev)
            aliases = {len(args) - 1: 1}
    return pl.pallas_call(
        body,
        grid=(m // tm, n // tn),
        in_specs=in_specs,
        out_specs=out_specs,
        out_shape=out_shape,
        input_output_aliases=aliases,
        compiler_params=_cp(("parallel", "arbitrary")),
        name=name,
    )(*args)


def _merge_body(oa_ref, ob_ref, oc_ref, m0_ref, m1_ref, m2_ref, h_ref, wb_ref, wo_ref, out_ref):
    ya = jnp.dot(oa_ref[...], wb_ref[0:NSA_WIDTH, :], preferred_element_type=F32)
    yb = jnp.dot(ob_ref[...], wb_ref[NSA_WIDTH:NSA_WIDTH + RET_WIDTH, :], preferred_element_type=F32)
    yc = jnp.dot(oc_ref[...], wb_ref[NSA_WIDTH + RET_WIDTH:, :], preferred_element_type=F32)
    mixed = _sigmoid(m0_ref[...]) * ya + _sigmoid(m1_ref[...]) * yb + _sigmoid(m2_ref[...]) * yc
    out_ref[...] = h_ref[...] + jnp.dot(mixed.astype(BF16), wo_ref[...], preferred_element_type=F32)


def _merge(o_nsa, o_ret, o_hgrn, zplain, h, w_branch, w_o, li, *, tm):
    m, d = h.shape
    mb = PLAIN_MERGE_OFF // d
    row = lambda i: (i, 0)
    layer = lambda i: (li, 0, 0)
    return pl.pallas_call(
        _merge_body,
        grid=(m // tm,),
        in_specs=[
            pl.BlockSpec((tm, NSA_WIDTH), row),
            pl.BlockSpec((tm, RET_WIDTH), row),
            pl.BlockSpec((tm, HGRN_WIDTH), row),
            pl.BlockSpec((tm, d), lambda i: (i, mb)),
            pl.BlockSpec((tm, d), lambda i: (i, mb + 1)),
            pl.BlockSpec((tm, d), lambda i: (i, mb + 2)),
            pl.BlockSpec((tm, d), row),
            pl.BlockSpec((None,) + w_branch.shape[1:], layer),
            pl.BlockSpec((None,) + w_o.shape[1:], layer),
        ],
        out_specs=pl.BlockSpec((tm, d), row),
        out_shape=jax.ShapeDtypeStruct((m, d), F32),
        compiler_params=_cp(("parallel",)),
        name="merge",
    )(o_nsa, o_ret, o_hgrn, zplain, zplain, zplain, h, w_branch, w_o)


def _ple_body(h_ref, u_ref, p_ref, wg_ref, wp_ref, out_ref):
    gate = _sigmoid(jnp.dot(u_ref[...], wg_ref[...], preferred_element_type=F32))
    proj = jnp.dot(p_ref[...].astype(BF16), wp_ref[...], preferred_element_type=F32)
    out_ref[...] = h_ref[...] + gate * proj


def _ple(h, u, p, w_gate, w_proj, li, *, tm):
    m, d = h.shape
    row = lambda i: (i, 0)
    layer = lambda i: (li, 0, 0)
    return pl.pallas_call(
        _ple_body,
        grid=(m // tm,),
        in_specs=[
            pl.BlockSpec((tm, d), row),
            pl.BlockSpec((tm, d), row),
            pl.BlockSpec((tm, p.shape[1]), row),
            pl.BlockSpec((None,) + w_gate.shape[1:], layer),
            pl.BlockSpec((None,) + w_proj.shape[1:], layer),
        ],
        out_specs=pl.BlockSpec((tm, d), row),
        out_shape=jax.ShapeDtypeStruct((m, d), F32),
        compiler_params=_cp(("parallel",)),
        name="ple",
    )(h, u, p, w_gate, w_proj)


def _retention_tables(c):
    lg = np.log1p(-np.exp2(-5.0 - np.arange(RET_HEADS, dtype=np.float64)))
    i = np.arange(c, dtype=np.float64)
    diff = i[:, None] - i[None, :]
    dmat = np.where(diff >= 0, np.exp(lg[:, None, None] * np.maximum(diff, 0.0)), 0.0)
    q_dec = np.exp(lg[:, None] * (i + 1.0))[..., None] * np.ones((1, 1, HEAD_DIM))
    k_dec = np.exp(lg[:, None] * (c - 1.0 - i))[..., None] * np.ones((1, 1, HEAD_DIM))
    c_dec = np.exp(lg * c)[:, None, None] * np.ones((1, 8, HEAD_DIM))
    return tuple(jnp.asarray(a, F32) for a in (dmat, q_dec, k_dec, c_dec))


def _retention_body(*refs, has_state, nch, c):
    if has_state:
        q_ref, k_ref, v_ref, g_ref, dm_ref, qd_ref, kd_ref, cd_ref, gain_ref, s0_ref, o_ref, s_ref = refs
    else:
        q_ref, k_ref, v_ref, g_ref, dm_ref, qd_ref, kd_ref, cd_ref, gain_ref, o_ref, s_ref = refs
    heads = range(RET_HEADS)
    cols = lambda hh: slice(hh * HEAD_DIM, (hh + 1) * HEAD_DIM)

    def chunk(ci, states):
        rows = pl.ds(pl.multiple_of(ci * c, c), c)
        out = []
        for hh in heads:
            q = q_ref[rows, cols(hh)]
            k = k_ref[rows, cols(hh)] * ATT_SCALE
            v = v_ref[rows, cols(hh)]
            a = _dot_nt(q, k) * dm_ref[hh]
            o = _dot(a, v) + _dot(q * qd_ref[hh], states[hh])
            g = g_ref[rows, cols(hh)]
            o_ref[rows, cols(hh)] = (_rms(o) * gain_ref[:, cols(hh)] * (g * _sigmoid(g))).astype(o_ref.dtype)
            out.append(states[hh] * cd_ref[hh, 0:1, :] + _dot_tn(k * kd_ref[hh], v))
        return tuple(out)

    init = tuple(s0_ref[hh] if has_state else jnp.zeros((LANES, LANES), F32) for hh in heads)
    final = lax.fori_loop(0, nch, chunk, init)
    for hh in heads:
        s_ref[hh] = final[hh]


def _retention(zplain, gain, s0, li, *, n, t, c):
    nch = t // c
    tables = _retention_tables(c)
    hb = RET_HEADS
    width = hb * HEAD_DIM
    first = PLAIN_RET_BLK * LANES // width
    in_specs = [pl.BlockSpec((t, width), (lambda w: (lambda i: (i, first + w)))(w)) for w in range(4)]
    in_specs += [pl.BlockSpec(tab.shape, lambda i: (0, 0, 0)) for tab in tables]
    in_specs.append(pl.BlockSpec((1, width), lambda i: (0, 0)))
    args = [zplain] * 4 + list(tables) + [gain]
    if s0 is not None:
        in_specs.append(pl.BlockSpec((None, None, hb, LANES, LANES), lambda i: (li, i, 0, 0, 0)))
        args.append(s0)
    return pl.pallas_call(
        functools.partial(_retention_body, has_state=s0 is not None, nch=nch, c=c),
        grid=(n,),
        in_specs=in_specs,
        out_specs=[
            pl.BlockSpec((t, width), lambda i: (i, 0)),
            pl.BlockSpec((None, hb, LANES, LANES), lambda i: (i, 0, 0, 0)),
        ],
        out_shape=[
            jax.ShapeDtypeStruct((n * t, RET_WIDTH), BF16),
            jax.ShapeDtypeStruct((n, hb, LANES, LANES), F32),
        ],
        compiler_params=_cp(("parallel",)),
        name="retention",
    )(*args)


HGRN_SUB = 16


def _hgrn_body(*refs, has_state, nch, c):
    if has_state:
        q_ref, f_ref, v_ref, g_ref, lb_ref, gain_ref, s0_ref, o_ref, s_ref, st_sc, b_sc, k_sc, v_sc, o_sc = refs
    else:
        q_ref, f_ref, v_ref, g_ref, lb_ref, gain_ref, o_ref, s_ref, st_sc, b_sc, k_sc, v_sc, o_sc = refs
    ci = pl.program_id(1)
    heads = range(HGRN_HEADS)
    cols = lambda hh: slice(hh * HEAD_DIM, (hh + 1) * HEAD_DIM)

    @pl.when(ci == 0)
    def _():
        for hh in heads:
            st_sc[hh] = s0_ref[hh].T if has_state else jnp.zeros((LANES, LANES), F32)

    ri = lax.broadcasted_iota(jnp.int32, (c, c), 0)
    si = lax.broadcasted_iota(jnp.int32, (c, c), 1)
    tri = jnp.where(ri >= si, 1.0, 0.0).astype(BF16)
    q, kk, v, b, st = [], [], [], [], []
    for hh in heads:
        lb = lb_ref[:, cols(hh)]
        f = lb + (1.0 - lb) * _sigmoid(f_ref[:, cols(hh)])
        logf = jnp.log(f)
        hi = logf.astype(BF16)
        r1 = logf - hi.astype(F32)
        mid = r1.astype(BF16)
        lo = (r1 - mid.astype(F32)).astype(BF16)
        b.append(jnp.dot(tri, hi, preferred_element_type=F32) + jnp.dot(tri, mid, preferred_element_type=F32)
                 + jnp.dot(tri, lo, preferred_element_type=F32))
        q.append(q_ref[:, cols(hh)])
        kk.append(1.0 - f)
        v.append(v_ref[:, cols(hh)])
        b_sc[hh] = b[hh]
        k_sc[hh] = kk[hh]
        v_sc[hh] = v[hh]
        st.append(st_sc[hh])
        o_sc[hh] = _dot_nt(q[hh] * jnp.exp(b[hh]), st[hh])

    sub = min(HGRN_SUB, c)
    row_id = lax.broadcasted_iota(jnp.int32, (sub, LANES), 0)
    for blk in range(c // sub):
        r0 = blk * sub
        b_i = [b[hh][r0:r0 + sub] for hh in heads]
        q_i = [q[hh][r0:r0 + sub] for hh in heads]
        acc = [jnp.zeros((sub, LANES), F32) for _ in heads]
        if blk > 0:
            for hh in heads:
                ref = b_sc[hh, r0 - 1:r0, :]
                qt = q_i[hh] * jnp.exp(b_i[hh] - ref)
                kt = kk[hh][0:r0] * jnp.exp(ref - b[hh][0:r0])
                acc[hh] = _dot(_dot_nt(qt, kt), v[hh][0:r0])
        for s in range(sub):
            for hh in heads:
                b_s = b_sc[hh, r0 + s:r0 + s + 1, :]
                k_s = k_sc[hh, r0 + s:r0 + s + 1, :]
                v_s = v_sc[hh, r0 + s:r0 + s + 1, :]
                w = q_i[hh] * jnp.exp(jnp.minimum(b_i[hh] - b_s, 0.0)) * k_s
                w = jnp.where(row_id >= s, w, 0.0)
                acc[hh] = acc[hh] + jnp.sum(w, axis=-1, keepdims=True) * v_s
        for hh in heads:
            o_sc[hh, r0:r0 + sub, :] += acc[hh]

    for hh in heads:
        b_last = b_sc[hh, c - 1:c, :]
        st_sc[hh] = st[hh] * jnp.exp(b_last) + _dot_tn(v[hh], kk[hh] * jnp.exp(b_last - b[hh]))
        g = g_ref[:, cols(hh)]
        o_ref[:, cols(hh)] = (_rms(o_sc[hh]) * gain_ref[:, cols(hh)] * (g * _sigmoid(g))).astype(o_ref.dtype)

    @pl.when(ci == nch - 1)
    def _():
        for hh in heads:
            s_ref[hh] = st_sc[hh].T


def _hgrn(zplain, lb, gain, s0, li, *, n, t, c):
    nch = t // c
    hb = HGRN_HEADS
    width = hb * HEAD_DIM
    first = PLAIN_HGRN_BLK * LANES // width
    row = lambda w: (lambda i, j: (i * nch + j, first + w))
    const = lambda i, j: (0, 0)
    in_specs = [pl.BlockSpec((c, width), row(w)) for w in range(4)]
    in_specs += [pl.BlockSpec((1, width), const), pl.BlockSpec((1, width), const)]
    args = [zplain] * 4 + [lb, gain]
    if s0 is not None:
        in_specs.append(pl.BlockSpec((None, None, hb, LANES, LANES), lambda i, j: (li, i, 0, 0, 0)))
        args.append(s0)
    return pl.pallas_call(
        functools.partial(_hgrn_body, has_state=s0 is not None, nch=nch, c=c),
        grid=(n, nch),
        in_specs=in_specs,
        out_specs=[
            pl.BlockSpec((c, width), lambda i, j: (i * nch + j, 0)),
            pl.BlockSpec((None, hb, LANES, LANES), lambda i, j: (i, 0, 0, 0)),
        ],
        out_shape=[
            jax.ShapeDtypeStruct((n * t, HGRN_WIDTH), BF16),
            jax.ShapeDtypeStruct((n, hb, LANES, LANES), F32),
        ],
        scratch_shapes=[pltpu.VMEM((hb, LANES, LANES), F32)] + [pltpu.VMEM((hb, c, LANES), F32)] * 4,
        compiler_params=_cp(("parallel", "arbitrary")),
        name="hgrn",
    )(*args)


def _lb_body(x_ref, o_ref):
    x = x_ref[...]
    e = jnp.exp(x - jnp.max(x, axis=0, keepdims=True))
    sm = e / jnp.sum(e, axis=0, keepdims=True)
    acc = jnp.zeros_like(sm[0:1])
    o_ref[0:1, :] = acc
    for layer in range(1, x.shape[0]):
        acc = acc + sm[layer:layer + 1]
        o_ref[layer:layer + 1, :] = acc


def _hgrn_lower_bounds(hgrn_lb):
    return pl.pallas_call(
        _lb_body,
        out_shape=jax.ShapeDtypeStruct(hgrn_lb.shape, F32),
        name="hgrn_lb",
    )(hgrn_lb.astype(F32))


def _group_sums(x, p0, p1):
    rows, width = x.shape
    xg = x.reshape(rows // CMP_STRIDE, CMP_STRIDE, width)
    return jnp.sum(xg * p0[None], axis=1), jnp.sum(xg * p1[None], axis=1)


def _cmp_partial_body(x_ref, p0_ref, p1_ref, a0_ref, a1_ref):
    a0, a1 = _group_sums(x_ref[...], p0_ref[...], p1_ref[...])
    a0_ref[...] = a0
    a1_ref[...] = a1


def _cmp_partial(kvrows, p0, p1, *, rb):
    m = kvrows.shape[0]
    w = p0.shape[1]
    const = lambda i: (0, 0)
    out = jax.ShapeDtypeStruct((m // CMP_STRIDE, w), F32)
    return pl.pallas_call(
        _cmp_partial_body,
        grid=(m // rb,),
        in_specs=[pl.BlockSpec((rb, w), lambda i: (i, 0)), pl.BlockSpec(p0.shape, const), pl.BlockSpec(p1.shape, const)],
        out_specs=[pl.BlockSpec((rb // CMP_STRIDE, w), lambda i: (i, 0))] * 2,
        out_shape=[out, out],
        compiler_params=_cp(("parallel",)),
        name="cmp_partial",
    )(kvrows, p0, p1)


CACHE_SUB = 4 * NSA_KV_HEADS


def _page_group_sums(x_refs, p0_ref, p1_ref, a0_ref, a1_ref, page):
    gp = page // CMP_STRIDE
    for k, x_ref in enumerate(x_refs):
        for s in range(2 * NSA_KV_HEADS):
            sl = slice(s * HEAD_DIM, (s + 1) * HEAD_DIM)
            x = x_ref[pl.ds(s, page, stride=CACHE_SUB), :]
            a0, a1 = _group_sums(x, p0_ref[:, sl], p1_ref[:, sl])
            a0_ref[k * gp:(k + 1) * gp, sl] = a0
            a1_ref[k * gp:(k + 1) * gp, sl] = a1


def _cmp_partial_paged_body(pt_ref, *refs, pages, page):
    p0_ref, p1_ref, a0_ref, a1_ref = refs[pages:]
    _page_group_sums(refs[:pages], p0_ref, p1_ref, a0_ref, a1_ref, page)


def _proj_cache_sums_body(pt_ref, u_ref, w_ref, *refs, pages, page, units, nj):
    p0_ref, p1_ref, o_ref, a0_ref, a1_ref = refs[pages:]
    o_ref[...] = _dot_nt(u_ref[...], w_ref[...])

    @pl.when(pl.program_id(0) * nj + pl.program_id(1) < units)
    def _():
        _page_group_sums(refs[:pages], p0_ref, p1_ref, a0_ref, a1_ref, page)


def _proj_with_cache_sums(u, w, li, cache_rows, page_table, p0, p1, *, n, tm, tn, pages):
    m, k = u.shape
    nd, n_pages = page_table.shape
    page = cache_rows.shape[2] // CACHE_SUB
    width = p0.shape[1]
    gp = page // CMP_STRIDE
    chunks = n_pages // pages
    units = nd * chunks
    ni, nj = m // tm, n // tn
    unit = lambda i, j: jnp.minimum(i * nj + j, units - 1)
    const = lambda i, j, pt: (0, 0)
    page_spec = lambda kk: pl.BlockSpec(
        (None, None, page * CACHE_SUB, HEAD_DIM),
        lambda i, j, pt: (li, pt[unit(i, j) // chunks, (unit(i, j) % chunks) * pages + kk], 0, 0))
    sums_spec = pl.BlockSpec((None, pages * gp, width), lambda i, j, pt: (unit(i, j) // chunks, unit(i, j) % chunks, 0))
    sums = jax.ShapeDtypeStruct((nd, n_pages * gp, width), F32)
    return pl.pallas_call(
        functools.partial(_proj_cache_sums_body, pages=pages, page=page, units=units, nj=nj),
        grid_spec=pltpu.PrefetchScalarGridSpec(
            num_scalar_prefetch=1,
            grid=(ni, nj),
            in_specs=[pl.BlockSpec((tm, k), lambda i, j, pt: (i, 0)),
                      pl.BlockSpec((None, tn, k), lambda i, j, pt: (li, j, 0))]
            + [page_spec(kk) for kk in range(pages)]
            + [pl.BlockSpec(p0.shape, const), pl.BlockSpec(p1.shape, const)],
            out_specs=[pl.BlockSpec((tm, tn), lambda i, j, pt: (i, j)), sums_spec, sums_spec],
        ),
        out_shape=[jax.ShapeDtypeStruct((m, n), F32), sums, sums],
        compiler_params=_cp(("arbitrary", "arbitrary")),
        name="proj_plain_cache_sums",
    )(page_table, u, w, *([cache_rows] * pages), p0, p1)


def _cache_sum_pages(m, tm, n, tn, page_table):
    nd, n_pages = page_table.shape
    steps = (m // tm) * (n // tn)
    for pages in (4, 8, 16, 32):
        if n_pages % pages == 0 and nd * (n_pages // pages) <= steps:
            return pages
    return None


def _cmp_partial_paged(cache_rows, page_table, p0, p1, li, *, pages):
    n, n_pages = page_table.shape
    page = cache_rows.shape[2] // CACHE_SUB
    w = p0.shape[1]
    gp = page // CMP_STRIDE
    const = lambda i, c, pt: (0, 0)
    page_spec = lambda k: pl.BlockSpec((None, None, page * CACHE_SUB, HEAD_DIM),
                                       lambda i, c, pt: (li, pt[i, c * pages + k], 0, 0))
    out = jax.ShapeDtypeStruct((n, n_pages * gp, w), F32)
    return pl.pallas_call(
        functools.partial(_cmp_partial_paged_body, pages=pages, page=page),
        grid_spec=pltpu.PrefetchScalarGridSpec(
            num_scalar_prefetch=1,
            grid=(n, n_pages // pages),
            in_specs=[page_spec(k) for k in range(pages)] + [pl.BlockSpec(p0.shape, const), pl.BlockSpec(p1.shape, const)],
            out_specs=[pl.BlockSpec((None, pages * gp, w), lambda i, c, pt: (i, c, 0))] * 2,
        ),
        out_shape=[out, out],
        compiler_params=_cp(("parallel", "arbitrary")),
        name="cmp_partial_paged",
    )(page_table, *([cache_rows] * pages), p0, p1)


def _cmp_final_body(a0_ref, a1_ref, wk_ref, wv_ref, gain_ref, kc_ref, vc_ref):
    ng = a0_ref.shape[0]
    agg = a0_ref[...] + pltpu.roll(a1_ref[...], ng - 1, 0)
    for g in range(NSA_KV_HEADS):
        sl = slice(g * HEAD_DIM, (g + 1) * HEAD_DIM)
        ak = agg[:, sl]
        av = agg[:, NSA_KV_HEADS * HEAD_DIM + g * HEAD_DIM:NSA_KV_HEADS * HEAD_DIM + (g + 1) * HEAD_DIM]
        kc_ref[:, sl] = (_rms(_dot(ak, wk_ref[...])) * gain_ref[...]).astype(kc_ref.dtype)
        vc_ref[:, sl] = _dot(av, wv_ref[...]).astype(vc_ref.dtype)


def _cmp_final(a0, a1, w_k, w_v, gain):
    n, ng, w = a0.shape
    const = lambda i: (0, 0)
    out = jax.ShapeDtypeStruct((n, ng, NSA_KV_HEADS * HEAD_DIM), BF16)
    return pl.pallas_call(
        _cmp_final_body,
        grid=(n,),
        in_specs=[
            pl.BlockSpec((None, ng, w), lambda i: (i, 0, 0)),
            pl.BlockSpec((None, ng, w), lambda i: (i, 0, 0)),
            pl.BlockSpec(w_k.shape, const),
            pl.BlockSpec(w_v.shape, const),
            pl.BlockSpec(gain.shape, const),
        ],
        out_specs=[pl.BlockSpec((None, ng, NSA_KV_HEADS * HEAD_DIM), lambda i: (i, 0, 0))] * 2,
        out_shape=[out, out],
        compiler_params=_cp(("parallel",)),
        name="cmp_final",
    )(a0, a1, w_k, w_v, gain)


def _overlap_matrix(n_cmp_pad, n_slc, width):
    ci = np.arange(n_cmp_pad)[:, None]
    sj = np.arange(width)[None, :]
    c_start = ci * CMP_STRIDE
    s_start = sj * SEL_BLOCK
    hit = ((c_start < s_start + SEL_BLOCK) & (c_start + CMP_BLOCK > s_start)
           & (ci < n_cmp_pad - 1) & (sj < n_slc))
    return jnp.asarray(hit, BF16)


def _block_scores(imp, qpos_i, ov, n_slc):
    hi, lo = _split_bf16(imp)
    score = jnp.dot(hi, ov, preferred_element_type=F32) + jnp.dot(lo, ov, preferred_element_type=F32)
    sj = lax.broadcasted_iota(jnp.int32, score.shape, 1)
    cur = qpos_i // SEL_BLOCK
    forced = (sj == 0) | (sj == cur) | (sj == cur - 1)
    score = jnp.where(forced, FORCE_SCORE, score)
    score = jnp.where(sj * SEL_BLOCK <= qpos_i, score, -1.0)
    return jnp.where(sj < n_slc, score, -2.0)


def _masked_softmax(s, allow):
    s = jnp.where(allow, s, NEG_INF)
    m = jnp.max(s, axis=-1, keepdims=True)
    e = jnp.where(allow, jnp.exp(s - m), 0.0)
    return e / jnp.maximum(jnp.sum(e, axis=-1, keepdims=True), 1e-30)


SEL_KEY_BLOCK = 256


def _nsa_prompt_body(q_ref, zg_ref, kc_ref, vc_ref, ov_ref, ks_ref, vs_ref, kw_ref, vw_ref, o_ref, *, t_len, tq):
    g = pl.program_id(1)
    t0 = pl.program_id(2) * tq
    n_cmp_pad = kc_ref.shape[0]
    n_slc = -(-t_len // SEL_BLOCK)
    hpg = HEADS_PER_GROUP
    qpos_i = t0 + lax.broadcasted_iota(jnp.int32, (tq, 1), 0)
    slopes = [jnp.where(g == 0, 2.0 ** -(r + 1), 2.0 ** -(r + 1 + hpg)) for r in range(hpg)]
    slopes2 = [sl * LOG2E for sl in slopes]
    qs = [q_ref[:, r * HEAD_DIM:(r + 1) * HEAD_DIM] for r in range(hpg)]

    ci = lax.broadcasted_iota(jnp.int32, (tq, n_cmp_pad), 1)
    cdist_i = qpos_i - (ci * CMP_STRIDE + CMP_BLOCK - 1)
    callow = (cdist_i >= 0) & (ci < n_cmp_pad - 1)
    cdist = cdist_i.astype(F32)
    kc = kc_ref[...]
    vc = vc_ref[...]
    imp = jnp.zeros((tq, n_cmp_pad), F32)
    o_cmp = []
    for r in range(hpg):
        p = _masked_softmax(_dot_nt(qs[r], kc) * ATT_SCALE - slopes[r] * cdist, callow)
        o_cmp.append(_dot(p, vc))
        imp = imp + p

    score = _block_scores(imp, qpos_i, ov_ref[...], n_slc)
    s_t = score.T[0:n_slc]
    jrow = lax.broadcasted_iota(jnp.int32, (n_slc, tq), 0)
    rank = jnp.zeros((n_slc, tq), jnp.int32)
    for jp in range(n_slc):
        row = s_t[jp:jp + 1, :]
        beats = (row > s_t) | ((row == s_t) & (jp < jrow))
        rank = rank + jnp.where(beats, 1, 0)
    sel_t = jnp.where((rank < N_SEL) & (s_t >= 0.0), 1.0, 0.0)
    sel_t = jnp.concatenate([sel_t, jnp.zeros((LANES - n_slc, tq), F32)], axis=0)
    sel = sel_t.T.astype(BF16)

    kb = SEL_KEY_BLOCK
    nkb = (t0 + tq + kb - 1) // kb

    def sel_step(i, carry):
        k0 = pl.multiple_of(i * kb, kb)
        kblk = ks_ref[pl.ds(k0, kb), :].astype(BF16)
        vblk = vs_ref[pl.ds(k0, kb), :].astype(BF16)
        kpos_i = k0 + lax.broadcasted_iota(jnp.int32, (1, kb), 1)
        ej = lax.broadcasted_iota(jnp.int32, (LANES, kb), 0)
        ec = lax.broadcasted_iota(jnp.int32, (LANES, kb), 1)
        expand = jnp.where(ej == (k0 + ec) // SEL_BLOCK, 1.0, 0.0).astype(BF16)
        allow = (jnp.dot(sel, expand, preferred_element_type=F32) > 0.5) & (kpos_i <= qpos_i)
        kpos = kpos_i.astype(F32)
        out = []
        for r in range(hpg):
            m_old, l_old, acc_old = carry[3 * r:3 * r + 3]
            s = jnp.where(allow, _dot_nt(qs[r], kblk) * (ATT_SCALE * LOG2E) + slopes2[r] * kpos, NEG_INF)
            m_new = jnp.maximum(m_old, jnp.max(s, axis=-1, keepdims=True))
            alpha = jnp.exp2(m_old - m_new)
            e = jnp.exp2(s - m_new)
            out += [m_new, alpha * l_old + jnp.sum(e, axis=-1, keepdims=True),
                    alpha * acc_old + _dot(e, vblk)]
        return tuple(out)

    init = (jnp.full((tq, 1), NEG_INF, F32), jnp.zeros((tq, 1), F32), jnp.zeros((tq, HEAD_DIM), F32)) * hpg
    fin = lax.fori_loop(0, nkb, sel_step, init)
    o_slc = [fin[3 * r + 2] / fin[3 * r + 1] for r in range(hpg)]

    wk = WINDOW + tq
    ws = pl.multiple_of(jnp.clip(t0 - WINDOW, 0, t_len - wk), LANES)
    kw = kw_ref[pl.ds(ws, wk), :].astype(BF16)
    vw = vw_ref[pl.ds(ws, wk), :].astype(BF16)
    wpos_i = ws + lax.broadcasted_iota(jnp.int32, (1, wk), 1)
    wdist_i = qpos_i - wpos_i
    wallow = (wdist_i >= 0) & (wdist_i <= WINDOW)
    wpos = wpos_i.astype(F32)
    gates = _sigmoid(zg_ref[...])
    for r in range(hpg):
        s = jnp.where(wallow, _dot_nt(qs[r], kw) * (ATT_SCALE * LOG2E) + slopes2[r] * wpos, NEG_INF)
        e = jnp.exp2(s - jnp.max(s, axis=-1, keepdims=True))
        o_win = _dot(e, vw) / jnp.sum(e, axis=-1, keepdims=True)
        gate = [jnp.where(g == 0, gates[:, 3 * r + b:3 * r + b + 1],
                          gates[:, 3 * (r + hpg) + b:3 * (r + hpg) + b + 1]) for b in range(3)]
        o = gate[0] * o_cmp[r] + gate[1] * o_slc[r] + gate[2] * o_win
        o_ref[:, r * HEAD_DIM:(r + 1) * HEAD_DIM] = o.astype(o_ref.dtype)


def _nsa_prompt(q, zplain, kc, vc, kvrows, winrows, *, n, t, tq):
    nt = t // tq
    gw = HEADS_PER_GROUP * HEAD_DIM
    n_cmp_pad = kc.shape[1]
    g_blocks = NSA_KV_HEADS
    qrow = lambda i, g, j: (i * nt + j, g)
    seq = lambda blk: (lambda i, g, j: (i, blk + g))
    ov = _overlap_matrix(n_cmp_pad, -(-t // SEL_BLOCK), LANES)
    return pl.pallas_call(
        functools.partial(_nsa_prompt_body, t_len=t, tq=tq),
        grid=(n, NSA_KV_HEADS, nt),
        in_specs=[
            pl.BlockSpec((tq, gw), qrow),
            pl.BlockSpec((tq, LANES), lambda i, g, j: (i * nt + j, PLAIN_GATE_BLK)),
            pl.BlockSpec((None, n_cmp_pad, HEAD_DIM), lambda i, g, j: (i, 0, g)),
            pl.BlockSpec((None, n_cmp_pad, HEAD_DIM), lambda i, g, j: (i, 0, g)),
            pl.BlockSpec(ov.shape, lambda i, g, j: (0, 0)),
            pl.BlockSpec((t, HEAD_DIM), seq(2 * g_blocks)),
            pl.BlockSpec((t, HEAD_DIM), seq(3 * g_blocks)),
            pl.BlockSpec((t, HEAD_DIM), seq(0)),
            pl.BlockSpec((t, HEAD_DIM), seq(g_blocks)),
        ],
        out_specs=pl.BlockSpec((tq, gw), qrow),
        out_shape=jax.ShapeDtypeStruct((n * t, NSA_WIDTH), BF16),
        compiler_params=_cp(("parallel", "parallel", "arbitrary")),
        name="nsa_prompt",
    )(q, zplain, kc, vc, ov, kvrows, kvrows, winrows, winrows)


KV_OFF = NSA_WIDTH
GATE_OFF = KV_OFF + KV_WIDTH
PLAIN_OFF = GATE_OFF + GATE_WIDTH
SLOT_WIDTH = NSA_KV_HEADS * HEAD_DIM


def _stacked_weights(w_in, cmp_w, w_branch, w_o, w_gu, w_down, w_ple_proj, w_ple_gate):
    w_t = jnp.swapaxes(w_in, 1, 2).astype(BF16)
    tail = PLAIN_WIDTH - PLAIN_GATE_BLK * LANES - GATE_WIDTH
    w_gate = jnp.pad(w_t[:, GATE_OFF:PLAIN_OFF], ((0, 0), (0, tail), (0, 0)))
    return dict(
        w_head=w_t,
        w_plain=jnp.concatenate([w_t[:, PLAIN_OFF:], w_gate], axis=1),
        w_phi=cmp_w.astype(BF16),
        w_branch=w_branch.astype(BF16),
        w_o=w_o.astype(BF16),
        w_gu=w_gu.astype(BF16),
        w_down=w_down.astype(BF16),
        w_ple_proj=w_ple_proj.astype(BF16),
        w_ple_gate=w_ple_gate.astype(BF16),
    )


def _layer_weights(li, stacked, norms, qk_gain, cmp_pos, out_gain, lb_all):
    ones = jnp.ones((SLOT_WIDTH,), F32)
    zeros = jnp.zeros((SLOT_WIDTH,), F32)
    gain = lambda i: jnp.tile(qk_gain[li, i], NSA_KV_HEADS)
    row = lambda v: v.reshape(1, -1).astype(F32)
    cp = cmp_pos[li]
    half = lambda w, m: w[m * CMP_STRIDE:(m + 1) * CMP_STRIDE]
    pos = lambda m: jnp.concatenate([half(cp[0], m)] * NSA_KV_HEADS + [half(cp[1], m)] * NSA_KV_HEADS, axis=1)
    w = dict(stacked)
    w.update(
        li=li,
        norm=[row(norms[li, i]) for i in range(4)],
        q_gain=row(jnp.tile(qk_gain[li, 0], NSA_HEADS)),
        q_flag=jnp.ones((1, NSA_WIDTH), F32),
        kv_gain=row(jnp.concatenate([ones, ones, gain(1), ones])),
        kv_flag=row(jnp.concatenate([zeros, zeros, ones, zeros])),
        win_gain=row(jnp.concatenate([gain(2), ones])),
        win_flag=row(jnp.concatenate([ones, zeros])),
        cmp_p0=pos(0).astype(F32),
        cmp_p1=pos(1).astype(F32),
        w_phi_k=stacked["w_phi"][li, 0],
        w_phi_v=stacked["w_phi"][li, 1],
        kc_gain=row(qk_gain[li, 3]),
        ret_gain=row(out_gain[li, 0]),
        hgrn_gain=row(out_gain[li, 1]),
        lb=lb_all[li:li + 1],
    )
    return w


def _tiles(m):
    big = 512 if m % 512 == 0 else m
    small = 256 if m % 256 == 0 else m
    return big, small


def _projections(u, w, tm, row_bufs, cache=None):
    li, depth = w["li"], w["depth"]
    head = functools.partial(_proj, u, w["w_head"], li, tm=tm)
    q = head(col0=0, n=NSA_WIDTH, tn=512, out_dtype=BF16, gain=w["q_gain"], flag=w["q_flag"], name="proj_q")
    kvrows, kv_buf = head(col0=KV_OFF, n=4 * SLOT_WIDTH, tn=4 * SLOT_WIDTH, out_dtype=F32, gain=w["kv_gain"],
                          flag=w["kv_flag"], name="proj_kv", rows_out=(depth, row_bufs[0]))
    winrows, win_buf = head(col0=KV_OFF + 4 * SLOT_WIDTH, n=2 * SLOT_WIDTH, tn=2 * SLOT_WIDTH, out_dtype=F32,
                            gain=w["win_gain"], flag=w["win_flag"], name="proj_win", rows_out=(depth, row_bufs[1]))
    tn = PLAIN_TILE
    pages = None if cache is None else _cache_sum_pages(u.shape[0], tm, PLAIN_WIDTH, tn, cache[1])
    if pages is None:
        zplain = _proj(u, w["w_plain"], li, col0=0, n=PLAIN_WIDTH, tm=tm, tn=tn, out_dtype=F32, name="proj_plain")
        return q, kvrows, winrows, zplain, (kv_buf, win_buf), None
    zplain, a0, a1 = _proj_with_cache_sums(u, w["w_plain"], li, cache[0], cache[1], w["cmp_p0"], w["cmp_p1"],
                                           n=PLAIN_WIDTH, tm=tm, tn=tn, pages=pages)
    return q, kvrows, winrows, zplain, (kv_buf, win_buf), (a0, a1)


def _layer_tail(h, o_nsa, o_ret, o_hgrn, zplain, p, w, tm, ts):
    li = w["li"]
    h = _merge(o_nsa, o_ret, o_hgrn, zplain, h, w["w_branch"], w["w_o"], li, tm=ts)
    h, u = _ffn(h, w["norm"][2], w["w_gu"], w["w_down"], w["norm"][3], li, 1, tm=tm, tf=512)
    return _ple(h, u, p, w["w_ple_gate"], w["w_ple_proj"], li, tm=ts)


def _prompt_layer(x, p, w, row_bufs, *, n, t, cache=None):
    tm, ts = _tiles(n * t)
    h, u = _ffn(x, w["norm"][0], w["w_gu"], w["w_down"], w["norm"][1], w["li"], 0, tm=tm, tf=512)
    q, kvrows, winrows, zplain, row_bufs, cache_sums = _projections(
        u, w, 1024 if (n * t) % 1024 == 0 else tm, row_bufs, cache)
    a0, a1 = _cmp_partial(kvrows, w["cmp_p0"], w["cmp_p1"], rb=min(1024, t))
    ng = t // CMP_STRIDE
    kc, vc = _cmp_final(a0.reshape(n, ng, -1), a1.reshape(n, ng, -1), w["w_phi_k"], w["w_phi_v"], w["kc_gain"])
    o_nsa = _nsa_prompt(q, zplain, kc, vc, kvrows, winrows, n=n, t=t, tq=2 * LANES)
    o_ret, s_ret = _retention(zplain, w["ret_gain"], None, 0, n=n, t=t, c=LANES)
    o_hgrn, s_hgrn = _hgrn(zplain, w["lb"], w["hgrn_gain"], None, 0, n=n, t=t, c=LANES)
    h = _layer_tail(h, o_nsa, o_ret, o_hgrn, zplain, p, w, tm, ts)
    return h, row_bufs, s_ret, s_hgrn, cache_sums


def _nsa_dec_cmp_body(q_ref, kc_ref, vc_ref, ov_ref, ocmp_ref, sel_ref, *, past, steps, n_slc):
    n, gb = q_ref.shape[0], q_ref.shape[1]
    nc = kc_ref.shape[1]
    hpg = HEADS_PER_GROUP
    rows = hpg * steps
    ri = lax.broadcasted_iota(jnp.int32, (rows, 1), 0)
    qpos_i = past + ri % steps
    ci = lax.broadcasted_iota(jnp.int32, (rows, nc), 1)
    dist_i = qpos_i - (ci * CMP_STRIDE + CMP_BLOCK - 1)
    allow = (dist_i >= 0) & (ci < nc - 1)
    dist = dist_i.astype(F32)
    imps = []
    for i in range(n):
        for g in range(gb):
            slope = jnp.exp2(-(ri // steps + 1 + g * hpg).astype(F32))
            sl = slice(g * HEAD_DIM, (g + 1) * HEAD_DIM)
            p = _masked_softmax(_dot_nt(q_ref[i, g], kc_ref[i, :, sl]) * ATT_SCALE - slope * dist, allow)
            ocmp_ref[i, g] = _dot(p, vc_ref[i, :, sl])
            imp = p[0:steps]
            for r in range(1, hpg):
                imp = imp + p[r * steps:(r + 1) * steps]
            imps.append(imp)
    imp_all = jnp.concatenate(imps, axis=0)

    width = ov_ref.shape[1]
    all_rows = n * gb * steps
    qpos_all = past + lax.broadcasted_iota(jnp.int32, (all_rows, 1), 0) % steps
    score = _block_scores(imp_all, qpos_all, ov_ref[...], n_slc)
    sj = lax.broadcasted_iota(jnp.int32, score.shape, 1)
    lane = lax.broadcasted_iota(jnp.int32, (all_rows, LANES), 1)
    picked = jnp.full((all_rows, LANES), -1, jnp.int32)
    for it in range(min(N_SEL, n_slc)):
        m = jnp.max(score, axis=-1, keepdims=True)
        idx = jnp.min(jnp.where(score == m, sj, width), axis=-1, keepdims=True)
        picked = jnp.where(lane == it, jnp.where(m >= 0.0, idx, -1), picked)
        score = jnp.where(sj == idx, -3.0, score)
    sel_ref[...] = picked


def _nsa_dec_cmp(q_rt, kc, vc, *, past, steps, n_slc):
    n, ng = kc.shape[0], kc.shape[1]
    gb = NSA_KV_HEADS
    ov = _overlap_matrix(ng, n_slc, -(-n_slc // LANES) * LANES)
    o_cmp, sel = pl.pallas_call(
        functools.partial(_nsa_dec_cmp_body, past=past, steps=steps, n_slc=n_slc),
        out_shape=[
            jax.ShapeDtypeStruct((n, gb, HEADS_PER_GROUP * steps, HEAD_DIM), F32),
            jax.ShapeDtypeStruct((n * gb * steps, LANES), jnp.int32),
        ],
        compiler_params=pltpu.CompilerParams(vmem_limit_bytes=VMEM_LIMIT),
        name="nsa_dec_cmp",
    )(q_rt, kc, vc, ov)
    return o_cmp, sel.reshape(n, gb, steps, LANES)


ROW_PAD = 8


def _nsa_dec_attend_body(pt_ref, sel_ref, q_ref, gate_ref, ocmp_ref, kvn_ref, wn_ref, wp_ref, *refs,
                         past, steps, n_sel, past_blocks):
    gb = NSA_KV_HEADS
    blk_refs = refs[:gb * n_sel]
    o_ref = refs[gb * n_sel]
    i = pl.program_id(0)
    t = pl.program_id(1)
    qpos = past + t
    head = jnp.minimum(lax.broadcasted_iota(jnp.int32, (ROW_PAD, 1), 0), HEADS_PER_GROUP - 1)
    new_i = lax.broadcasted_iota(jnp.int32, (1, steps), 1)
    new_allow = new_i <= t
    new_dist = (t - new_i).astype(F32)
    cols = n_sel * SEL_BLOCK
    col = lax.broadcasted_iota(jnp.int32, (1, cols), 1)
    col_pick = col // SEL_BLOCK
    head_cols = lambda ref, blk: ref[:, blk * HEAD_DIM:(blk + 1) * HEAD_DIM]

    def two_part_attention(s_a, allow_a, v_a, s_b, allow_b, v_b):
        s_a = jnp.where(allow_a, s_a, NEG_INF)
        s_b = jnp.where(allow_b, s_b, NEG_INF)
        m = jnp.maximum(jnp.max(s_a, axis=-1, keepdims=True), jnp.max(s_b, axis=-1, keepdims=True))
        e_a = jnp.where(allow_a, jnp.exp(s_a - m), 0.0)
        e_b = jnp.where(allow_b, jnp.exp(s_b - m), 0.0)
        l = jnp.sum(e_a, axis=-1, keepdims=True) + jnp.sum(e_b, axis=-1, keepdims=True)
        return (_dot(e_a, v_a) + _dot(e_b, v_b)) / l

    for g in range(gb):
        q = q_ref[g]
        slope = jnp.exp2(-(head + 1 + g * HEADS_PER_GROUP).astype(F32))
        picked = [blk_refs[g * n_sel + k] for k in range(n_sel)]
        kall = jnp.concatenate([r[pl.ds(2 * gb + g, SEL_BLOCK, stride=CACHE_SUB), :] for r in picked], axis=0)
        vall = jnp.concatenate([r[pl.ds(3 * gb + g, SEL_BLOCK, stride=CACHE_SUB), :] for r in picked], axis=0)
        base = ((i * gb + g) * steps + t) * n_sel
        blk = jnp.zeros((1, cols), jnp.int32)
        for k in range(n_sel):
            blk = jnp.where(col_pick == k, sel_ref[base + k], blk)
        dist_i = qpos - (blk * SEL_BLOCK + col % SEL_BLOCK)
        allow = (blk >= 0) & (blk < past_blocks) & (dist_i >= 0)
        s_sel = _dot_nt(q, kall) * ATT_SCALE - slope * dist_i.astype(F32)
        s_new = _dot_nt(q, head_cols(kvn_ref, 2 * gb + g)) * ATT_SCALE - slope * new_dist
        o_slc = two_part_attention(s_sel, allow, vall, s_new, new_allow, head_cols(kvn_ref, 3 * gb + g))

        wb = wp_ref.shape[0] // (2 * gb)
        kwp = wp_ref[pl.ds(g, wb, stride=2 * gb), :]
        vwp = wp_ref[pl.ds(gb + g, wb, stride=2 * gb), :]
        wdist_i = qpos - (past - wb + lax.broadcasted_iota(jnp.int32, (1, wb), 1))
        wallow = (wdist_i >= 0) & (wdist_i <= WINDOW)
        s_wp = _dot_nt(q, kwp) * ATT_SCALE - slope * wdist_i.astype(F32)
        s_wn = _dot_nt(q, head_cols(wn_ref, g)) * ATT_SCALE - slope * new_dist
        o_win = two_part_attention(s_wp, wallow, vwp, s_wn, new_allow, head_cols(wn_ref, gb + g))

        gates = _sigmoid(gate_ref[g])
        o = gates[:, 0:1] * ocmp_ref[g] + gates[:, 1:2] * o_slc + gates[:, 2:3] * o_win
        o_ref[g] = o.astype(o_ref.dtype)


def _nsa_dec_attend(page_table, sel_flat, q_tr, gates_tr, ocmp_tr, kvrows, winrows, cache_half, win_rows_view, li, *,
                    past, steps, n_sel):
    n = q_tr.shape[0]
    rows = steps * ROW_PAD
    gb = NSA_KV_HEADS
    halves = cache_half.shape[2]
    past_blocks = past // SEL_BLOCK
    step = lambda i, t, pt, sel: (i, 0, t, 0)
    new = lambda i, t, pt, sel: (i, 0)

    def gather(g, k):
        def index(i, t, pt, sel):
            j = jnp.clip(sel[((i * gb + g) * steps + t) * n_sel + k], 0, past_blocks - 1)
            return (li, pt[i, j // halves], j % halves, 0, 0)
        return pl.BlockSpec((None, None, None, SEL_BLOCK * CACHE_SUB, HEAD_DIM), index)

    in_specs = [
        pl.BlockSpec((None, gb, ROW_PAD, HEAD_DIM), step),
        pl.BlockSpec((None, gb, ROW_PAD, LANES), step),
        pl.BlockSpec((None, gb, ROW_PAD, HEAD_DIM), step),
        pl.BlockSpec((steps, kvrows.shape[1]), new),
        pl.BlockSpec((steps, winrows.shape[1]), new),
        pl.BlockSpec((None, None, win_rows_view.shape[2], HEAD_DIM), lambda i, t, pt, sel: (li, i, 0, 0)),
    ]
    in_specs += [gather(g, k) for g in range(gb) for k in range(n_sel)]
    return pl.pallas_call(
        functools.partial(_nsa_dec_attend_body, past=past, steps=steps, n_sel=n_sel, past_blocks=past_blocks),
        grid_spec=pltpu.PrefetchScalarGridSpec(
            num_scalar_prefetch=2,
            grid=(n, steps),
            in_specs=in_specs,
            out_specs=pl.BlockSpec((None, gb, ROW_PAD, HEAD_DIM), step),
        ),
        out_shape=jax.ShapeDtypeStruct((n, gb, rows, HEAD_DIM), BF16),
        compiler_params=_cp(("parallel", "arbitrary")),
        name="nsa_dec_attend",
    )(page_table, sel_flat, q_tr, gates_tr, ocmp_tr, kvrows, winrows, win_rows_view, *([cache_half] * (gb * n_sel)))


def _decode_layer(x, p, w, row_bufs, cache_rows, cache_half, win_rows_view, state_ret, state_hgrn, page_table,
                  cache_sums, *, n, steps):
    m = n * steps
    li = w["li"]
    hpg, gb = HEADS_PER_GROUP, NSA_KV_HEADS
    past = page_table.shape[1] * (cache_rows.shape[2] // CACHE_SUB)
    n_slc = -(-(past + steps) // SEL_BLOCK)
    n_sel = min(N_SEL, n_slc)
    h, u = _ffn(x, w["norm"][0], w["w_gu"], w["w_down"], w["norm"][1], li, 0, tm=m, tf=512)
    q, kvrows, winrows, zplain, row_bufs, _ = _projections(u, w, m, row_bufs)
    if cache_sums is None:
        cache_sums = _cmp_partial_paged(cache_rows, page_table, w["cmp_p0"], w["cmp_p1"], li, pages=8)
    kc, vc = _cmp_final(*cache_sums, w["w_phi_k"], w["w_phi_v"], w["kc_gain"])
    q5 = q.reshape(n, steps, gb, hpg, HEAD_DIM)
    q_rt = q5.transpose(0, 2, 3, 1, 4).reshape(n, gb, hpg * steps, HEAD_DIM)
    o_cmp, sel = _nsa_dec_cmp(q_rt, kc, vc, past=past, steps=steps, n_slc=n_slc)
    pad_heads = lambda a: jnp.pad(a, ((0, 0), (0, 0), (0, 0), (0, ROW_PAD - hpg), (0, 0)))
    rows = steps * ROW_PAD
    q_tr = pad_heads(q5.transpose(0, 2, 1, 3, 4)).reshape(n, gb, rows, HEAD_DIM)
    ocmp_tr = pad_heads(o_cmp.reshape(n, gb, hpg, steps, HEAD_DIM).transpose(0, 1, 3, 2, 4)).reshape(
        n, gb, rows, HEAD_DIM)
    zg = zplain[:, PLAIN_GATE_BLK * LANES:PLAIN_GATE_BLK * LANES + GATE_WIDTH].reshape(n, steps, gb, hpg, 3)
    gates_tr = pad_heads(zg.transpose(0, 2, 1, 3, 4)).reshape(n, gb, rows, 3)
    gates_tr = jnp.pad(gates_tr, ((0, 0), (0, 0), (0, 0), (0, LANES - 3)))
    sel_flat = sel[..., :n_sel].reshape(-1)
    o_tr = _nsa_dec_attend(page_table, sel_flat, q_tr, gates_tr, ocmp_tr, kvrows, winrows, cache_half, win_rows_view,
                           li, past=past, steps=steps, n_sel=n_sel)
    o_nsa = o_tr.reshape(n, gb, steps, ROW_PAD, HEAD_DIM)[:, :, :, :hpg].transpose(0, 2, 1, 3, 4).reshape(
        m, NSA_WIDTH)
    o_ret, s_ret = _retention(zplain, w["ret_gain"], state_ret, li, n=n, t=steps, c=steps)
    o_hgrn, s_hgrn = _hgrn(zplain, w["lb"], w["hgrn_gain"], state_hgrn, li, n=n, t=steps, c=steps)
    h = _layer_tail(h, o_nsa, o_ret, o_hgrn, zplain, p, w, m, m)
    return h, row_bufs, s_ret, s_hgrn


def kernel(x_prompt, x_sample, cache_kv, cache_win, state_ret, state_hgrn, page_table, p_prompt, p_sample, norms,
           w_in, qk_gain, cmp_pos, cmp_w, out_gain, hgrn_lb, w_branch, w_o, w_gu, w_down, w_ple_proj, w_ple_gate):
    n_p, t_p, d = x_prompt.shape
    n_d, t_d, _ = x_sample.shape
    depth, n_pool, page = cache_kv.shape[:3]
    cache_rows = cache_kv.reshape(depth, n_pool, page * CACHE_SUB, HEAD_DIM)
    cache_half = cache_kv.reshape(depth, n_pool, page // SEL_BLOCK, SEL_BLOCK * CACHE_SUB, HEAD_DIM)
    wb = cache_win.shape[2]
    win_rows_view = cache_win.reshape(depth, n_d, wb * 2 * NSA_KV_HEADS, HEAD_DIM)
    lb_all = _hgrn_lower_bounds(hgrn_lb)
    stacked = _stacked_weights(w_in, cmp_w, w_branch, w_o, w_gu, w_down, w_ple_proj, w_ple_gate)
    hp = x_prompt.reshape(n_p * t_p, d)
    hs = x_sample.reshape(n_d * t_d, d)
    bufs_p = bufs_s = (None, None)
    states = [[] for _ in range(4)]
    for li in range(depth):
        w = _layer_weights(li, stacked, norms, qk_gain, cmp_pos, out_gain, lb_all)
        w["depth"] = depth
        hp, bufs_p, s_r, s_h, cache_sums = _prompt_layer(hp, p_prompt[li].reshape(n_p * t_p, -1), w, bufs_p,
                                                         n=n_p, t=t_p, cache=(cache_rows, page_table))
        states[0].append(s_r)
        states[2].append(s_h)
        hs, bufs_s, s_r, s_h = _decode_layer(hs, p_sample[li].reshape(n_d * t_d, -1), w, bufs_s, cache_rows,
                                             cache_half, win_rows_view, state_ret, state_hgrn, page_table,
                                             cache_sums, n=n_d, steps=t_d)
        states[1].append(s_r)
        states[3].append(s_h)
    gb = NSA_KV_HEADS
    kv_p = bufs_p[0].reshape(depth, n_p, t_p, 4, gb, HEAD_DIM)
    kv_s = bufs_s[0].reshape(depth, n_d, t_d, 4, gb, HEAD_DIM)
    win_p = bufs_p[1].reshape(depth, n_p, t_p, 2, gb, HEAD_DIM)[:, :, t_p - min(WINDOW, t_p):]
    win_all = jnp.concatenate([cache_win, bufs_s[1].reshape(depth, n_d, t_d, 2, gb, HEAD_DIM)], axis=2)
    win_s = win_all[:, :, wb + t_d - min(WINDOW, wb + t_d):]
    ret_p, ret_s, hg_p, hg_s = (jnp.stack(s) for s in states)
    return (hp.reshape(n_p, t_p, d), hs.reshape(n_d, t_d, d), kv_p, kv_s, win_p, win_s, ret_p, ret_s, hg_p, hg_s)
```

```python
import functools
import math

import jax
import jax.numpy as jnp
import numpy as np
from jax import lax
from jax.experimental import pallas as pl
from jax.experimental.pallas import tpu as pltpu

F32 = jnp.float32
BF16 = jnp.bfloat16

D_MODEL = 2048
HEAD_DIM = 128
NSA_HEADS = 8
NSA_KV_HEADS = 2
HEADS_PER_GROUP = NSA_HEADS // NSA_KV_HEADS
CMP_BLOCK = 32
CMP_STRIDE = 16
SEL_BLOCK = 64
N_SEL = 16
WINDOW = 512
RET_HEADS = 4
HGRN_HEADS = 4
D_FF = 5632
PLE_DIM = 256
NSA_WIDTH = NSA_HEADS * HEAD_DIM
RET_WIDTH = RET_HEADS * HEAD_DIM
HGRN_WIDTH = HGRN_HEADS * HEAD_DIM
KV_WIDTH = 6 * NSA_KV_HEADS * HEAD_DIM
GATE_WIDTH = 3 * NSA_HEADS
NEG_INF = -1e30
FORCE_SCORE = 1e6
EPS = 1e-6
ATT_SCALE = HEAD_DIM ** -0.5
LOG2E = math.log2(math.e)

LANES = 128
PLAIN_RET_BLK = 0
PLAIN_HGRN_BLK = (4 * RET_WIDTH) // LANES
PLAIN_MERGE_OFF = 4 * RET_WIDTH + 4 * HGRN_WIDTH
PLAIN_GATE_BLK = (PLAIN_MERGE_OFF + 3 * D_MODEL) // LANES
PLAIN_WIDTH = PLAIN_MERGE_OFF + 3 * D_MODEL + LANES
PLAIN_TILE = PLAIN_WIDTH // 9

VMEM_LIMIT = 56 * 1024 * 1024


def _cp(sem, vmem=VMEM_LIMIT):
    return pltpu.CompilerParams(dimension_semantics=sem, vmem_limit_bytes=vmem)


def _rms(x):
    return x * lax.rsqrt(jnp.mean(x * x, axis=-1, keepdims=True) + EPS)


def _dot(a, b):
    return jnp.dot(a.astype(BF16), b.astype(BF16), preferred_element_type=F32)


def _dot_nt(a, b):
    return lax.dot_general(a.astype(BF16), b.astype(BF16), (((1,), (1,)), ((), ())),
                           preferred_element_type=F32)


def _dot_tn(a, b):
    rows = a.shape[0]
    if rows % LANES:
        pad = LANES - rows % LANES
        a = jnp.concatenate([a, jnp.zeros((pad, a.shape[1]), a.dtype)], axis=0)
        b = jnp.concatenate([b, jnp.zeros((pad, b.shape[1]), b.dtype)], axis=0)
    return _dot(a.T, b)


def _split_bf16(x):
    hi = x.astype(BF16)
    lo = (x - hi.astype(F32)).astype(BF16)
    return hi, lo


def _sigmoid(x):
    return 1.0 / (1.0 + jnp.exp(-x))


def _ffn_body(x_ref, g1_ref, wg_ref, wv_ref, wd_ref, g2_ref, h_ref, u_ref, *rest, nf, emit):
    xn_sc, acc_sc = rest[-2:]
    j = pl.program_id(1)

    @pl.when(j == 0)
    def _():
        xn_sc[...] = (_rms(x_ref[...]) * g1_ref[...]).astype(BF16)
        acc_sc[...] = jnp.zeros_like(acc_sc)

    wg, wv, wd = wg_ref[...].astype(BF16), wv_ref[...].astype(BF16), wd_ref[...].astype(BF16)
    if emit:
        rest[0][...], rest[1][...], rest[2][...] = wg, wv, wd
    xn = xn_sc[...]
    g = jnp.dot(xn, wg, preferred_element_type=F32)
    v = jnp.dot(xn, wv, preferred_element_type=F32)
    a = (g * _sigmoid(g) * v).astype(BF16)
    acc_sc[...] += jnp.dot(a, wd, preferred_element_type=F32)

    @pl.when(j == nf - 1)
    def _():
        h = x_ref[...] + 0.5 * acc_sc[...]
        h_ref[...] = h
        u_ref[...] = (_rms(h) * g2_ref[...]).astype(BF16)


def _ffn(x, g1, weights, g2, *, tm, tf, cast_from=None):
    m, d = x.shape
    row = lambda i, j: (i, 0)
    const = lambda i, j: (0, 0)
    out_specs = [pl.BlockSpec((tm, d), row), pl.BlockSpec((tm, d), row)]
    out_shape = [jax.ShapeDtypeStruct((m, d), F32), jax.ShapeDtypeStruct((m, d), BF16)]
    w_out_specs = [pl.BlockSpec((d, tf), lambda i, j: (0, j)), pl.BlockSpec((d, tf), lambda i, j: (0, j)),
                   pl.BlockSpec((tf, d), lambda i, j: (j, 0))]
    if cast_from is None:
        w_args = list(weights)
        f = weights[2].shape[0]
        nf = f // tf
        w_specs = w_out_specs
    else:
        assert m == tm, "weight tiles are cast and emitted once, by the only row tile"
        li, which = cast_from
        w_gu, w_down = weights
        f = w_down.shape[2]
        nf = f // tf
        w_args = [w_gu, w_gu, w_down]
        w_specs = [pl.BlockSpec((None, None, d, tf), lambda i, j: (li, which, 0, j)),
                   pl.BlockSpec((None, None, d, tf), lambda i, j: (li, which, 0, j + nf)),
                   pl.BlockSpec((None, None, tf, d), lambda i, j: (li, which, j, 0))]
        out_specs += w_out_specs
        out_shape += [jax.ShapeDtypeStruct((d, f), BF16), jax.ShapeDtypeStruct((d, f), BF16),
                      jax.ShapeDtypeStruct((f, d), BF16)]
    out = pl.pallas_call(
        functools.partial(_ffn_body, nf=nf, emit=cast_from is not None),
        grid=(m // tm, nf),
        in_specs=[pl.BlockSpec((tm, d), row), pl.BlockSpec((1, d), const)] + w_specs + [pl.BlockSpec((1, d), const)],
        out_specs=out_specs,
        out_shape=out_shape,
        scratch_shapes=[pltpu.VMEM((tm, d), BF16), pltpu.VMEM((tm, d), F32)],
        compiler_params=_cp(("parallel", "arbitrary")),
        name="ffn",
    )(x, g1, *w_args, g2)
    return out[0], out[1], tuple(out[2:])


def _proj_plain_body(u_ref, w_ref, o_ref):
    o_ref[...] = _dot_nt(u_ref[...], w_ref[...]).astype(o_ref.dtype)


def _proj_norm_body(u_ref, w_ref, gain_ref, flag_ref, *refs, tn, rows_out):
    o_ref = refs[-2] if rows_out else refs[-1]
    z = _dot_nt(u_ref[...], w_ref[...])
    chunks = tn // LANES
    for c in range(chunks):
        sl = slice(c * LANES, (c + 1) * LANES)
        zc = z[:, sl]
        normed = _rms(zc) * gain_ref[:, sl]
        val = jnp.where(flag_ref[:, sl] > 0.5, normed, zc)
        o_ref[:, sl] = val.astype(o_ref.dtype)
        if rows_out:
            refs[-1][pl.ds(c, z.shape[0], stride=chunks), :] = val


def _proj(u, w, li, *, col0, n, tm, tn, out_dtype, gain=None, flag=None, name, rows_out=None):
    m, k = u.shape
    cb = col0 // tn
    in_specs = [pl.BlockSpec((tm, k), lambda i, j: (i, 0)), pl.BlockSpec((None, tn, k), lambda i, j: (li, cb + j, 0))]
    args = [u, w]
    out_specs = pl.BlockSpec((tm, tn), lambda i, j: (i, j))
    out_shape = jax.ShapeDtypeStruct((m, n), out_dtype)
    aliases = {}
    if gain is None:
        body = _proj_plain_body
    else:
        body = functools.partial(_proj_norm_body, tn=tn, rows_out=rows_out is not None)
        in_specs += [pl.BlockSpec((1, tn), lambda i, j: (0, j)), pl.BlockSpec((1, tn), lambda i, j: (0, j))]
        args += [gain, flag]
    if rows_out is not None:
        assert n == tn, "the row-major store needs the whole width in one column tile"
        depth, prev = rows_out
        chunks = n // LANES
        out_specs = [out_specs, pl.BlockSpec((None, tm * chunks, LANES), lambda i, j: (li, i, 0))]
        out_shape = [out_shape, jax.ShapeDtypeStruct((depth, m * chunks, LANES), F32)]
        if prev is not None:
            in_specs.append(pl.BlockSpec(memory_space=pl.ANY))
            args.append(prev)
            aliases = {len(args) - 1: 1}
    return pl.pallas_call(
        body,
        grid=(m // tm, n // tn),
        in_specs=in_specs,
        out_specs=out_specs,
        out_shape=out_shape,
        input_output_aliases=aliases,
        compiler_params=_cp(("parallel", "arbitrary")),
        name=name,
    )(*args)


def _merge_body(oa_ref, ob_ref, oc_ref, m0_ref, m1_ref, m2_ref, h_ref, wb_ref, wo_ref, out_ref):
    ya = jnp.dot(oa_ref[...], wb_ref[0:NSA_WIDTH, :], preferred_element_type=F32)
    yb = jnp.dot(ob_ref[...], wb_ref[NSA_WIDTH:NSA_WIDTH + RET_WIDTH, :], preferred_element_type=F32)
    yc = jnp.dot(oc_ref[...], wb_ref[NSA_WIDTH + RET_WIDTH:, :], preferred_element_type=F32)
    mixed = _sigmoid(m0_ref[...]) * ya + _sigmoid(m1_ref[...]) * yb + _sigmoid(m2_ref[...]) * yc
    out_ref[...] = h_ref[...] + jnp.dot(mixed.astype(BF16), wo_ref[...], preferred_element_type=F32)


def _merge(o_nsa, o_ret, o_hgrn, zplain, h, w_branch, w_o, li, *, tm):
    m, d = h.shape
    mb = PLAIN_MERGE_OFF // d
    row = lambda i: (i, 0)
    layer = lambda i: (li, 0, 0)
    return pl.pallas_call(
        _merge_body,
        grid=(m // tm,),
        in_specs=[
            pl.BlockSpec((tm, NSA_WIDTH), row),
            pl.BlockSpec((tm, RET_WIDTH), row),
            pl.BlockSpec((tm, HGRN_WIDTH), row),
            pl.BlockSpec((tm, d), lambda i: (i, mb)),
            pl.BlockSpec((tm, d), lambda i: (i, mb + 1)),
            pl.BlockSpec((tm, d), lambda i: (i, mb + 2)),
            pl.BlockSpec((tm, d), row),
            pl.BlockSpec((None,) + w_branch.shape[1:], layer),
            pl.BlockSpec((None,) + w_o.shape[1:], layer),
        ],
        out_specs=pl.BlockSpec((tm, d), row),
        out_shape=jax.ShapeDtypeStruct((m, d), F32),
        compiler_params=_cp(("parallel",)),
        name="merge",
    )(o_nsa, o_ret, o_hgrn, zplain, zplain, zplain, h, w_branch, w_o)


def _ple_body(h_ref, u_ref, p_ref, wg_ref, wp_ref, out_ref):
    gate = _sigmoid(jnp.dot(u_ref[...], wg_ref[...], preferred_element_type=F32))
    proj = jnp.dot(p_ref[...].astype(BF16), wp_ref[...], preferred_element_type=F32)
    out_ref[...] = h_ref[...] + gate * proj


def _ple(h, u, p, w_gate, w_proj, li, *, tm):
    m, d = h.shape
    row = lambda i: (i, 0)
    layer = lambda i: (li, 0, 0)
    return pl.pallas_call(
        _ple_body,
        grid=(m // tm,),
        in_specs=[
            pl.BlockSpec((tm, d), row),
            pl.BlockSpec((tm, d), row),
            pl.BlockSpec((tm, p.shape[1]), row),
            pl.BlockSpec((None,) + w_gate.shape[1:], layer),
            pl.BlockSpec((None,) + w_proj.shape[1:], layer),
        ],
        out_specs=pl.BlockSpec((tm, d), row),
        out_shape=jax.ShapeDtypeStruct((m, d), F32),
        compiler_params=_cp(("parallel",)),
        name="ple",
    )(h, u, p, w_gate, w_proj)


def _retention_tables(c):
    lg = np.log1p(-np.exp2(-5.0 - np.arange(RET_HEADS, dtype=np.float64)))
    i = np.arange(c, dtype=np.float64)
    diff = i[:, None] - i[None, :]
    dmat = np.where(diff >= 0, np.exp(lg[:, None, None] * np.maximum(diff, 0.0)), 0.0)
    q_dec = np.exp(lg[:, None] * (i + 1.0))[..., None] * np.ones((1, 1, HEAD_DIM))
    k_dec = np.exp(lg[:, None] * (c - 1.0 - i))[..., None] * np.ones((1, 1, HEAD_DIM))
    c_dec = np.exp(lg * c)[:, None, None] * np.ones((1, 8, HEAD_DIM))
    return tuple(jnp.asarray(a, F32) for a in (dmat, q_dec, k_dec, c_dec))


def _retention_body(*refs, has_state, nch, c):
    if has_state:
        q_ref, k_ref, v_ref, g_ref, dm_ref, qd_ref, kd_ref, cd_ref, gain_ref, s0_ref, o_ref, s_ref = refs
    else:
        q_ref, k_ref, v_ref, g_ref, dm_ref, qd_ref, kd_ref, cd_ref, gain_ref, o_ref, s_ref = refs
    heads = range(RET_HEADS)
    cols = lambda hh: slice(hh * HEAD_DIM, (hh + 1) * HEAD_DIM)

    def chunk(ci, states):
        rows = pl.ds(pl.multiple_of(ci * c, c), c)
        out = []
        for hh in heads:
            q = q_ref[rows, cols(hh)]
            k = k_ref[rows, cols(hh)] * ATT_SCALE
            v = v_ref[rows, cols(hh)]
            a = _dot_nt(q, k) * dm_ref[hh]
            o = _dot(a, v) + _dot(q * qd_ref[hh], states[hh])
            g = g_ref[rows, cols(hh)]
            o_ref[rows, cols(hh)] = (_rms(o) * gain_ref[:, cols(hh)] * (g * _sigmoid(g))).astype(o_ref.dtype)
            out.append(states[hh] * cd_ref[hh, 0:1, :] + _dot_tn(k * kd_ref[hh], v))
        return tuple(out)

    init = tuple(s0_ref[hh] if has_state else jnp.zeros((LANES, LANES), F32) for hh in heads)
    final = lax.fori_loop(0, nch, chunk, init)
    for hh in heads:
        s_ref[hh] = final[hh]


def _retention(zplain, gain, s0, li, *, n, t, c):
    nch = t // c
    tables = _retention_tables(c)
    hb = RET_HEADS
    width = hb * HEAD_DIM
    first = PLAIN_RET_BLK * LANES // width
    in_specs = [pl.BlockSpec((t, width), (lambda w: (lambda i: (i, first + w)))(w)) for w in range(4)]
    in_specs += [pl.BlockSpec(tab.shape, lambda i: (0, 0, 0)) for tab in tables]
    in_specs.append(pl.BlockSpec((1, width), lambda i: (0, 0)))
    args = [zplain] * 4 + list(tables) + [gain]
    if s0 is not None:
        in_specs.append(pl.BlockSpec((None, None, hb, LANES, LANES), lambda i: (li, i, 0, 0, 0)))
        args.append(s0)
    return pl.pallas_call(
        functools.partial(_retention_body, has_state=s0 is not None, nch=nch, c=c),
        grid=(n,),
        in_specs=in_specs,
        out_specs=[
            pl.BlockSpec((t, width), lambda i: (i, 0)),
            pl.BlockSpec((None, hb, LANES, LANES), lambda i: (i, 0, 0, 0)),
        ],
        out_shape=[
            jax.ShapeDtypeStruct((n * t, RET_WIDTH), BF16),
            jax.ShapeDtypeStruct((n, hb, LANES, LANES), F32),
        ],
        compiler_params=_cp(("parallel",)),
        name="retention",
    )(*args)


HGRN_SUB = 16


def _hgrn_body(*refs, has_state, nch, c):
    if has_state:
        q_ref, f_ref, v_ref, g_ref, lb_ref, gain_ref, s0_ref, o_ref, s_ref, st_sc, b_sc, k_sc, v_sc, o_sc = refs
    else:
        q_ref, f_ref, v_ref, g_ref, lb_ref, gain_ref, o_ref, s_ref, st_sc, b_sc, k_sc, v_sc, o_sc = refs
    ci = pl.program_id(1)
    heads = range(HGRN_HEADS)
    cols = lambda hh: slice(hh * HEAD_DIM, (hh + 1) * HEAD_DIM)

    @pl.when(ci == 0)
    def _():
        for hh in heads:
            st_sc[hh] = s0_ref[hh].T if has_state else jnp.zeros((LANES, LANES), F32)

    ri = lax.broadcasted_iota(jnp.int32, (c, c), 0)
    si = lax.broadcasted_iota(jnp.int32, (c, c), 1)
    tri = jnp.where(ri >= si, 1.0, 0.0).astype(BF16)
    q, kk, v, b, st = [], [], [], [], []
    for hh in heads:
        lb = lb_ref[:, cols(hh)]
        f = lb + (1.0 - lb) * _sigmoid(f_ref[:, cols(hh)])
        logf = jnp.log(f)
        hi = logf.astype(BF16)
        r1 = logf - hi.astype(F32)
        mid = r1.astype(BF16)
        lo = (r1 - mid.astype(F32)).astype(BF16)
        b.append(jnp.dot(tri, hi, preferred_element_type=F32) + jnp.dot(tri, mid, preferred_element_type=F32)
                 + jnp.dot(tri, lo, preferred_element_type=F32))
        q.append(q_ref[:, cols(hh)])
        kk.append(1.0 - f)
        v.append(v_ref[:, cols(hh)])
        b_sc[hh] = b[hh]
        k_sc[hh] = kk[hh]
        v_sc[hh] = v[hh]
        st.append(st_sc[hh])
        o_sc[hh] = _dot_nt(q[hh] * jnp.exp(b[hh]), st[hh])

    sub = min(HGRN_SUB, c)
    row_id = lax.broadcasted_iota(jnp.int32, (sub, LANES), 0)
    for blk in range(c // sub):
        r0 = blk * sub
        b_i = [b[hh][r0:r0 + sub] for hh in heads]
        q_i = [q[hh][r0:r0 + sub] for hh in heads]
        acc = [jnp.zeros((sub, LANES), F32) for _ in heads]
        if blk > 0:
            for hh in heads:
                ref = b_sc[hh, r0 - 1:r0, :]
                qt = q_i[hh] * jnp.exp(b_i[hh] - ref)
                kt = kk[hh][0:r0] * jnp.exp(ref - b[hh][0:r0])
                acc[hh] = _dot(_dot_nt(qt, kt), v[hh][0:r0])
        for s in range(sub):
            for hh in heads:
                b_s = b_sc[hh, r0 + s:r0 + s + 1, :]
                k_s = k_sc[hh, r0 + s:r0 + s + 1, :]
                v_s = v_sc[hh, r0 + s:r0 + s + 1, :]
                w = q_i[hh] * jnp.exp(jnp.minimum(b_i[hh] - b_s, 0.0)) * k_s
                w = jnp.where(row_id >= s, w, 0.0)
                acc[hh] = acc[hh] + jnp.sum(w, axis=-1, keepdims=True) * v_s
        for hh in heads:
            o_sc[hh, r0:r0 + sub, :] += acc[hh]

    for hh in heads:
        b_last = b_sc[hh, c - 1:c, :]
        st_sc[hh] = st[hh] * jnp.exp(b_last) + _dot_tn(v[hh], kk[hh] * jnp.exp(b_last - b[hh]))
        g = g_ref[:, cols(hh)]
        o_ref[:, cols(hh)] = (_rms(o_sc[hh]) * gain_ref[:, cols(hh)] * (g * _sigmoid(g))).astype(o_ref.dtype)

    @pl.when(ci == nch - 1)
    def _():
        for hh in heads:
            s_ref[hh] = st_sc[hh].T


def _hgrn(zplain, lb, gain, s0, li, *, n, t, c):
    nch = t // c
    hb = HGRN_HEADS
    width = hb * HEAD_DIM
    first = PLAIN_HGRN_BLK * LANES // width
    row = lambda w: (lambda i, j: (i * nch + j, first + w))
    const = lambda i, j: (0, 0)
    in_specs = [pl.BlockSpec((c, width), row(w)) for w in range(4)]
    in_specs += [pl.BlockSpec((1, width), const), pl.BlockSpec((1, width), const)]
    args = [zplain] * 4 + [lb, gain]
    if s0 is not None:
        in_specs.append(pl.BlockSpec((None, None, hb, LANES, LANES), lambda i, j: (li, i, 0, 0, 0)))
        args.append(s0)
    return pl.pallas_call(
        functools.partial(_hgrn_body, has_state=s0 is not None, nch=nch, c=c),
        grid=(n, nch),
        in_specs=in_specs,
        out_specs=[
            pl.BlockSpec((c, width), lambda i, j: (i * nch + j, 0)),
            pl.BlockSpec((None, hb, LANES, LANES), lambda i, j: (i, 0, 0, 0)),
        ],
        out_shape=[
            jax.ShapeDtypeStruct((n * t, HGRN_WIDTH), BF16),
            jax.ShapeDtypeStruct((n, hb, LANES, LANES), F32),
        ],
        scratch_shapes=[pltpu.VMEM((hb, LANES, LANES), F32)] + [pltpu.VMEM((hb, c, LANES), F32)] * 4,
        compiler_params=_cp(("parallel", "arbitrary")),
        name="hgrn",
    )(*args)


def _lb_body(x_ref, o_ref):
    x = x_ref[...]
    e = jnp.exp(x - jnp.max(x, axis=0, keepdims=True))
    sm = e / jnp.sum(e, axis=0, keepdims=True)
    acc = jnp.zeros_like(sm[0:1])
    o_ref[0:1, :] = acc
    for layer in range(1, x.shape[0]):
        acc = acc + sm[layer:layer + 1]
        o_ref[layer:layer + 1, :] = acc


def _hgrn_lower_bounds(hgrn_lb):
    return pl.pallas_call(
        _lb_body,
        out_shape=jax.ShapeDtypeStruct(hgrn_lb.shape, F32),
        name="hgrn_lb",
    )(hgrn_lb.astype(F32))


def _group_sums(x, p0, p1):
    rows, width = x.shape
    xg = x.reshape(rows // CMP_STRIDE, CMP_STRIDE, width)
    return jnp.sum(xg * p0[None], axis=1), jnp.sum(xg * p1[None], axis=1)


def _cmp_partial_body(x_ref, p0_ref, p1_ref, a0_ref, a1_ref):
    a0, a1 = _group_sums(x_ref[...], p0_ref[...], p1_ref[...])
    a0_ref[...] = a0
    a1_ref[...] = a1


def _cmp_partial(kvrows, p0, p1, *, rb):
    m = kvrows.shape[0]
    w = p0.shape[1]
    const = lambda i: (0, 0)
    out = jax.ShapeDtypeStruct((m // CMP_STRIDE, w), F32)
    return pl.pallas_call(
        _cmp_partial_body,
        grid=(m // rb,),
        in_specs=[pl.BlockSpec((rb, w), lambda i: (i, 0)), pl.BlockSpec(p0.shape, const), pl.BlockSpec(p1.shape, const)],
        out_specs=[pl.BlockSpec((rb // CMP_STRIDE, w), lambda i: (i, 0))] * 2,
        out_shape=[out, out],
        compiler_params=_cp(("parallel",)),
        name="cmp_partial",
    )(kvrows, p0, p1)


CACHE_SUB = 4 * NSA_KV_HEADS


def _page_group_sums(x_refs, p0_ref, p1_ref, a0_ref, a1_ref, page):
    gp = page // CMP_STRIDE
    for k, x_ref in enumerate(x_refs):
        for s in range(2 * NSA_KV_HEADS):
            sl = slice(s * HEAD_DIM, (s + 1) * HEAD_DIM)
            x = x_ref[pl.ds(s, page, stride=CACHE_SUB), :]
            a0, a1 = _group_sums(x, p0_ref[:, sl], p1_ref[:, sl])
            a0_ref[k * gp:(k + 1) * gp, sl] = a0
            a1_ref[k * gp:(k + 1) * gp, sl] = a1


def _cmp_partial_paged_body(pt_ref, *refs, pages, page):
    p0_ref, p1_ref, a0_ref, a1_ref = refs[pages:]
    _page_group_sums(refs[:pages], p0_ref, p1_ref, a0_ref, a1_ref, page)


def _proj_cache_sums_body(pt_ref, u_ref, w_ref, *refs, pages, page):
    p0_ref, p1_ref, o_ref, a0_ref, a1_ref = refs[pages:]
    o_ref[...] = _dot_nt(u_ref[...], w_ref[...])
    _page_group_sums(refs[:pages], p0_ref, p1_ref, a0_ref, a1_ref, page)


def _proj_with_cache_sums(u, w, li, cache_rows, page_table, p0, p1, *, n, tm, tn, pages):
    m, k = u.shape
    nd, n_pages = page_table.shape
    page = cache_rows.shape[2] // CACHE_SUB
    width = p0.shape[1]
    gp = page // CMP_STRIDE
    chunks = n_pages // pages
    units = nd * chunks
    ni, nj = m // tm, n // tn
    unit = lambda i, j: jnp.minimum(i * nj + j, units - 1)
    const = lambda i, j, pt: (0, 0)
    page_spec = lambda kk: pl.BlockSpec(
        (None, None, page * CACHE_SUB, HEAD_DIM),
        lambda i, j, pt: (li, pt[unit(i, j) // chunks, (unit(i, j) % chunks) * pages + kk], 0, 0))
    sums_spec = pl.BlockSpec((None, pages * gp, width), lambda i, j, pt: (unit(i, j) // chunks, unit(i, j) % chunks, 0))
    sums = jax.ShapeDtypeStruct((nd, n_pages * gp, width), F32)
    return pl.pallas_call(
        functools.partial(_proj_cache_sums_body, pages=pages, page=page),
        grid_spec=pltpu.PrefetchScalarGridSpec(
            num_scalar_prefetch=1,
            grid=(ni, nj),
            in_specs=[pl.BlockSpec((tm, k), lambda i, j, pt: (i, 0)),
                      pl.BlockSpec((None, tn, k), lambda i, j, pt: (li, j, 0))]
            + [page_spec(kk) for kk in range(pages)]
            + [pl.BlockSpec(p0.shape, const), pl.BlockSpec(p1.shape, const)],
            out_specs=[pl.BlockSpec((tm, tn), lambda i, j, pt: (i, j)), sums_spec, sums_spec],
        ),
        out_shape=[jax.ShapeDtypeStruct((m, n), F32), sums, sums],
        compiler_params=_cp(("arbitrary", "arbitrary")),
        name="proj_plain_cache_sums",
    )(page_table, u, w, *([cache_rows] * pages), p0, p1)


def _cache_sum_pages(m, tm, n, tn, page_table):
    nd, n_pages = page_table.shape
    steps = (m // tm) * (n // tn)
    for pages in (4, 8, 16, 32):
        if n_pages % pages == 0 and nd * (n_pages // pages) <= steps:
            return pages
    return None


def _cmp_partial_paged(cache_rows, page_table, p0, p1, li, *, pages):
    n, n_pages = page_table.shape
    page = cache_rows.shape[2] // CACHE_SUB
    w = p0.shape[1]
    gp = page // CMP_STRIDE
    const = lambda i, c, pt: (0, 0)
    page_spec = lambda k: pl.BlockSpec((None, None, page * CACHE_SUB, HEAD_DIM),
                                       lambda i, c, pt: (li, pt[i, c * pages + k], 0, 0))
    out = jax.ShapeDtypeStruct((n, n_pages * gp, w), F32)
    return pl.pallas_call(
        functools.partial(_cmp_partial_paged_body, pages=pages, page=page),
        grid_spec=pltpu.PrefetchScalarGridSpec(
            num_scalar_prefetch=1,
            grid=(n, n_pages // pages),
            in_specs=[page_spec(k) for k in range(pages)] + [pl.BlockSpec(p0.shape, const), pl.BlockSpec(p1.shape, const)],
            out_specs=[pl.BlockSpec((None, pages * gp, w), lambda i, c, pt: (i, c, 0))] * 2,
        ),
        out_shape=[out, out],
        compiler_params=_cp(("parallel", "arbitrary")),
        name="cmp_partial_paged",
    )(page_table, *([cache_rows] * pages), p0, p1)


def _cmp_final_body(a0_ref, a1_ref, wk_ref, wv_ref, gain_ref, kc_ref, vc_ref):
    ng = a0_ref.shape[0]
    agg = a0_ref[...] + pltpu.roll(a1_ref[...], ng - 1, 0)
    for g in range(NSA_KV_HEADS):
        sl = slice(g * HEAD_DIM, (g + 1) * HEAD_DIM)
        ak = agg[:, sl]
        av = agg[:, NSA_KV_HEADS * HEAD_DIM + g * HEAD_DIM:NSA_KV_HEADS * HEAD_DIM + (g + 1) * HEAD_DIM]
        kc_ref[:, sl] = (_rms(_dot(ak, wk_ref[...])) * gain_ref[...]).astype(kc_ref.dtype)
        vc_ref[:, sl] = _dot(av, wv_ref[...]).astype(vc_ref.dtype)


def _cmp_final(a0, a1, w_k, w_v, gain):
    n, ng, w = a0.shape
    const = lambda i: (0, 0)
    out = jax.ShapeDtypeStruct((n, ng, NSA_KV_HEADS * HEAD_DIM), BF16)
    return pl.pallas_call(
        _cmp_final_body,
        grid=(n,),
        in_specs=[
            pl.BlockSpec((None, ng, w), lambda i: (i, 0, 0)),
            pl.BlockSpec((None, ng, w), lambda i: (i, 0, 0)),
            pl.BlockSpec(w_k.shape, const),
            pl.BlockSpec(w_v.shape, const),
            pl.BlockSpec(gain.shape, const),
        ],
        out_specs=[pl.BlockSpec((None, ng, NSA_KV_HEADS * HEAD_DIM), lambda i: (i, 0, 0))] * 2,
        out_shape=[out, out],
        compiler_params=_cp(("parallel",)),
        name="cmp_final",
    )(a0, a1, w_k, w_v, gain)


def _overlap_matrix(n_cmp_pad, n_slc, width):
    ci = np.arange(n_cmp_pad)[:, None]
    sj = np.arange(width)[None, :]
    c_start = ci * CMP_STRIDE
    s_start = sj * SEL_BLOCK
    hit = ((c_start < s_start + SEL_BLOCK) & (c_start + CMP_BLOCK > s_start)
           & (ci < n_cmp_pad - 1) & (sj < n_slc))
    return jnp.asarray(hit, BF16)


def _block_scores(imp, qpos_i, ov, n_slc):
    hi, lo = _split_bf16(imp)
    score = jnp.dot(hi, ov, preferred_element_type=F32) + jnp.dot(lo, ov, preferred_element_type=F32)
    sj = lax.broadcasted_iota(jnp.int32, score.shape, 1)
    cur = qpos_i // SEL_BLOCK
    forced = (sj == 0) | (sj == cur) | (sj == cur - 1)
    score = jnp.where(forced, FORCE_SCORE, score)
    score = jnp.where(sj * SEL_BLOCK <= qpos_i, score, -1.0)
    return jnp.where(sj < n_slc, score, -2.0)


def _masked_softmax(s, allow):
    s = jnp.where(allow, s, NEG_INF)
    m = jnp.max(s, axis=-1, keepdims=True)
    e = jnp.where(allow, jnp.exp(s - m), 0.0)
    return e / jnp.maximum(jnp.sum(e, axis=-1, keepdims=True), 1e-30)


SEL_KEY_BLOCK = 256


def _nsa_prompt_body(q_ref, zg_ref, kc_ref, vc_ref, ov_ref, ks_ref, vs_ref, kw_ref, vw_ref, o_ref, *, t_len, tq):
    g = pl.program_id(1)
    t0 = pl.program_id(2) * tq
    n_cmp_pad = kc_ref.shape[0]
    n_slc = -(-t_len // SEL_BLOCK)
    hpg = HEADS_PER_GROUP
    qpos_i = t0 + lax.broadcasted_iota(jnp.int32, (tq, 1), 0)
    slopes = [jnp.where(g == 0, 2.0 ** -(r + 1), 2.0 ** -(r + 1 + hpg)) for r in range(hpg)]
    slopes2 = [sl * LOG2E for sl in slopes]
    qs = [q_ref[:, r * HEAD_DIM:(r + 1) * HEAD_DIM] for r in range(hpg)]

    ci = lax.broadcasted_iota(jnp.int32, (tq, n_cmp_pad), 1)
    cdist_i = qpos_i - (ci * CMP_STRIDE + CMP_BLOCK - 1)
    callow = (cdist_i >= 0) & (ci < n_cmp_pad - 1)
    cdist = cdist_i.astype(F32)
    kc = kc_ref[...]
    vc = vc_ref[...]
    imp = jnp.zeros((tq, n_cmp_pad), F32)
    o_cmp = []
    for r in range(hpg):
        p = _masked_softmax(_dot_nt(qs[r], kc) * ATT_SCALE - slopes[r] * cdist, callow)
        o_cmp.append(_dot(p, vc))
        imp = imp + p

    score = _block_scores(imp, qpos_i, ov_ref[...], n_slc)
    s_t = score.T[0:n_slc]
    jrow = lax.broadcasted_iota(jnp.int32, (n_slc, tq), 0)
    rank = jnp.zeros((n_slc, tq), jnp.int32)
    for jp in range(n_slc):
        row = s_t[jp:jp + 1, :]
        beats = (row > s_t) | ((row == s_t) & (jp < jrow))
        rank = rank + jnp.where(beats, 1, 0)
    sel_t = jnp.where((rank < N_SEL) & (s_t >= 0.0), 1.0, 0.0)
    sel_t = jnp.concatenate([sel_t, jnp.zeros((LANES - n_slc, tq), F32)], axis=0)
    sel = sel_t.T.astype(BF16)

    kb = SEL_KEY_BLOCK
    nkb = (t0 + tq + kb - 1) // kb

    def sel_step(i, carry):
        k0 = pl.multiple_of(i * kb, kb)
        kblk = ks_ref[pl.ds(k0, kb), :].astype(BF16)
        vblk = vs_ref[pl.ds(k0, kb), :].astype(BF16)
        kpos_i = k0 + lax.broadcasted_iota(jnp.int32, (1, kb), 1)
        ej = lax.broadcasted_iota(jnp.int32, (LANES, kb), 0)
        ec = lax.broadcasted_iota(jnp.int32, (LANES, kb), 1)
        expand = jnp.where(ej == (k0 + ec) // SEL_BLOCK, 1.0, 0.0).astype(BF16)
        allow = (jnp.dot(sel, expand, preferred_element_type=F32) > 0.5) & (kpos_i <= qpos_i)
        kpos = kpos_i.astype(F32)
        out = []
        for r in range(hpg):
            m_old, l_old, acc_old = carry[3 * r:3 * r + 3]
            s = jnp.where(allow, _dot_nt(qs[r], kblk) * (ATT_SCALE * LOG2E) + slopes2[r] * kpos, NEG_INF)
            m_new = jnp.maximum(m_old, jnp.max(s, axis=-1, keepdims=True))
            alpha = jnp.exp2(m_old - m_new)
            e = jnp.exp2(s - m_new)
            out += [m_new, alpha * l_old + jnp.sum(e, axis=-1, keepdims=True),
                    alpha * acc_old + _dot(e, vblk)]
        return tuple(out)

    init = (jnp.full((tq, 1), NEG_INF, F32), jnp.zeros((tq, 1), F32), jnp.zeros((tq, HEAD_DIM), F32)) * hpg
    fin = lax.fori_loop(0, nkb, sel_step, init)
    o_slc = [fin[3 * r + 2] / fin[3 * r + 1] for r in range(hpg)]

    wk = WINDOW + tq
    ws = pl.multiple_of(jnp.clip(t0 - WINDOW, 0, t_len - wk), LANES)
    kw = kw_ref[pl.ds(ws, wk), :].astype(BF16)
    vw = vw_ref[pl.ds(ws, wk), :].astype(BF16)
    wpos_i = ws + lax.broadcasted_iota(jnp.int32, (1, wk), 1)
    wdist_i = qpos_i - wpos_i
    wallow = (wdist_i >= 0) & (wdist_i <= WINDOW)
    wpos = wpos_i.astype(F32)
    gates = _sigmoid(zg_ref[...])
    for r in range(hpg):
        s = jnp.where(wallow, _dot_nt(qs[r], kw) * (ATT_SCALE * LOG2E) + slopes2[r] * wpos, NEG_INF)
        e = jnp.exp2(s - jnp.max(s, axis=-1, keepdims=True))
        o_win = _dot(e, vw) / jnp.sum(e, axis=-1, keepdims=True)
        gate = [jnp.where(g == 0, gates[:, 3 * r + b:3 * r + b + 1],
                          gates[:, 3 * (r + hpg) + b:3 * (r + hpg) + b + 1]) for b in range(3)]
        o = gate[0] * o_cmp[r] + gate[1] * o_slc[r] + gate[2] * o_win
        o_ref[:, r * HEAD_DIM:(r + 1) * HEAD_DIM] = o.astype(o_ref.dtype)


def _nsa_prompt(q, zplain, kc, vc, kvrows, winrows, *, n, t, tq):
    nt = t // tq
    gw = HEADS_PER_GROUP * HEAD_DIM
    n_cmp_pad = kc.shape[1]
    g_blocks = NSA_KV_HEADS
    qrow = lambda i, g, j: (i * nt + j, g)
    seq = lambda blk: (lambda i, g, j: (i, blk + g))
    ov = _overlap_matrix(n_cmp_pad, -(-t // SEL_BLOCK), LANES)
    return pl.pallas_call(
        functools.partial(_nsa_prompt_body, t_len=t, tq=tq),
        grid=(n, NSA_KV_HEADS, nt),
        in_specs=[
            pl.BlockSpec((tq, gw), qrow),
            pl.BlockSpec((tq, LANES), lambda i, g, j: (i * nt + j, PLAIN_GATE_BLK)),
            pl.BlockSpec((None, n_cmp_pad, HEAD_DIM), lambda i, g, j: (i, 0, g)),
            pl.BlockSpec((None, n_cmp_pad, HEAD_DIM), lambda i, g, j: (i, 0, g)),
            pl.BlockSpec(ov.shape, lambda i, g, j: (0, 0)),
            pl.BlockSpec((t, HEAD_DIM), seq(2 * g_blocks)),
            pl.BlockSpec((t, HEAD_DIM), seq(3 * g_blocks)),
            pl.BlockSpec((t, HEAD_DIM), seq(0)),
            pl.BlockSpec((t, HEAD_DIM), seq(g_blocks)),
        ],
        out_specs=pl.BlockSpec((tq, gw), qrow),
        out_shape=jax.ShapeDtypeStruct((n * t, NSA_WIDTH), BF16),
        compiler_params=_cp(("parallel", "parallel", "arbitrary")),
        name="nsa_prompt",
    )(q, zplain, kc, vc, ov, kvrows, kvrows, winrows, winrows)


KV_OFF = NSA_WIDTH
GATE_OFF = KV_OFF + KV_WIDTH
PLAIN_OFF = GATE_OFF + GATE_WIDTH
SLOT_WIDTH = NSA_KV_HEADS * HEAD_DIM


def _stacked_weights(w_in, cmp_w, w_branch, w_o, w_ple_proj, w_ple_gate):
    w_t = jnp.swapaxes(w_in, 1, 2).astype(BF16)
    tail = PLAIN_WIDTH - PLAIN_GATE_BLK * LANES - GATE_WIDTH
    w_gate = jnp.pad(w_t[:, GATE_OFF:PLAIN_OFF], ((0, 0), (0, tail), (0, 0)))
    return dict(
        w_head=w_t,
        w_plain=jnp.concatenate([w_t[:, PLAIN_OFF:], w_gate], axis=1),
        w_phi=cmp_w.astype(BF16),
        w_branch=w_branch.astype(BF16),
        w_o=w_o.astype(BF16),
        w_ple_proj=w_ple_proj.astype(BF16),
        w_ple_gate=w_ple_gate.astype(BF16),
    )


def _layer_weights(li, stacked, norms, qk_gain, cmp_pos, out_gain, lb_all):
    ones = jnp.ones((SLOT_WIDTH,), F32)
    zeros = jnp.zeros((SLOT_WIDTH,), F32)
    gain = lambda i: jnp.tile(qk_gain[li, i], NSA_KV_HEADS)
    row = lambda v: v.reshape(1, -1).astype(F32)
    cp = cmp_pos[li]
    half = lambda w, m: w[m * CMP_STRIDE:(m + 1) * CMP_STRIDE]
    pos = lambda m: jnp.concatenate([half(cp[0], m)] * NSA_KV_HEADS + [half(cp[1], m)] * NSA_KV_HEADS, axis=1)
    w = dict(stacked)
    w.update(
        li=li,
        norm=[row(norms[li, i]) for i in range(4)],
        q_gain=row(jnp.tile(qk_gain[li, 0], NSA_HEADS)),
        q_flag=jnp.ones((1, NSA_WIDTH), F32),
        kv_gain=row(jnp.concatenate([ones, ones, gain(1), ones])),
        kv_flag=row(jnp.concatenate([zeros, zeros, ones, zeros])),
        win_gain=row(jnp.concatenate([gain(2), ones])),
        win_flag=row(jnp.concatenate([ones, zeros])),
        cmp_p0=pos(0).astype(F32),
        cmp_p1=pos(1).astype(F32),
        w_phi_k=stacked["w_phi"][li, 0],
        w_phi_v=stacked["w_phi"][li, 1],
        kc_gain=row(qk_gain[li, 3]),
        ret_gain=row(out_gain[li, 0]),
        hgrn_gain=row(out_gain[li, 1]),
        lb=lb_all[li:li + 1],
    )
    return w


def _tiles(m):
    big = 512 if m % 512 == 0 else m
    small = 256 if m % 256 == 0 else m
    return big, small


def _projections(u, w, tm, row_bufs, cache=None):
    li, depth = w["li"], w["depth"]
    head = functools.partial(_proj, u, w["w_head"], li, tm=tm)
    q = head(col0=0, n=NSA_WIDTH, tn=512, out_dtype=BF16, gain=w["q_gain"], flag=w["q_flag"], name="proj_q")
    kvrows, kv_buf = head(col0=KV_OFF, n=4 * SLOT_WIDTH, tn=4 * SLOT_WIDTH, out_dtype=F32, gain=w["kv_gain"],
                          flag=w["kv_flag"], name="proj_kv", rows_out=(depth, row_bufs[0]))
    winrows, win_buf = head(col0=KV_OFF + 4 * SLOT_WIDTH, n=2 * SLOT_WIDTH, tn=2 * SLOT_WIDTH, out_dtype=F32,
                            gain=w["win_gain"], flag=w["win_flag"], name="proj_win", rows_out=(depth, row_bufs[1]))
    tn = PLAIN_TILE
    pages = None if cache is None else _cache_sum_pages(u.shape[0], tm, PLAIN_WIDTH, tn, cache[1])
    if pages is None:
        zplain = _proj(u, w["w_plain"], li, col0=0, n=PLAIN_WIDTH, tm=tm, tn=tn, out_dtype=F32, name="proj_plain")
        return q, kvrows, winrows, zplain, (kv_buf, win_buf), None
    zplain, a0, a1 = _proj_with_cache_sums(u, w["w_plain"], li, cache[0], cache[1], w["cmp_p0"], w["cmp_p1"],
                                           n=PLAIN_WIDTH, tm=tm, tn=tn, pages=pages)
    return q, kvrows, winrows, zplain, (kv_buf, win_buf), (a0, a1)


FFN_TILE = 512


def _layer_tail(h, p, w, ffn_weights, tm, ts, cast_from=None):
    h, u, w_bf = _ffn(h, w["norm"][2], ffn_weights, w["norm"][3], tm=tm, tf=FFN_TILE, cast_from=cast_from)
    return _ple(h, u, p, w["w_ple_gate"], w["w_ple_proj"], w["li"], tm=ts), w_bf


def _prompt_mix(x, w, row_bufs, ffn_weights, *, n, t, cache=None):
    tm, ts = _tiles(n * t)
    h, u, _ = _ffn(x, w["norm"][0], ffn_weights, w["norm"][1], tm=tm, tf=FFN_TILE)
    q, kvrows, winrows, zplain, row_bufs, cache_sums = _projections(
        u, w, 1024 if (n * t) % 1024 == 0 else tm, row_bufs, cache)
    a0, a1 = _cmp_partial(kvrows, w["cmp_p0"], w["cmp_p1"], rb=min(1024, t))
    ng = t // CMP_STRIDE
    kc, vc = _cmp_final(a0.reshape(n, ng, -1), a1.reshape(n, ng, -1), w["w_phi_k"], w["w_phi_v"], w["kc_gain"])
    o_nsa = _nsa_prompt(q, zplain, kc, vc, kvrows, winrows, n=n, t=t, tq=2 * LANES)
    o_ret, s_ret = _retention(zplain, w["ret_gain"], None, 0, n=n, t=t, c=LANES)
    o_hgrn, s_hgrn = _hgrn(zplain, w["lb"], w["hgrn_gain"], None, 0, n=n, t=t, c=LANES)
    h = _merge(o_nsa, o_ret, o_hgrn, zplain, h, w["w_branch"], w["w_o"], w["li"], tm=ts)
    return h, row_bufs, s_ret, s_hgrn, cache_sums


def _nsa_dec_cmp_body(q_ref, kc_ref, vc_ref, ov_ref, ocmp_ref, sel_ref, *, past, steps, n_slc):
    n, gb = q_ref.shape[0], q_ref.shape[1]
    nc = kc_ref.shape[1]
    hpg = HEADS_PER_GROUP
    rows = hpg * steps
    ri = lax.broadcasted_iota(jnp.int32, (rows, 1), 0)
    qpos_i = past + ri % steps
    ci = lax.broadcasted_iota(jnp.int32, (rows, nc), 1)
    dist_i = qpos_i - (ci * CMP_STRIDE + CMP_BLOCK - 1)
    allow = (dist_i >= 0) & (ci < nc - 1)
    dist = dist_i.astype(F32)
    imps = []
    for i in range(n):
        for g in range(gb):
            slope = jnp.exp2(-(ri // steps + 1 + g * hpg).astype(F32))
            sl = slice(g * HEAD_DIM, (g + 1) * HEAD_DIM)
            p = _masked_softmax(_dot_nt(q_ref[i, g], kc_ref[i, :, sl]) * ATT_SCALE - slope * dist, allow)
            ocmp_ref[i, g] = _dot(p, vc_ref[i, :, sl])
            imp = p[0:steps]
            for r in range(1, hpg):
                imp = imp + p[r * steps:(r + 1) * steps]
            imps.append(imp)
    imp_all = jnp.concatenate(imps, axis=0)

    width = ov_ref.shape[1]
    all_rows = n * gb * steps
    qpos_all = past + lax.broadcasted_iota(jnp.int32, (all_rows, 1), 0) % steps
    score = _block_scores(imp_all, qpos_all, ov_ref[...], n_slc)
    sj = lax.broadcasted_iota(jnp.int32, score.shape, 1)
    lane = lax.broadcasted_iota(jnp.int32, (all_rows, LANES), 1)
    picked = jnp.full((all_rows, LANES), -1, jnp.int32)
    for it in range(min(N_SEL, n_slc)):
        m = jnp.max(score, axis=-1, keepdims=True)
        idx = jnp.min(jnp.where(score == m, sj, width), axis=-1, keepdims=True)
        picked = jnp.where(lane == it, jnp.where(m >= 0.0, idx, -1), picked)
        score = jnp.where(sj == idx, -3.0, score)
    sel_ref[...] = picked


def _nsa_dec_cmp(q_rt, kc, vc, *, past, steps, n_slc):
    n, ng = kc.shape[0], kc.shape[1]
    gb = NSA_KV_HEADS
    ov = _overlap_matrix(ng, n_slc, -(-n_slc // LANES) * LANES)
    o_cmp, sel = pl.pallas_call(
        functools.partial(_nsa_dec_cmp_body, past=past, steps=steps, n_slc=n_slc),
        out_shape=[
            jax.ShapeDtypeStruct((n, gb, HEADS_PER_GROUP * steps, HEAD_DIM), F32),
            jax.ShapeDtypeStruct((n * gb * steps, LANES), jnp.int32),
        ],
        compiler_params=pltpu.CompilerParams(vmem_limit_bytes=VMEM_LIMIT),
        name="nsa_dec_cmp",
    )(q_rt, kc, vc, ov)
    return o_cmp, sel.reshape(n, gb, steps, LANES)


ROW_PAD = 8


def _nsa_dec_attend_body(pt_ref, sel_ref, q_ref, gate_ref, ocmp_ref, kvn_ref, wn_ref, wp_ref, *refs,
                         past, steps, n_sel, past_blocks):
    gb = NSA_KV_HEADS
    blk_refs = refs[:gb * n_sel]
    o_ref = refs[gb * n_sel]
    i = pl.program_id(0)
    t = pl.program_id(1)
    qpos = past + t
    head = jnp.minimum(lax.broadcasted_iota(jnp.int32, (ROW_PAD, 1), 0), HEADS_PER_GROUP - 1)
    new_i = lax.broadcasted_iota(jnp.int32, (1, steps), 1)
    new_allow = new_i <= t
    new_dist = (t - new_i).astype(F32)
    cols = n_sel * SEL_BLOCK
    col = lax.broadcasted_iota(jnp.int32, (1, cols), 1)
    col_pick = col // SEL_BLOCK
    head_cols = lambda ref, blk: ref[:, blk * HEAD_DIM:(blk + 1) * HEAD_DIM]

    def two_part_attention(s_a, allow_a, v_a, s_b, allow_b, v_b):
        s_a = jnp.where(allow_a, s_a, NEG_INF)
        s_b = jnp.where(allow_b, s_b, NEG_INF)
        m = jnp.maximum(jnp.max(s_a, axis=-1, keepdims=True), jnp.max(s_b, axis=-1, keepdims=True))
        e_a = jnp.where(allow_a, jnp.exp(s_a - m), 0.0)
        e_b = jnp.where(allow_b, jnp.exp(s_b - m), 0.0)
        l = jnp.sum(e_a, axis=-1, keepdims=True) + jnp.sum(e_b, axis=-1, keepdims=True)
        return (_dot(e_a, v_a) + _dot(e_b, v_b)) / l

    for g in range(gb):
        q = q_ref[g]
        slope = jnp.exp2(-(head + 1 + g * HEADS_PER_GROUP).astype(F32))
        picked = [blk_refs[g * n_sel + k] for k in range(n_sel)]
        kall = jnp.concatenate([r[pl.ds(2 * gb + g, SEL_BLOCK, stride=CACHE_SUB), :] for r in picked], axis=0)
        vall = jnp.concatenate([r[pl.ds(3 * gb + g, SEL_BLOCK, stride=CACHE_SUB), :] for r in picked], axis=0)
        base = ((i * gb + g) * steps + t) * n_sel
        blk = jnp.zeros((1, cols), jnp.int32)
        for k in range(n_sel):
            blk = jnp.where(col_pick == k, sel_ref[base + k], blk)
        dist_i = qpos - (blk * SEL_BLOCK + col % SEL_BLOCK)
        allow = (blk >= 0) & (blk < past_blocks) & (dist_i >= 0)
        s_sel = _dot_nt(q, kall) * ATT_SCALE - slope * dist_i.astype(F32)
        s_new = _dot_nt(q, head_cols(kvn_ref, 2 * gb + g)) * ATT_SCALE - slope * new_dist
        o_slc = two_part_attention(s_sel, allow, vall, s_new, new_allow, head_cols(kvn_ref, 3 * gb + g))

        wb = wp_ref.shape[0] // (2 * gb)
        kwp = wp_ref[pl.ds(g, wb, stride=2 * gb), :]
        vwp = wp_ref[pl.ds(gb + g, wb, stride=2 * gb), :]
        wdist_i = qpos - (past - wb + lax.broadcasted_iota(jnp.int32, (1, wb), 1))
        wallow = (wdist_i >= 0) & (wdist_i <= WINDOW)
        s_wp = _dot_nt(q, kwp) * ATT_SCALE - slope * wdist_i.astype(F32)
        s_wn = _dot_nt(q, head_cols(wn_ref, g)) * ATT_SCALE - slope * new_dist
        o_win = two_part_attention(s_wp, wallow, vwp, s_wn, new_allow, head_cols(wn_ref, gb + g))

        gates = _sigmoid(gate_ref[g])
        o = gates[:, 0:1] * ocmp_ref[g] + gates[:, 1:2] * o_slc + gates[:, 2:3] * o_win
        o_ref[g] = o.astype(o_ref.dtype)


def _nsa_dec_attend(page_table, sel_flat, q_tr, gates_tr, ocmp_tr, kvrows, winrows, cache_half, win_rows_view, li, *,
                    past, steps, n_sel):
    n = q_tr.shape[0]
    rows = steps * ROW_PAD
    gb = NSA_KV_HEADS
    halves = cache_half.shape[2]
    past_blocks = past // SEL_BLOCK
    step = lambda i, t, pt, sel: (i, 0, t, 0)
    new = lambda i, t, pt, sel: (i, 0)

    def gather(g, k):
        def index(i, t, pt, sel):
            j = jnp.clip(sel[((i * gb + g) * steps + t) * n_sel + k], 0, past_blocks - 1)
            return (li, pt[i, j // halves], j % halves, 0, 0)
        return pl.BlockSpec((None, None, None, SEL_BLOCK * CACHE_SUB, HEAD_DIM), index)

    in_specs = [
        pl.BlockSpec((None, gb, ROW_PAD, HEAD_DIM), step),
        pl.BlockSpec((None, gb, ROW_PAD, LANES), step),
        pl.BlockSpec((None, gb, ROW_PAD, HEAD_DIM), step),
        pl.BlockSpec((steps, kvrows.shape[1]), new),
        pl.BlockSpec((steps, winrows.shape[1]), new),
        pl.BlockSpec((None, None, win_rows_view.shape[2], HEAD_DIM), lambda i, t, pt, sel: (li, i, 0, 0)),
    ]
    in_specs += [gather(g, k) for g in range(gb) for k in range(n_sel)]
    return pl.pallas_call(
        functools.partial(_nsa_dec_attend_body, past=past, steps=steps, n_sel=n_sel, past_blocks=past_blocks),
        grid_spec=pltpu.PrefetchScalarGridSpec(
            num_scalar_prefetch=2,
            grid=(n, steps),
            in_specs=in_specs,
            out_specs=pl.BlockSpec((None, gb, ROW_PAD, HEAD_DIM), step),
        ),
        out_shape=jax.ShapeDtypeStruct((n, gb, rows, HEAD_DIM), BF16),
        compiler_params=_cp(("parallel", "arbitrary")),
        name="nsa_dec_attend",
    )(page_table, sel_flat, q_tr, gates_tr, ocmp_tr, kvrows, winrows, win_rows_view, *([cache_half] * (gb * n_sel)))


def _decode_ffn_a(x, w, w_gu, w_down):
    return _ffn(x, w["norm"][0], (w_gu, w_down), w["norm"][1], tm=x.shape[0], tf=FFN_TILE, cast_from=(w["li"], 0))


def _decode_mix(h, u, w, row_bufs, cache_rows, cache_half, win_rows_view, state_ret, state_hgrn, page_table,
                cache_sums, *, n, steps):
    m = n * steps
    li = w["li"]
    hpg, gb = HEADS_PER_GROUP, NSA_KV_HEADS
    past = page_table.shape[1] * (cache_rows.shape[2] // CACHE_SUB)
    n_slc = -(-(past + steps) // SEL_BLOCK)
    n_sel = min(N_SEL, n_slc)
    q, kvrows, winrows, zplain, row_bufs, _ = _projections(u, w, m, row_bufs)
    if cache_sums is None:
        cache_sums = _cmp_partial_paged(cache_rows, page_table, w["cmp_p0"], w["cmp_p1"], li, pages=8)
    kc, vc = _cmp_final(*cache_sums, w["w_phi_k"], w["w_phi_v"], w["kc_gain"])
    q5 = q.reshape(n, steps, gb, hpg, HEAD_DIM)
    q_rt = q5.transpose(0, 2, 3, 1, 4).reshape(n, gb, hpg * steps, HEAD_DIM)
    o_cmp, sel = _nsa_dec_cmp(q_rt, kc, vc, past=past, steps=steps, n_slc=n_slc)
    pad_heads = lambda a: jnp.pad(a, ((0, 0), (0, 0), (0, 0), (0, ROW_PAD - hpg), (0, 0)))
    rows = steps * ROW_PAD
    q_tr = pad_heads(q5.transpose(0, 2, 1, 3, 4)).reshape(n, gb, rows, HEAD_DIM)
    ocmp_tr = pad_heads(o_cmp.reshape(n, gb, hpg, steps, HEAD_DIM).transpose(0, 1, 3, 2, 4)).reshape(
        n, gb, rows, HEAD_DIM)
    zg = zplain[:, PLAIN_GATE_BLK * LANES:PLAIN_GATE_BLK * LANES + GATE_WIDTH].reshape(n, steps, gb, hpg, 3)
    gates_tr = pad_heads(zg.transpose(0, 2, 1, 3, 4)).reshape(n, gb, rows, 3)
    gates_tr = jnp.pad(gates_tr, ((0, 0), (0, 0), (0, 0), (0, LANES - 3)))
    sel_flat = sel[..., :n_sel].reshape(-1)
    o_tr = _nsa_dec_attend(page_table, sel_flat, q_tr, gates_tr, ocmp_tr, kvrows, winrows, cache_half, win_rows_view,
                           li, past=past, steps=steps, n_sel=n_sel)
    o_nsa = o_tr.reshape(n, gb, steps, ROW_PAD, HEAD_DIM)[:, :, :, :hpg].transpose(0, 2, 1, 3, 4).reshape(
        m, NSA_WIDTH)
    o_ret, s_ret = _retention(zplain, w["ret_gain"], state_ret, li, n=n, t=steps, c=steps)
    o_hgrn, s_hgrn = _hgrn(zplain, w["lb"], w["hgrn_gain"], state_hgrn, li, n=n, t=steps, c=steps)
    h = _merge(o_nsa, o_ret, o_hgrn, zplain, h, w["w_branch"], w["w_o"], li, tm=m)
    return h, row_bufs, s_ret, s_hgrn


def kernel(x_prompt, x_sample, cache_kv, cache_win, state_ret, state_hgrn, page_table, p_prompt, p_sample, norms,
           w_in, qk_gain, cmp_pos, cmp_w, out_gain, hgrn_lb, w_branch, w_o, w_gu, w_down, w_ple_proj, w_ple_gate):
    n_p, t_p, d = x_prompt.shape
    n_d, t_d, _ = x_sample.shape
    depth, n_pool, page = cache_kv.shape[:3]
    cache_rows = cache_kv.reshape(depth, n_pool, page * CACHE_SUB, HEAD_DIM)
    cache_half = cache_kv.reshape(depth, n_pool, page // SEL_BLOCK, SEL_BLOCK * CACHE_SUB, HEAD_DIM)
    wb = cache_win.shape[2]
    win_rows_view = cache_win.reshape(depth, n_d, wb * 2 * NSA_KV_HEADS, HEAD_DIM)
    lb_all = _hgrn_lower_bounds(hgrn_lb)
    stacked = _stacked_weights(w_in, cmp_w, w_branch, w_o, w_ple_proj, w_ple_gate)
    hp = x_prompt.reshape(n_p * t_p, d)
    hs = x_sample.reshape(n_d * t_d, d)
    bufs_p = bufs_s = (None, None)
    states = [[] for _ in range(4)]
    m_d = n_d * t_d
    tm_p, ts_p = _tiles(n_p * t_p)
    for li in range(depth):
        w = _layer_weights(li, stacked, norms, qk_gain, cmp_pos, out_gain, lb_all)
        w["depth"] = depth
        hs, us, ffn_a = _decode_ffn_a(hs, w, w_gu, w_down)
        hp, bufs_p, s_r, s_h, cache_sums = _prompt_mix(hp, w, bufs_p, ffn_a, n=n_p, t=t_p,
                                                       cache=(cache_rows, page_table))
        states[0].append(s_r)
        states[2].append(s_h)
        hs, bufs_s, s_r, s_h = _decode_mix(hs, us, w, bufs_s, cache_rows, cache_half, win_rows_view, state_ret,
                                           state_hgrn, page_table, cache_sums, n=n_d, steps=t_d)
        states[1].append(s_r)
        states[3].append(s_h)
        hs, ffn_b = _layer_tail(hs, p_sample[li].reshape(m_d, -1), w, (w_gu, w_down), m_d, m_d, cast_from=(li, 1))
        hp, _ = _layer_tail(hp, p_prompt[li].reshape(n_p * t_p, -1), w, ffn_b, tm_p, ts_p)
    gb = NSA_KV_HEADS
    kv_p = bufs_p[0].reshape(depth, n_p, t_p, 4, gb, HEAD_DIM)
    kv_s = bufs_s[0].reshape(depth, n_d, t_d, 4, gb, HEAD_DIM)
    win_p = bufs_p[1].reshape(depth, n_p, t_p, 2, gb, HEAD_DIM)[:, :, t_p - min(WINDOW, t_p):]
    win_all = jnp.concatenate([cache_win, bufs_s[1].reshape(depth, n_d, t_d, 2, gb, HEAD_DIM)], axis=2)
    win_s = win_all[:, :, wb + t_d - min(WINDOW, wb + t_d):]
    ret_p, ret_s, hg_p, hg_s = (jnp.stack(s) for s in states)
    return (hp.reshape(n_p, t_p, d), hs.reshape(n_d, t_d, d), kv_p, kv_s, win_p, win_s, ret_p, ret_s, hg_p, hg_s)
```

```python
import functools
import math

import jax
import jax.numpy as jnp
import numpy as np
from jax import lax
from jax.experimental import pallas as pl
from jax.experimental.pallas import tpu as pltpu

F32 = jnp.float32
BF16 = jnp.bfloat16

D_MODEL = 2048
HEAD_DIM = 128
NSA_HEADS = 8
NSA_KV_HEADS = 2
HEADS_PER_GROUP = NSA_HEADS // NSA_KV_HEADS
CMP_BLOCK = 32
CMP_STRIDE = 16
SEL_BLOCK = 64
N_SEL = 16
WINDOW = 512
RET_HEADS = 4
HGRN_HEADS = 4
D_FF = 5632
PLE_DIM = 256
NSA_WIDTH = NSA_HEADS * HEAD_DIM
RET_WIDTH = RET_HEADS * HEAD_DIM
HGRN_WIDTH = HGRN_HEADS * HEAD_DIM
KV_WIDTH = 6 * NSA_KV_HEADS * HEAD_DIM
GATE_WIDTH = 3 * NSA_HEADS
NEG_INF = -1e30
FORCE_SCORE = 1e6
EPS = 1e-6
ATT_SCALE = HEAD_DIM ** -0.5
LOG2E = math.log2(math.e)

LANES = 128
PLAIN_RET_BLK = 0
PLAIN_HGRN_BLK = (4 * RET_WIDTH) // LANES
PLAIN_MERGE_OFF = 4 * RET_WIDTH + 4 * HGRN_WIDTH
PLAIN_GATE_BLK = (PLAIN_MERGE_OFF + 3 * D_MODEL) // LANES
PLAIN_WIDTH = PLAIN_MERGE_OFF + 3 * D_MODEL + LANES
PLAIN_COL_TILES = 9
PLAIN_TILE = PLAIN_WIDTH // PLAIN_COL_TILES
assert PLAIN_TILE * PLAIN_COL_TILES == PLAIN_WIDTH and PLAIN_TILE % LANES == 0

VMEM_LIMIT = 56 * 1024 * 1024


def _cp(sem, vmem=VMEM_LIMIT):
    return pltpu.CompilerParams(dimension_semantics=sem, vmem_limit_bytes=vmem)


def _rms(x):
    return x * lax.rsqrt(jnp.mean(x * x, axis=-1, keepdims=True) + EPS)


def _dot(a, b):
    return jnp.dot(a.astype(BF16), b.astype(BF16), preferred_element_type=F32)


def _dot_nt(a, b):
    return lax.dot_general(a.astype(BF16), b.astype(BF16), (((1,), (1,)), ((), ())),
                           preferred_element_type=F32)


def _dot_tn(a, b):
    rows = a.shape[0]
    if rows % LANES:
        pad = LANES - rows % LANES
        a = jnp.concatenate([a, jnp.zeros((pad, a.shape[1]), a.dtype)], axis=0)
        b = jnp.concatenate([b, jnp.zeros((pad, b.shape[1]), b.dtype)], axis=0)
    return _dot(a.T, b)


def _split_bf16(x):
    hi = x.astype(BF16)
    lo = (x - hi.astype(F32)).astype(BF16)
    return hi, lo


def _sigmoid(x):
    return 1.0 / (1.0 + jnp.exp(-x))


def _ffn_body(x_ref, g1_ref, wg_ref, wv_ref, wd_ref, g2_ref, h_ref, u_ref, *rest, nf, emit):
    xn_sc, acc_sc = rest[-2:]
    j = pl.program_id(1)

    @pl.when(j == 0)
    def _():
        xn_sc[...] = (_rms(x_ref[...]) * g1_ref[...]).astype(BF16)
        acc_sc[...] = jnp.zeros_like(acc_sc)

    wg, wv, wd = wg_ref[...].astype(BF16), wv_ref[...].astype(BF16), wd_ref[...].astype(BF16)
    if emit:
        rest[0][...], rest[1][...], rest[2][...] = wg, wv, wd
    xn = xn_sc[...]
    g = jnp.dot(xn, wg, preferred_element_type=F32)
    v = jnp.dot(xn, wv, preferred_element_type=F32)
    a = (g * _sigmoid(g) * v).astype(BF16)
    acc_sc[...] += jnp.dot(a, wd, preferred_element_type=F32)

    @pl.when(j == nf - 1)
    def _():
        h = x_ref[...] + 0.5 * acc_sc[...]
        h_ref[...] = h
        u_ref[...] = (_rms(h) * g2_ref[...]).astype(BF16)


def _ffn(x, g1, weights, g2, *, tm, tf, cast_from=None):
    m, d = x.shape
    row = lambda i, j: (i, 0)
    const = lambda i, j: (0, 0)
    out_specs = [pl.BlockSpec((tm, d), row), pl.BlockSpec((tm, d), row)]
    out_shape = [jax.ShapeDtypeStruct((m, d), F32), jax.ShapeDtypeStruct((m, d), BF16)]
    w_out_specs = [pl.BlockSpec((d, tf), lambda i, j: (0, j)), pl.BlockSpec((d, tf), lambda i, j: (0, j)),
                   pl.BlockSpec((tf, d), lambda i, j: (j, 0))]
    if cast_from is None:
        w_args = list(weights)
        f = weights[2].shape[0]
        nf = f // tf
        w_specs = w_out_specs
    else:
        assert m == tm, "weight tiles are cast and emitted once, by the only row tile"
        li, which = cast_from
        w_gu, w_down = weights
        f = w_down.shape[2]
        nf = f // tf
        w_args = [w_gu, w_gu, w_down]
        w_specs = [pl.BlockSpec((None, None, d, tf), lambda i, j: (li, which, 0, j)),
                   pl.BlockSpec((None, None, d, tf), lambda i, j: (li, which, 0, j + nf)),
                   pl.BlockSpec((None, None, tf, d), lambda i, j: (li, which, j, 0))]
        out_specs += w_out_specs
        out_shape += [jax.ShapeDtypeStruct((d, f), BF16), jax.ShapeDtypeStruct((d, f), BF16),
                      jax.ShapeDtypeStruct((f, d), BF16)]
    out = pl.pallas_call(
        functools.partial(_ffn_body, nf=nf, emit=cast_from is not None),
        grid=(m // tm, nf),
        in_specs=[pl.BlockSpec((tm, d), row), pl.BlockSpec((1, d), const)] + w_specs + [pl.BlockSpec((1, d), const)],
        out_specs=out_specs,
        out_shape=out_shape,
        scratch_shapes=[pltpu.VMEM((tm, d), BF16), pltpu.VMEM((tm, d), F32)],
        compiler_params=_cp(("parallel", "arbitrary")),
        name="ffn",
    )(x, g1, *w_args, g2)
    return out[0], out[1], tuple(out[2:])


def _proj_plain_body(u_ref, w_ref, o_ref):
    o_ref[...] = _dot_nt(u_ref[...], w_ref[...]).astype(o_ref.dtype)


def _proj_norm_body(u_ref, w_ref, gain_ref, flag_ref, *refs, tn, rows_out):
    o_ref = refs[-2] if rows_out else refs[-1]
    z = _dot_nt(u_ref[...], w_ref[...])
    chunks = tn // LANES
    for c in range(chunks):
        sl = slice(c * LANES, (c + 1) * LANES)
        zc = z[:, sl]
        normed = _rms(zc) * gain_ref[:, sl]
        val = jnp.where(flag_ref[:, sl] > 0.5, normed, zc)
        o_ref[:, sl] = val.astype(o_ref.dtype)
        if rows_out:
            refs[-1][pl.ds(c, z.shape[0], stride=chunks), :] = val


def _proj(u, w, li, *, col0, n, tm, tn, out_dtype, gain=None, flag=None, name, rows_out=None):
    m, k = u.shape
    cb = col0 // tn
    in_specs = [pl.BlockSpec((tm, k), lambda i, j: (i, 0)), pl.BlockSpec((None, tn, k), lambda i, j: (li, cb + j, 0))]
    args = [u, w]
    out_specs = pl.BlockSpec((tm, tn), lambda i, j: (i, j))
    out_shape = jax.ShapeDtypeStruct((m, n), out_dtype)
    aliases = {}
    if gain is None:
        body = _proj_plain_body
    else:
        body = functools.partial(_proj_norm_body, tn=tn, rows_out=rows_out is not None)
        in_specs += [pl.BlockSpec((1, tn), lambda i, j: (0, j)), pl.BlockSpec((1, tn), lambda i, j: (0, j))]
        args += [gain, flag]
    if rows_out is not None:
        assert n == tn, "the row-major store needs the whole width in one column tile"
        depth, prev = rows_out
        chunks = n // LANES
        out_specs = [out_specs, pl.BlockSpec((None, tm * chunks, LANES), lambda i, j: (li, i, 0))]
        out_shape = [out_shape, jax.ShapeDtypeStruct((depth, m * chunks, LANES), F32)]
        if prev is not None:
            in_specs.append(pl.BlockSpec(memory_space=pl.ANY))
            args.append(prev)
            aliases = {len(args) - 1: 1}
    return pl.pallas_call(
        body,
        grid=(m // tm, n // tn),
        in_specs=in_specs,
        out_specs=out_specs,
        out_shape=out_shape,
        input_output_aliases=aliases,
        compiler_params=_cp(("parallel", "arbitrary")),
        name=name,
    )(*args)


def _merge_body(oa_ref, ob_ref, oc_ref, m0_ref, m1_ref, m2_ref, h_ref, wb_ref, wo_ref, out_ref):
    ya = jnp.dot(oa_ref[...], wb_ref[0:NSA_WIDTH, :], preferred_element_type=F32)
    yb = jnp.dot(ob_ref[...], wb_ref[NSA_WIDTH:NSA_WIDTH + RET_WIDTH, :], preferred_element_type=F32)
    yc = jnp.dot(oc_ref[...], wb_ref[NSA_WIDTH + RET_WIDTH:, :], preferred_element_type=F32)
    mixed = _sigmoid(m0_ref[...]) * ya + _sigmoid(m1_ref[...]) * yb + _sigmoid(m2_ref[...]) * yc
    out_ref[...] = h_ref[...] + jnp.dot(mixed.astype(BF16), wo_ref[...], preferred_element_type=F32)


def _merge(o_nsa, o_ret, o_hgrn, zplain, h, w_branch, w_o, li, *, tm):
    m, d = h.shape
    mb = PLAIN_MERGE_OFF // d
    row = lambda i: (i, 0)
    layer = lambda i: (li, 0, 0)
    return pl.pallas_call(
        _merge_body,
        grid=(m // tm,),
        in_specs=[
            pl.BlockSpec((tm, NSA_WIDTH), row),
            pl.BlockSpec((tm, RET_WIDTH), row),
            pl.BlockSpec((tm, HGRN_WIDTH), row),
            pl.BlockSpec((tm, d), lambda i: (i, mb)),
            pl.BlockSpec((tm, d), lambda i: (i, mb + 1)),
            pl.BlockSpec((tm, d), lambda i: (i, mb + 2)),
            pl.BlockSpec((tm, d), row),
            pl.BlockSpec((None,) + w_branch.shape[1:], layer),
            pl.BlockSpec((None,) + w_o.shape[1:], layer),
        ],
        out_specs=pl.BlockSpec((tm, d), row),
        out_shape=jax.ShapeDtypeStruct((m, d), F32),
        compiler_params=_cp(("parallel",)),
        name="merge",
    )(o_nsa, o_ret, o_hgrn, zplain, zplain, zplain, h, w_branch, w_o)


def _ple_body(h_ref, u_ref, p_ref, wg_ref, wp_ref, out_ref):
    gate = _sigmoid(jnp.dot(u_ref[...], wg_ref[...], preferred_element_type=F32))
    proj = jnp.dot(p_ref[...].astype(BF16), wp_ref[...], preferred_element_type=F32)
    out_ref[...] = h_ref[...] + gate * proj


def _ple(h, u, p, w_gate, w_proj, li, *, tm):
    m, d = h.shape
    row = lambda i: (i, 0)
    layer = lambda i: (li, 0, 0)
    return pl.pallas_call(
        _ple_body,
        grid=(m // tm,),
        in_specs=[
            pl.BlockSpec((tm, d), row),
            pl.BlockSpec((tm, d), row),
            pl.BlockSpec((tm, p.shape[1]), row),
            pl.BlockSpec((None,) + w_gate.shape[1:], layer),
            pl.BlockSpec((None,) + w_proj.shape[1:], layer),
        ],
        out_specs=pl.BlockSpec((tm, d), row),
        out_shape=jax.ShapeDtypeStruct((m, d), F32),
        compiler_params=_cp(("parallel",)),
        name="ple",
    )(h, u, p, w_gate, w_proj)


def _retention_tables(c):
    lg = np.log1p(-np.exp2(-5.0 - np.arange(RET_HEADS, dtype=np.float64)))
    i = np.arange(c, dtype=np.float64)
    diff = i[:, None] - i[None, :]
    dmat = np.where(diff >= 0, np.exp(lg[:, None, None] * np.maximum(diff, 0.0)), 0.0)
    q_dec = np.exp(lg[:, None] * (i + 1.0))[..., None] * np.ones((1, 1, HEAD_DIM))
    k_dec = np.exp(lg[:, None] * (c - 1.0 - i))[..., None] * np.ones((1, 1, HEAD_DIM))
    c_dec = np.exp(lg * c)[:, None, None] * np.ones((1, 8, HEAD_DIM))
    return tuple(jnp.asarray(a, F32) for a in (dmat, q_dec, k_dec, c_dec))


def _retention_body(*refs, has_state, nch, c):
    if has_state:
        q_ref, k_ref, v_ref, g_ref, dm_ref, qd_ref, kd_ref, cd_ref, gain_ref, s0_ref, o_ref, s_ref = refs
    else:
        q_ref, k_ref, v_ref, g_ref, dm_ref, qd_ref, kd_ref, cd_ref, gain_ref, o_ref, s_ref = refs
    heads = range(RET_HEADS)
    cols = lambda hh: slice(hh * HEAD_DIM, (hh + 1) * HEAD_DIM)

    def chunk(ci, states):
        rows = pl.ds(pl.multiple_of(ci * c, c), c)
        out = []
        for hh in heads:
            q = q_ref[rows, cols(hh)]
            k = k_ref[rows, cols(hh)] * ATT_SCALE
            v = v_ref[rows, cols(hh)]
            a = _dot_nt(q, k) * dm_ref[hh]
            o = _dot(a, v) + _dot(q * qd_ref[hh], states[hh])
            g = g_ref[rows, cols(hh)]
            o_ref[rows, cols(hh)] = (_rms(o) * gain_ref[:, cols(hh)] * (g * _sigmoid(g))).astype(o_ref.dtype)
            out.append(states[hh] * cd_ref[hh, 0:1, :] + _dot_tn(k * kd_ref[hh], v))
        return tuple(out)

    init = tuple(s0_ref[hh] if has_state else jnp.zeros((LANES, LANES), F32) for hh in heads)
    final = lax.fori_loop(0, nch, chunk, init)
    for hh in heads:
        s_ref[hh] = final[hh]


def _retention(zplain, gain, s0, li, *, n, t, c):
    nch = t // c
    tables = _retention_tables(c)
    hb = RET_HEADS
    width = hb * HEAD_DIM
    first = PLAIN_RET_BLK * LANES // width
    in_specs = [pl.BlockSpec((t, width), (lambda w: (lambda i: (i, first + w)))(w)) for w in range(4)]
    in_specs += [pl.BlockSpec(tab.shape, lambda i: (0, 0, 0)) for tab in tables]
    in_specs.append(pl.BlockSpec((1, width), lambda i: (0, 0)))
    args = [zplain] * 4 + list(tables) + [gain]
    if s0 is not None:
        in_specs.append(pl.BlockSpec((None, None, hb, LANES, LANES), lambda i: (li, i, 0, 0, 0)))
        args.append(s0)
    return pl.pallas_call(
        functools.partial(_retention_body, has_state=s0 is not None, nch=nch, c=c),
        grid=(n,),
        in_specs=in_specs,
        out_specs=[
            pl.BlockSpec((t, width), lambda i: (i, 0)),
            pl.BlockSpec((None, hb, LANES, LANES), lambda i: (i, 0, 0, 0)),
        ],
        out_shape=[
            jax.ShapeDtypeStruct((n * t, RET_WIDTH), BF16),
            jax.ShapeDtypeStruct((n, hb, LANES, LANES), F32),
        ],
        compiler_params=_cp(("parallel",)),
        name="retention",
    )(*args)


HGRN_SUB = 16


def _hgrn_body(*refs, has_state, nch, c):
    if has_state:
        q_ref, f_ref, v_ref, g_ref, lb_ref, gain_ref, s0_ref, o_ref, s_ref, st_sc, b_sc, k_sc, v_sc, o_sc = refs
    else:
        q_ref, f_ref, v_ref, g_ref, lb_ref, gain_ref, o_ref, s_ref, st_sc, b_sc, k_sc, v_sc, o_sc = refs
    ci = pl.program_id(1)
    heads = range(HGRN_HEADS)
    cols = lambda hh: slice(hh * HEAD_DIM, (hh + 1) * HEAD_DIM)

    @pl.when(ci == 0)
    def _():
        for hh in heads:
            st_sc[hh] = s0_ref[hh].T if has_state else jnp.zeros((LANES, LANES), F32)

    ri = lax.broadcasted_iota(jnp.int32, (c, c), 0)
    si = lax.broadcasted_iota(jnp.int32, (c, c), 1)
    tri = jnp.where(ri >= si, 1.0, 0.0).astype(BF16)
    q, kk, v, b, st = [], [], [], [], []
    for hh in heads:
        lb = lb_ref[:, cols(hh)]
        f = lb + (1.0 - lb) * _sigmoid(f_ref[:, cols(hh)])
        logf = jnp.log(f)
        hi = logf.astype(BF16)
        r1 = logf - hi.astype(F32)
        mid = r1.astype(BF16)
        lo = (r1 - mid.astype(F32)).astype(BF16)
        b.append(jnp.dot(tri, hi, preferred_element_type=F32) + jnp.dot(tri, mid, preferred_element_type=F32)
                 + jnp.dot(tri, lo, preferred_element_type=F32))
        q.append(q_ref[:, cols(hh)])
        kk.append(1.0 - f)
        v.append(v_ref[:, cols(hh)])
        b_sc[hh] = b[hh]
        k_sc[hh] = kk[hh]
        v_sc[hh] = v[hh]
        st.append(st_sc[hh])
        o_sc[hh] = _dot_nt(q[hh] * jnp.exp(b[hh]), st[hh])

    sub = min(HGRN_SUB, c)
    row_id = lax.broadcasted_iota(jnp.int32, (sub, LANES), 0)
    for blk in range(c // sub):
        r0 = blk * sub
        b_i = [b[hh][r0:r0 + sub] for hh in heads]
        q_i = [q[hh][r0:r0 + sub] for hh in heads]
        acc = [jnp.zeros((sub, LANES), F32) for _ in heads]
        if blk > 0:
            for hh in heads:
                ref = b_sc[hh, r0 - 1:r0, :]
                qt = q_i[hh] * jnp.exp(b_i[hh] - ref)
                kt = kk[hh][0:r0] * jnp.exp(ref - b[hh][0:r0])
                acc[hh] = _dot(_dot_nt(qt, kt), v[hh][0:r0])
        for s in range(sub):
            for hh in heads:
                b_s = b_sc[hh, r0 + s:r0 + s + 1, :]
                k_s = k_sc[hh, r0 + s:r0 + s + 1, :]
                v_s = v_sc[hh, r0 + s:r0 + s + 1, :]
                w = q_i[hh] * jnp.exp(jnp.minimum(b_i[hh] - b_s, 0.0)) * k_s
                w = jnp.where(row_id >= s, w, 0.0)
                acc[hh] = acc[hh] + jnp.sum(w, axis=-1, keepdims=True) * v_s
        for hh in heads:
            o_sc[hh, r0:r0 + sub, :] += acc[hh]

    for hh in heads:
        b_last = b_sc[hh, c - 1:c, :]
        st_sc[hh] = st[hh] * jnp.exp(b_last) + _dot_tn(v[hh], kk[hh] * jnp.exp(b_last - b[hh]))
        g = g_ref[:, cols(hh)]
        o_ref[:, cols(hh)] = (_rms(o_sc[hh]) * gain_ref[:, cols(hh)] * (g * _sigmoid(g))).astype(o_ref.dtype)

    @pl.when(ci == nch - 1)
    def _():
        for hh in heads:
            s_ref[hh] = st_sc[hh].T


def _hgrn(zplain, lb, gain, s0, li, *, n, t, c):
    nch = t // c
    hb = HGRN_HEADS
    width = hb * HEAD_DIM
    first = PLAIN_HGRN_BLK * LANES // width
    row = lambda w: (lambda i, j: (i * nch + j, first + w))
    const = lambda i, j: (0, 0)
    in_specs = [pl.BlockSpec((c, width), row(w)) for w in range(4)]
    in_specs += [pl.BlockSpec((1, width), const), pl.BlockSpec((1, width), const)]
    args = [zplain] * 4 + [lb, gain]
    if s0 is not None:
        in_specs.append(pl.BlockSpec((None, None, hb, LANES, LANES), lambda i, j: (li, i, 0, 0, 0)))
        args.append(s0)
    return pl.pallas_call(
        functools.partial(_hgrn_body, has_state=s0 is not None, nch=nch, c=c),
        grid=(n, nch),
        in_specs=in_specs,
        out_specs=[
            pl.BlockSpec((c, width), lambda i, j: (i * nch + j, 0)),
            pl.BlockSpec((None, hb, LANES, LANES), lambda i, j: (i, 0, 0, 0)),
        ],
        out_shape=[
            jax.ShapeDtypeStruct((n * t, HGRN_WIDTH), BF16),
            jax.ShapeDtypeStruct((n, hb, LANES, LANES), F32),
        ],
        scratch_shapes=[pltpu.VMEM((hb, LANES, LANES), F32)] + [pltpu.VMEM((hb, c, LANES), F32)] * 4,
        compiler_params=_cp(("parallel", "arbitrary")),
        name="hgrn",
    )(*args)


def _lb_body(x_ref, o_ref):
    x = x_ref[...]
    e = jnp.exp(x - jnp.max(x, axis=0, keepdims=True))
    sm = e / jnp.sum(e, axis=0, keepdims=True)
    acc = jnp.zeros_like(sm[0:1])
    o_ref[0:1, :] = acc
    for layer in range(1, x.shape[0]):
        acc = acc + sm[layer:layer + 1]
        o_ref[layer:layer + 1, :] = acc


def _hgrn_lower_bounds(hgrn_lb):
    return pl.pallas_call(
        _lb_body,
        out_shape=jax.ShapeDtypeStruct(hgrn_lb.shape, F32),
        name="hgrn_lb",
    )(hgrn_lb.astype(F32))


def _group_sums(x, p0, p1):
    rows, width = x.shape
    xg = x.reshape(rows // CMP_STRIDE, CMP_STRIDE, width)
    return jnp.sum(xg * p0[None], axis=1), jnp.sum(xg * p1[None], axis=1)


def _cmp_partial_body(x_ref, p0_ref, p1_ref, a0_ref, a1_ref):
    a0, a1 = _group_sums(x_ref[...], p0_ref[...], p1_ref[...])
    a0_ref[...] = a0
    a1_ref[...] = a1


def _cmp_partial(kvrows, p0, p1, *, rb):
    m = kvrows.shape[0]
    w = p0.shape[1]
    const = lambda i: (0, 0)
    out = jax.ShapeDtypeStruct((m // CMP_STRIDE, w), F32)
    return pl.pallas_call(
        _cmp_partial_body,
        grid=(m // rb,),
        in_specs=[pl.BlockSpec((rb, w), lambda i: (i, 0)), pl.BlockSpec(p0.shape, const), pl.BlockSpec(p1.shape, const)],
        out_specs=[pl.BlockSpec((rb // CMP_STRIDE, w), lambda i: (i, 0))] * 2,
        out_shape=[out, out],
        compiler_params=_cp(("parallel",)),
        name="cmp_partial",
    )(kvrows, p0, p1)


CACHE_SUB = 4 * NSA_KV_HEADS


def _page_group_sums(x_refs, p0_ref, p1_ref, a0_ref, a1_ref, page):
    gp = page // CMP_STRIDE
    for k, x_ref in enumerate(x_refs):
        for s in range(2 * NSA_KV_HEADS):
            sl = slice(s * HEAD_DIM, (s + 1) * HEAD_DIM)
            x = x_ref[pl.ds(s, page, stride=CACHE_SUB), :]
            a0, a1 = _group_sums(x, p0_ref[:, sl], p1_ref[:, sl])
            a0_ref[k * gp:(k + 1) * gp, sl] = a0
            a1_ref[k * gp:(k + 1) * gp, sl] = a1


def _cmp_partial_paged_body(pt_ref, *refs, pages, page):
    p0_ref, p1_ref, a0_ref, a1_ref = refs[pages:]
    _page_group_sums(refs[:pages], p0_ref, p1_ref, a0_ref, a1_ref, page)


def _proj_cache_sums_body(pt_ref, u_ref, w_ref, *refs, pages, page):
    p0_ref, p1_ref, o_ref, a0_ref, a1_ref = refs[pages:]
    o_ref[...] = _dot_nt(u_ref[...], w_ref[...])
    _page_group_sums(refs[:pages], p0_ref, p1_ref, a0_ref, a1_ref, page)


def _proj_with_cache_sums(u, w, li, cache_rows, page_table, p0, p1, *, n, tm, tn, pages):
    m, k = u.shape
    nd, n_pages = page_table.shape
    page = cache_rows.shape[2] // CACHE_SUB
    width = p0.shape[1]
    gp = page // CMP_STRIDE
    chunks = n_pages // pages
    units = nd * chunks
    ni, nj = m // tm, n // tn
    unit = lambda i, j: jnp.minimum(i * nj + j, units - 1)
    const = lambda i, j, pt: (0, 0)
    page_spec = lambda kk: pl.BlockSpec(
        (None, None, page * CACHE_SUB, HEAD_DIM),
        lambda i, j, pt: (li, pt[unit(i, j) // chunks, (unit(i, j) % chunks) * pages + kk], 0, 0))
    sums_spec = pl.BlockSpec((None, pages * gp, width), lambda i, j, pt: (unit(i, j) // chunks, unit(i, j) % chunks, 0))
    sums = jax.ShapeDtypeStruct((nd, n_pages * gp, width), F32)
    return pl.pallas_call(
        functools.partial(_proj_cache_sums_body, pages=pages, page=page),
        grid_spec=pltpu.PrefetchScalarGridSpec(
            num_scalar_prefetch=1,
            grid=(ni, nj),
            in_specs=[pl.BlockSpec((tm, k), lambda i, j, pt: (i, 0)),
                      pl.BlockSpec((None, tn, k), lambda i, j, pt: (li, j, 0))]
            + [page_spec(kk) for kk in range(pages)]
            + [pl.BlockSpec(p0.shape, const), pl.BlockSpec(p1.shape, const)],
            out_specs=[pl.BlockSpec((tm, tn), lambda i, j, pt: (i, j)), sums_spec, sums_spec],
        ),
        out_shape=[jax.ShapeDtypeStruct((m, n), F32), sums, sums],
        compiler_params=_cp(("arbitrary", "arbitrary")),
        name="proj_plain_cache_sums",
    )(page_table, u, w, *([cache_rows] * pages), p0, p1)


def _cache_sum_pages(m, tm, n, tn, page_table):
    nd, n_pages = page_table.shape
    steps = (m // tm) * (n // tn)
    for pages in (4, 8, 16, 32):
        if n_pages % pages == 0 and nd * (n_pages // pages) <= steps:
            return pages
    return None


def _cmp_partial_paged(cache_rows, page_table, p0, p1, li, *, pages):
    n, n_pages = page_table.shape
    page = cache_rows.shape[2] // CACHE_SUB
    w = p0.shape[1]
    gp = page // CMP_STRIDE
    const = lambda i, c, pt: (0, 0)
    page_spec = lambda k: pl.BlockSpec((None, None, page * CACHE_SUB, HEAD_DIM),
                                       lambda i, c, pt: (li, pt[i, c * pages + k], 0, 0))
    out = jax.ShapeDtypeStruct((n, n_pages * gp, w), F32)
    return pl.pallas_call(
        functools.partial(_cmp_partial_paged_body, pages=pages, page=page),
        grid_spec=pltpu.PrefetchScalarGridSpec(
            num_scalar_prefetch=1,
            grid=(n, n_pages // pages),
            in_specs=[page_spec(k) for k in range(pages)] + [pl.BlockSpec(p0.shape, const), pl.BlockSpec(p1.shape, const)],
            out_specs=[pl.BlockSpec((None, pages * gp, w), lambda i, c, pt: (i, c, 0))] * 2,
        ),
        out_shape=[out, out],
        compiler_params=_cp(("parallel", "arbitrary")),
        name="cmp_partial_paged",
    )(page_table, *([cache_rows] * pages), p0, p1)


def _cmp_final_body(a0_ref, a1_ref, wk_ref, wv_ref, gain_ref, kc_ref, vc_ref):
    ng = a0_ref.shape[0]
    agg = a0_ref[...] + pltpu.roll(a1_ref[...], ng - 1, 0)
    for g in range(NSA_KV_HEADS):
        sl = slice(g * HEAD_DIM, (g + 1) * HEAD_DIM)
        ak = agg[:, sl]
        av = agg[:, NSA_KV_HEADS * HEAD_DIM + g * HEAD_DIM:NSA_KV_HEADS * HEAD_DIM + (g + 1) * HEAD_DIM]
        kc_ref[:, sl] = (_rms(_dot(ak, wk_ref[...])) * gain_ref[...]).astype(kc_ref.dtype)
        vc_ref[:, sl] = _dot(av, wv_ref[...]).astype(vc_ref.dtype)


def _cmp_final(a0, a1, w_k, w_v, gain):
    n, ng, w = a0.shape
    const = lambda i: (0, 0)
    out = jax.ShapeDtypeStruct((n, ng, NSA_KV_HEADS * HEAD_DIM), BF16)
    return pl.pallas_call(
        _cmp_final_body,
        grid=(n,),
        in_specs=[
            pl.BlockSpec((None, ng, w), lambda i: (i, 0, 0)),
            pl.BlockSpec((None, ng, w), lambda i: (i, 0, 0)),
            pl.BlockSpec(w_k.shape, const),
            pl.BlockSpec(w_v.shape, const),
            pl.BlockSpec(gain.shape, const),
        ],
        out_specs=[pl.BlockSpec((None, ng, NSA_KV_HEADS * HEAD_DIM), lambda i: (i, 0, 0))] * 2,
        out_shape=[out, out],
        compiler_params=_cp(("parallel",)),
        name="cmp_final",
    )(a0, a1, w_k, w_v, gain)


def _overlap_matrix(n_cmp_pad, n_slc, width):
    ci = np.arange(n_cmp_pad)[:, None]
    sj = np.arange(width)[None, :]
    c_start = ci * CMP_STRIDE
    s_start = sj * SEL_BLOCK
    hit = ((c_start < s_start + SEL_BLOCK) & (c_start + CMP_BLOCK > s_start)
           & (ci < n_cmp_pad - 1) & (sj < n_slc))
    return jnp.asarray(hit, BF16)


def _block_scores(imp, qpos_i, ov, n_slc):
    hi, lo = _split_bf16(imp)
    score = jnp.dot(hi, ov, preferred_element_type=F32) + jnp.dot(lo, ov, preferred_element_type=F32)
    sj = lax.broadcasted_iota(jnp.int32, score.shape, 1)
    cur = qpos_i // SEL_BLOCK
    forced = (sj == 0) | (sj == cur) | (sj == cur - 1)
    score = jnp.where(forced, FORCE_SCORE, score)
    score = jnp.where(sj * SEL_BLOCK <= qpos_i, score, -1.0)
    return jnp.where(sj < n_slc, score, -2.0)


def _masked_softmax(s, allow):
    s = jnp.where(allow, s, NEG_INF)
    m = jnp.max(s, axis=-1, keepdims=True)
    e = jnp.where(allow, jnp.exp(s - m), 0.0)
    return e / jnp.maximum(jnp.sum(e, axis=-1, keepdims=True), 1e-30)


SEL_KEY_BLOCK = 256


def _nsa_prompt_body(q_ref, zg_ref, kc_ref, vc_ref, ov_ref, ks_ref, vs_ref, kw_ref, vw_ref, o_ref, *, t_len, tq):
    g = pl.program_id(1)
    t0 = pl.program_id(2) * tq
    n_cmp_pad = kc_ref.shape[0]
    n_slc = -(-t_len // SEL_BLOCK)
    hpg = HEADS_PER_GROUP
    qpos_i = t0 + lax.broadcasted_iota(jnp.int32, (tq, 1), 0)
    slopes = [jnp.where(g == 0, 2.0 ** -(r + 1), 2.0 ** -(r + 1 + hpg)) for r in range(hpg)]
    slopes2 = [sl * LOG2E for sl in slopes]
    qs = [q_ref[:, r * HEAD_DIM:(r + 1) * HEAD_DIM] for r in range(hpg)]

    ci = lax.broadcasted_iota(jnp.int32, (tq, n_cmp_pad), 1)
    cdist_i = qpos_i - (ci * CMP_STRIDE + CMP_BLOCK - 1)
    callow = (cdist_i >= 0) & (ci < n_cmp_pad - 1)
    cdist = cdist_i.astype(F32)
    kc = kc_ref[...]
    vc = vc_ref[...]
    imp = jnp.zeros((tq, n_cmp_pad), F32)
    o_cmp = []
    for r in range(hpg):
        p = _masked_softmax(_dot_nt(qs[r], kc) * ATT_SCALE - slopes[r] * cdist, callow)
        o_cmp.append(_dot(p, vc))
        imp = imp + p

    score = _block_scores(imp, qpos_i, ov_ref[...], n_slc)
    s_t = score.T[0:n_slc]
    jrow = lax.broadcasted_iota(jnp.int32, (n_slc, tq), 0)
    rank = jnp.zeros((n_slc, tq), jnp.int32)
    for jp in range(n_slc):
        row = s_t[jp:jp + 1, :]
        beats = (row > s_t) | ((row == s_t) & (jp < jrow))
        rank = rank + jnp.where(beats, 1, 0)
    sel_t = jnp.where((rank < N_SEL) & (s_t >= 0.0), 1.0, 0.0)
    sel_t = jnp.concatenate([sel_t, jnp.zeros((LANES - n_slc, tq), F32)], axis=0)
    sel = sel_t.T.astype(BF16)

    kb = SEL_KEY_BLOCK
    nkb = (t0 + tq + kb - 1) // kb

    def sel_step(i, carry):
        k0 = pl.multiple_of(i * kb, kb)
        kblk = ks_ref[pl.ds(k0, kb), :].astype(BF16)
        vblk = vs_ref[pl.ds(k0, kb), :].astype(BF16)
        kpos_i = k0 + lax.broadcasted_iota(jnp.int32, (1, kb), 1)
        ej = lax.broadcasted_iota(jnp.int32, (LANES, kb), 0)
        ec = lax.broadcasted_iota(jnp.int32, (LANES, kb), 1)
        expand = jnp.where(ej == (k0 + ec) // SEL_BLOCK, 1.0, 0.0).astype(BF16)
        allow = (jnp.dot(sel, expand, preferred_element_type=F32) > 0.5) & (kpos_i <= qpos_i)
        kpos = kpos_i.astype(F32)
        out = []
        for r in range(hpg):
            m_old, l_old, acc_old = carry[3 * r:3 * r + 3]
            s = jnp.where(allow, _dot_nt(qs[r], kblk) * (ATT_SCALE * LOG2E) + slopes2[r] * kpos, NEG_INF)
            m_new = jnp.maximum(m_old, jnp.max(s, axis=-1, keepdims=True))
            alpha = jnp.exp2(m_old - m_new)
            e = jnp.exp2(s - m_new)
            out += [m_new, alpha * l_old + jnp.sum(e, axis=-1, keepdims=True),
                    alpha * acc_old + _dot(e, vblk)]
        return tuple(out)

    init = (jnp.full((tq, 1), NEG_INF, F32), jnp.zeros((tq, 1), F32), jnp.zeros((tq, HEAD_DIM), F32)) * hpg
    fin = lax.fori_loop(0, nkb, sel_step, init)
    o_slc = [fin[3 * r + 2] / fin[3 * r + 1] for r in range(hpg)]

    wk = WINDOW + tq
    ws = pl.multiple_of(jnp.clip(t0 - WINDOW, 0, t_len - wk), LANES)
    kw = kw_ref[pl.ds(ws, wk), :].astype(BF16)
    vw = vw_ref[pl.ds(ws, wk), :].astype(BF16)
    wpos_i = ws + lax.broadcasted_iota(jnp.int32, (1, wk), 1)
    wdist_i = qpos_i - wpos_i
    wallow = (wdist_i >= 0) & (wdist_i <= WINDOW)
    wpos = wpos_i.astype(F32)
    gates = _sigmoid(zg_ref[...])
    for r in range(hpg):
        s = jnp.where(wallow, _dot_nt(qs[r], kw) * (ATT_SCALE * LOG2E) + slopes2[r] * wpos, NEG_INF)
        e = jnp.exp2(s - jnp.max(s, axis=-1, keepdims=True))
        o_win = _dot(e, vw) / jnp.sum(e, axis=-1, keepdims=True)
        gate = [jnp.where(g == 0, gates[:, 3 * r + b:3 * r + b + 1],
                          gates[:, 3 * (r + hpg) + b:3 * (r + hpg) + b + 1]) for b in range(3)]
        o = gate[0] * o_cmp[r] + gate[1] * o_slc[r] + gate[2] * o_win
        o_ref[:, r * HEAD_DIM:(r + 1) * HEAD_DIM] = o.astype(o_ref.dtype)


def _nsa_prompt(q, zplain, kc, vc, kvrows, winrows, *, n, t, tq):
    nt = t // tq
    gw = HEADS_PER_GROUP * HEAD_DIM
    n_cmp_pad = kc.shape[1]
    g_blocks = NSA_KV_HEADS
    qrow = lambda i, g, j: (i * nt + j, g)
    seq = lambda blk: (lambda i, g, j: (i, blk + g))
    ov = _overlap_matrix(n_cmp_pad, -(-t // SEL_BLOCK), LANES)
    return pl.pallas_call(
        functools.partial(_nsa_prompt_body, t_len=t, tq=tq),
        grid=(n, NSA_KV_HEADS, nt),
        in_specs=[
            pl.BlockSpec((tq, gw), qrow),
            pl.BlockSpec((tq, LANES), lambda i, g, j: (i * nt + j, PLAIN_GATE_BLK)),
            pl.BlockSpec((None, n_cmp_pad, HEAD_DIM), lambda i, g, j: (i, 0, g)),
            pl.BlockSpec((None, n_cmp_pad, HEAD_DIM), lambda i, g, j: (i, 0, g)),
            pl.BlockSpec(ov.shape, lambda i, g, j: (0, 0)),
            pl.BlockSpec((t, HEAD_DIM), seq(2 * g_blocks)),
            pl.BlockSpec((t, HEAD_DIM), seq(3 * g_blocks)),
            pl.BlockSpec((t, HEAD_DIM), seq(0)),
            pl.BlockSpec((t, HEAD_DIM), seq(g_blocks)),
        ],
        out_specs=pl.BlockSpec((tq, gw), qrow),
        out_shape=jax.ShapeDtypeStruct((n * t, NSA_WIDTH), BF16),
        compiler_params=_cp(("parallel", "parallel", "arbitrary")),
        name="nsa_prompt",
    )(q, zplain, kc, vc, ov, kvrows, kvrows, winrows, winrows)


KV_OFF = NSA_WIDTH
GATE_OFF = KV_OFF + KV_WIDTH
PLAIN_OFF = GATE_OFF + GATE_WIDTH
SLOT_WIDTH = NSA_KV_HEADS * HEAD_DIM


def _stacked_weights(w_in, cmp_w, w_branch, w_o, w_ple_proj, w_ple_gate):
    w_t = jnp.swapaxes(w_in, 1, 2).astype(BF16)
    tail = PLAIN_WIDTH - PLAIN_GATE_BLK * LANES - GATE_WIDTH
    w_gate = jnp.pad(w_t[:, GATE_OFF:PLAIN_OFF], ((0, 0), (0, tail), (0, 0)))
    return dict(
        w_head=w_t,
        w_plain=jnp.concatenate([w_t[:, PLAIN_OFF:], w_gate], axis=1),
        w_phi=cmp_w.astype(BF16),
        w_branch=w_branch.astype(BF16),
        w_o=w_o.astype(BF16),
        w_ple_proj=w_ple_proj.astype(BF16),
        w_ple_gate=w_ple_gate.astype(BF16),
    )


def _layer_weights(li, stacked, norms, qk_gain, cmp_pos, out_gain, lb_all):
    ones = jnp.ones((SLOT_WIDTH,), F32)
    zeros = jnp.zeros((SLOT_WIDTH,), F32)
    gain = lambda i: jnp.tile(qk_gain[li, i], NSA_KV_HEADS)
    row = lambda v: v.reshape(1, -1).astype(F32)
    cp = cmp_pos[li]
    half = lambda w, m: w[m * CMP_STRIDE:(m + 1) * CMP_STRIDE]
    pos = lambda m: jnp.concatenate([half(cp[0], m)] * NSA_KV_HEADS + [half(cp[1], m)] * NSA_KV_HEADS, axis=1)
    w = dict(stacked)
    w.update(
        li=li,
        norm=[row(norms[li, i]) for i in range(4)],
        q_gain=row(jnp.tile(qk_gain[li, 0], NSA_HEADS)),
        q_flag=jnp.ones((1, NSA_WIDTH), F32),
        kv_gain=row(jnp.concatenate([ones, ones, gain(1), ones])),
        kv_flag=row(jnp.concatenate([zeros, zeros, ones, zeros])),
        win_gain=row(jnp.concatenate([gain(2), ones])),
        win_flag=row(jnp.concatenate([ones, zeros])),
        cmp_p0=pos(0).astype(F32),
        cmp_p1=pos(1).astype(F32),
        w_phi_k=stacked["w_phi"][li, 0],
        w_phi_v=stacked["w_phi"][li, 1],
        kc_gain=row(qk_gain[li, 3]),
        ret_gain=row(out_gain[li, 0]),
        hgrn_gain=row(out_gain[li, 1]),
        lb=lb_all[li:li + 1],
    )
    return w


def _tiles(m):
    big = 512 if m % 512 == 0 else m
    small = 256 if m % 256 == 0 else m
    return big, small


def _projections(u, w, tm, row_bufs, cache=None):
    li, depth = w["li"], w["depth"]
    head = functools.partial(_proj, u, w["w_head"], li, tm=tm)
    q = head(col0=0, n=NSA_WIDTH, tn=512, out_dtype=BF16, gain=w["q_gain"], flag=w["q_flag"], name="proj_q")
    kvrows, kv_buf = head(col0=KV_OFF, n=4 * SLOT_WIDTH, tn=4 * SLOT_WIDTH, out_dtype=F32, gain=w["kv_gain"],
                          flag=w["kv_flag"], name="proj_kv", rows_out=(depth, row_bufs[0]))
    winrows, win_buf = head(col0=KV_OFF + 4 * SLOT_WIDTH, n=2 * SLOT_WIDTH, tn=2 * SLOT_WIDTH, out_dtype=F32,
                            gain=w["win_gain"], flag=w["win_flag"], name="proj_win", rows_out=(depth, row_bufs[1]))
    tn = PLAIN_TILE
    pages = None if cache is None else _cache_sum_pages(u.shape[0], tm, PLAIN_WIDTH, tn, cache[1])
    if pages is None:
        zplain = _proj(u, w["w_plain"], li, col0=0, n=PLAIN_WIDTH, tm=tm, tn=tn, out_dtype=F32, name="proj_plain")
        return q, kvrows, winrows, zplain, (kv_buf, win_buf), None
    zplain, a0, a1 = _proj_with_cache_sums(u, w["w_plain"], li, cache[0], cache[1], w["cmp_p0"], w["cmp_p1"],
                                           n=PLAIN_WIDTH, tm=tm, tn=tn, pages=pages)
    return q, kvrows, winrows, zplain, (kv_buf, win_buf), (a0, a1)


FFN_TILE = 512


def _layer_tail(h, p, w, ffn_weights, tm, cast_from=None):
    h, u, w_bf = _ffn(h, w["norm"][2], ffn_weights, w["norm"][3], tm=tm, tf=FFN_TILE, cast_from=cast_from)
    return _ple(h, u, p, w["w_ple_gate"], w["w_ple_proj"], w["li"], tm=tm), w_bf


def _prompt_mix(x, w, row_bufs, ffn_weights, *, n, t, cache=None):
    tm, ts = _tiles(n * t)
    h, u, _ = _ffn(x, w["norm"][0], ffn_weights, w["norm"][1], tm=tm, tf=FFN_TILE)
    q, kvrows, winrows, zplain, row_bufs, cache_sums = _projections(
        u, w, 1024 if (n * t) % 1024 == 0 else tm, row_bufs, cache)
    a0, a1 = _cmp_partial(kvrows, w["cmp_p0"], w["cmp_p1"], rb=min(1024, t))
    ng = t // CMP_STRIDE
    kc, vc = _cmp_final(a0.reshape(n, ng, -1), a1.reshape(n, ng, -1), w["w_phi_k"], w["w_phi_v"], w["kc_gain"])
    o_nsa = _nsa_prompt(q, zplain, kc, vc, kvrows, winrows, n=n, t=t, tq=2 * LANES)
    o_ret, s_ret = _retention(zplain, w["ret_gain"], None, 0, n=n, t=t, c=LANES)
    o_hgrn, s_hgrn = _hgrn(zplain, w["lb"], w["hgrn_gain"], None, 0, n=n, t=t, c=LANES)
    h = _merge(o_nsa, o_ret, o_hgrn, zplain, h, w["w_branch"], w["w_o"], w["li"], tm=ts)
    return h, row_bufs, s_ret, s_hgrn, cache_sums


def _nsa_dec_cmp_body(q_ref, kc_ref, vc_ref, ov_ref, ocmp_ref, sel_ref, *, past, steps, n_slc):
    n, gb = q_ref.shape[0], q_ref.shape[1]
    nc = kc_ref.shape[1]
    hpg = HEADS_PER_GROUP
    rows = hpg * steps
    ri = lax.broadcasted_iota(jnp.int32, (rows, 1), 0)
    qpos_i = past + ri % steps
    ci = lax.broadcasted_iota(jnp.int32, (rows, nc), 1)
    dist_i = qpos_i - (ci * CMP_STRIDE + CMP_BLOCK - 1)
    allow = (dist_i >= 0) & (ci < nc - 1)
    dist = dist_i.astype(F32)
    imps = []
    for i in range(n):
        for g in range(gb):
            slope = jnp.exp2(-(ri // steps + 1 + g * hpg).astype(F32))
            sl = slice(g * HEAD_DIM, (g + 1) * HEAD_DIM)
            p = _masked_softmax(_dot_nt(q_ref[i, g], kc_ref[i, :, sl]) * ATT_SCALE - slope * dist, allow)
            ocmp_ref[i, g] = _dot(p, vc_ref[i, :, sl])
            imp = p[0:steps]
            for r in range(1, hpg):
                imp = imp + p[r * steps:(r + 1) * steps]
            imps.append(imp)
    imp_all = jnp.concatenate(imps, axis=0)

    width = ov_ref.shape[1]
    all_rows = n * gb * steps
    qpos_all = past + lax.broadcasted_iota(jnp.int32, (all_rows, 1), 0) % steps
    score = _block_scores(imp_all, qpos_all, ov_ref[...], n_slc)
    sj = lax.broadcasted_iota(jnp.int32, score.shape, 1)
    lane = lax.broadcasted_iota(jnp.int32, (all_rows, LANES), 1)
    picked = jnp.full((all_rows, LANES), -1, jnp.int32)
    for it in range(min(N_SEL, n_slc)):
        m = jnp.max(score, axis=-1, keepdims=True)
        idx = jnp.min(jnp.where(score == m, sj, width), axis=-1, keepdims=True)
        picked = jnp.where(lane == it, jnp.where(m >= 0.0, idx, -1), picked)
        score = jnp.where(sj == idx, -3.0, score)
    sel_ref[...] = picked


def _nsa_dec_cmp(q_rt, kc, vc, *, past, steps, n_slc):
    n, ng = kc.shape[0], kc.shape[1]
    gb = NSA_KV_HEADS
    ov = _overlap_matrix(ng, n_slc, -(-n_slc // LANES) * LANES)
    o_cmp, sel = pl.pallas_call(
        functools.partial(_nsa_dec_cmp_body, past=past, steps=steps, n_slc=n_slc),
        out_shape=[
            jax.ShapeDtypeStruct((n, gb, HEADS_PER_GROUP * steps, HEAD_DIM), F32),
            jax.ShapeDtypeStruct((n * gb * steps, LANES), jnp.int32),
        ],
        compiler_params=pltpu.CompilerParams(vmem_limit_bytes=VMEM_LIMIT),
        name="nsa_dec_cmp",
    )(q_rt, kc, vc, ov)
    return o_cmp, sel.reshape(n, gb, steps, LANES)


ROW_PAD = 8


def _nsa_dec_attend_body(pt_ref, sel_ref, q_ref, gate_ref, ocmp_ref, kvn_ref, wn_ref, wp_ref, *refs,
                         past, steps, n_sel, past_blocks):
    gb = NSA_KV_HEADS
    blk_refs = refs[:gb * n_sel]
    o_ref = refs[gb * n_sel]
    i = pl.program_id(0)
    t = pl.program_id(1)
    qpos = past + t
    head = jnp.minimum(lax.broadcasted_iota(jnp.int32, (ROW_PAD, 1), 0), HEADS_PER_GROUP - 1)
    new_i = lax.broadcasted_iota(jnp.int32, (1, steps), 1)
    new_allow = new_i <= t
    new_dist = (t - new_i).astype(F32)
    cols = n_sel * SEL_BLOCK
    col = lax.broadcasted_iota(jnp.int32, (1, cols), 1)
    col_pick = col // SEL_BLOCK
    head_cols = lambda ref, blk: ref[:, blk * HEAD_DIM:(blk + 1) * HEAD_DIM]

    def two_part_attention(s_a, allow_a, v_a, s_b, allow_b, v_b):
        s_a = jnp.where(allow_a, s_a, NEG_INF)
        s_b = jnp.where(allow_b, s_b, NEG_INF)
        m = jnp.maximum(jnp.max(s_a, axis=-1, keepdims=True), jnp.max(s_b, axis=-1, keepdims=True))
        e_a = jnp.where(allow_a, jnp.exp(s_a - m), 0.0)
        e_b = jnp.where(allow_b, jnp.exp(s_b - m), 0.0)
        l = jnp.sum(e_a, axis=-1, keepdims=True) + jnp.sum(e_b, axis=-1, keepdims=True)
        return (_dot(e_a, v_a) + _dot(e_b, v_b)) / l

    for g in range(gb):
        q = q_ref[g]
        slope = jnp.exp2(-(head + 1 + g * HEADS_PER_GROUP).astype(F32))
        picked = [blk_refs[g * n_sel + k] for k in range(n_sel)]
        kall = jnp.concatenate([r[pl.ds(2 * gb + g, SEL_BLOCK, stride=CACHE_SUB), :] for r in picked], axis=0)
        vall = jnp.concatenate([r[pl.ds(3 * gb + g, SEL_BLOCK, stride=CACHE_SUB), :] for r in picked], axis=0)
        base = ((i * gb + g) * steps + t) * n_sel
        blk = jnp.zeros((1, cols), jnp.int32)
        for k in range(n_sel):
            blk = jnp.where(col_pick == k, sel_ref[base + k], blk)
        dist_i = qpos - (blk * SEL_BLOCK + col % SEL_BLOCK)
        allow = (blk >= 0) & (blk < past_blocks) & (dist_i >= 0)
        s_sel = _dot_nt(q, kall) * ATT_SCALE - slope * dist_i.astype(F32)
        s_new = _dot_nt(q, head_cols(kvn_ref, 2 * gb + g)) * ATT_SCALE - slope * new_dist
        o_slc = two_part_attention(s_sel, allow, vall, s_new, new_allow, head_cols(kvn_ref, 3 * gb + g))

        wb = wp_ref.shape[0] // (2 * gb)
        kwp = wp_ref[pl.ds(g, wb, stride=2 * gb), :]
        vwp = wp_ref[pl.ds(gb + g, wb, stride=2 * gb), :]
        wdist_i = qpos - (past - wb + lax.broadcasted_iota(jnp.int32, (1, wb), 1))
        wallow = (wdist_i >= 0) & (wdist_i <= WINDOW)
        s_wp = _dot_nt(q, kwp) * ATT_SCALE - slope * wdist_i.astype(F32)
        s_wn = _dot_nt(q, head_cols(wn_ref, g)) * ATT_SCALE - slope * new_dist
        o_win = two_part_attention(s_wp, wallow, vwp, s_wn, new_allow, head_cols(wn_ref, gb + g))

        gates = _sigmoid(gate_ref[g])
        o = gates[:, 0:1] * ocmp_ref[g] + gates[:, 1:2] * o_slc + gates[:, 2:3] * o_win
        o_ref[g] = o.astype(o_ref.dtype)


def _nsa_dec_attend(page_table, sel_flat, q_tr, gates_tr, ocmp_tr, kvrows, winrows, cache_half, win_rows_view, li, *,
                    past, steps, n_sel):
    n = q_tr.shape[0]
    rows = steps * ROW_PAD
    gb = NSA_KV_HEADS
    halves = cache_half.shape[2]
    past_blocks = past // SEL_BLOCK
    step = lambda i, t, pt, sel: (i, 0, t, 0)
    new = lambda i, t, pt, sel: (i, 0)

    def gather(g, k):
        def index(i, t, pt, sel):
            j = jnp.clip(sel[((i * gb + g) * steps + t) * n_sel + k], 0, past_blocks - 1)
            return (li, pt[i, j // halves], j % halves, 0, 0)
        return pl.BlockSpec((None, None, None, SEL_BLOCK * CACHE_SUB, HEAD_DIM), index)

    in_specs = [
        pl.BlockSpec((None, gb, ROW_PAD, HEAD_DIM), step),
        pl.BlockSpec((None, gb, ROW_PAD, LANES), step),
        pl.BlockSpec((None, gb, ROW_PAD, HEAD_DIM), step),
        pl.BlockSpec((steps, kvrows.shape[1]), new),
        pl.BlockSpec((steps, winrows.shape[1]), new),
        pl.BlockSpec((None, None, win_rows_view.shape[2], HEAD_DIM), lambda i, t, pt, sel: (li, i, 0, 0)),
    ]
    in_specs += [gather(g, k) for g in range(gb) for k in range(n_sel)]
    return pl.pallas_call(
        functools.partial(_nsa_dec_attend_body, past=past, steps=steps, n_sel=n_sel, past_blocks=past_blocks),
        grid_spec=pltpu.PrefetchScalarGridSpec(
            num_scalar_prefetch=2,
            grid=(n, steps),
            in_specs=in_specs,
            out_specs=pl.BlockSpec((None, gb, ROW_PAD, HEAD_DIM), step),
        ),
        out_shape=jax.ShapeDtypeStruct((n, gb, rows, HEAD_DIM), BF16),
        compiler_params=_cp(("parallel", "arbitrary")),
        name="nsa_dec_attend",
    )(page_table, sel_flat, q_tr, gates_tr, ocmp_tr, kvrows, winrows, win_rows_view, *([cache_half] * (gb * n_sel)))


def _decode_ffn_a(x, w, w_gu, w_down):
    return _ffn(x, w["norm"][0], (w_gu, w_down), w["norm"][1], tm=x.shape[0], tf=FFN_TILE, cast_from=(w["li"], 0))


def _decode_mix(h, u, w, row_bufs, cache_rows, cache_half, win_rows_view, state_ret, state_hgrn, page_table,
                cache_sums, *, n, steps):
    m = n * steps
    li = w["li"]
    hpg, gb = HEADS_PER_GROUP, NSA_KV_HEADS
    past = page_table.shape[1] * (cache_rows.shape[2] // CACHE_SUB)
    n_slc = -(-(past + steps) // SEL_BLOCK)
    n_sel = min(N_SEL, n_slc)
    q, kvrows, winrows, zplain, row_bufs, _ = _projections(u, w, m, row_bufs)
    if cache_sums is None:
        cache_sums = _cmp_partial_paged(cache_rows, page_table, w["cmp_p0"], w["cmp_p1"], li, pages=8)
    kc, vc = _cmp_final(*cache_sums, w["w_phi_k"], w["w_phi_v"], w["kc_gain"])
    q5 = q.reshape(n, steps, gb, hpg, HEAD_DIM)
    q_rt = q5.transpose(0, 2, 3, 1, 4).reshape(n, gb, hpg * steps, HEAD_DIM)
    o_cmp, sel = _nsa_dec_cmp(q_rt, kc, vc, past=past, steps=steps, n_slc=n_slc)
    pad_heads = lambda a: jnp.pad(a, ((0, 0), (0, 0), (0, 0), (0, ROW_PAD - hpg), (0, 0)))
    rows = steps * ROW_PAD
    q_tr = pad_heads(q5.transpose(0, 2, 1, 3, 4)).reshape(n, gb, rows, HEAD_DIM)
    ocmp_tr = pad_heads(o_cmp.reshape(n, gb, hpg, steps, HEAD_DIM).transpose(0, 1, 3, 2, 4)).reshape(
        n, gb, rows, HEAD_DIM)
    zg = zplain[:, PLAIN_GATE_BLK * LANES:PLAIN_GATE_BLK * LANES + GATE_WIDTH].reshape(n, steps, gb, hpg, 3)
    gates_tr = pad_heads(zg.transpose(0, 2, 1, 3, 4)).reshape(n, gb, rows, 3)
    gates_tr = jnp.pad(gates_tr, ((0, 0), (0, 0), (0, 0), (0, LANES - 3)))
    sel_flat = sel[..., :n_sel].reshape(-1)
    o_tr = _nsa_dec_attend(page_table, sel_flat, q_tr, gates_tr, ocmp_tr, kvrows, winrows, cache_half, win_rows_view,
                           li, past=past, steps=steps, n_sel=n_sel)
    o_nsa = o_tr.reshape(n, gb, steps, ROW_PAD, HEAD_DIM)[:, :, :, :hpg].transpose(0, 2, 1, 3, 4).reshape(
        m, NSA_WIDTH)
    o_ret, s_ret = _retention(zplain, w["ret_gain"], state_ret, li, n=n, t=steps, c=steps)
    o_hgrn, s_hgrn = _hgrn(zplain, w["lb"], w["hgrn_gain"], state_hgrn, li, n=n, t=steps, c=steps)
    h = _merge(o_nsa, o_ret, o_hgrn, zplain, h, w["w_branch"], w["w_o"], li, tm=m)
    return h, row_bufs, s_ret, s_hgrn


def kernel(x_prompt, x_sample, cache_kv, cache_win, state_ret, state_hgrn, page_table, p_prompt, p_sample, norms,
           w_in, qk_gain, cmp_pos, cmp_w, out_gain, hgrn_lb, w_branch, w_o, w_gu, w_down, w_ple_proj, w_ple_gate):
    n_p, t_p, d = x_prompt.shape
    n_d, t_d, _ = x_sample.shape
    depth, n_pool, page = cache_kv.shape[:3]
    cache_rows = cache_kv.reshape(depth, n_pool, page * CACHE_SUB, HEAD_DIM)
    cache_half = cache_kv.reshape(depth, n_pool, page // SEL_BLOCK, SEL_BLOCK * CACHE_SUB, HEAD_DIM)
    wb = cache_win.shape[2]
    win_rows_view = cache_win.reshape(depth, n_d, wb * 2 * NSA_KV_HEADS, HEAD_DIM)
    lb_all = _hgrn_lower_bounds(hgrn_lb)
    stacked = _stacked_weights(w_in, cmp_w, w_branch, w_o, w_ple_proj, w_ple_gate)
    hp = x_prompt.reshape(n_p * t_p, d)
    hs = x_sample.reshape(n_d * t_d, d)
    bufs_p = bufs_s = (None, None)
    states = [[] for _ in range(4)]
    m_d = n_d * t_d
    tm_p, _ = _tiles(n_p * t_p)
    for li in range(depth):
        w = _layer_weights(li, stacked, norms, qk_gain, cmp_pos, out_gain, lb_all)
        w["depth"] = depth
        hs, us, ffn_a = _decode_ffn_a(hs, w, w_gu, w_down)
        hp, bufs_p, s_r, s_h, cache_sums = _prompt_mix(hp, w, bufs_p, ffn_a, n=n_p, t=t_p,
                                                       cache=(cache_rows, page_table))
        states[0].append(s_r)
        states[2].append(s_h)
        hs, bufs_s, s_r, s_h = _decode_mix(hs, us, w, bufs_s, cache_rows, cache_half, win_rows_view, state_ret,
                                           state_hgrn, page_table, cache_sums, n=n_d, steps=t_d)
        states[1].append(s_r)
        states[3].append(s_h)
        hs, ffn_b = _layer_tail(hs, p_sample[li].reshape(m_d, -1), w, (w_gu, w_down), m_d, cast_from=(li, 1))
        hp, _ = _layer_tail(hp, p_prompt[li].reshape(n_p * t_p, -1), w, ffn_b, tm_p)
    gb = NSA_KV_HEADS
    kv_p = bufs_p[0].reshape(depth, n_p, t_p, 4, gb, HEAD_DIM)
    kv_s = bufs_s[0].reshape(depth, n_d, t_d, 4, gb, HEAD_DIM)
    win_p = bufs_p[1].reshape(depth, n_p, t_p, 2, gb, HEAD_DIM)[:, :, t_p - min(WINDOW, t_p):]
    win_all = jnp.concatenate([cache_win, bufs_s[1].reshape(depth, n_d, t_d, 2, gb, HEAD_DIM)], axis=2)
    win_s = win_all[:, :, wb + t_d - min(WINDOW, wb + t_d):]
    ret_p, ret_s, hg_p, hg_s = (jnp.stack(s) for s in states)
    return (hp.reshape(n_p, t_p, d), hs.reshape(n_d, t_d, d), kv_p, kv_s, win_p, win_s, ret_p, ret_s, hg_p, hg_s)
```
